```python
import math
import jax
import jax.numpy as jnp
from jax import lax
import numpy as np

D_MODEL = 1024
BATCH = 8
SEQ = 2048
DEPTH = 2
DEC_BATCH = 32
DEC_SEQ = 4
PAST_LEN = 16384
PAGE_SIZE = 128

N_A_LAYERS = DEPTH // 2
N_B_LAYERS = DEPTH - N_A_LAYERS
N_DENSE = (DEPTH + 1) // 2
N_MOE = DEPTH // 2

CONV_WIDTH = 31
N_HEADS = 16
HEAD_DIM = 64
N_KV_HEADS = 4
GROUP = N_HEADS // N_KV_HEADS
BLOCK = 64
TOP_N = 16
WINDOW = 512
CMP_HIDDEN = 2 * HEAD_DIM
FORCE_SCORE = 1e4
Q_BLOCK = 32
N_BUCKETS = 32
MAX_EXACT = N_BUCKETS // 2
MAX_DISTANCE = 128
D_FF = 2816
N_EXPERTS = 8
TOP_K_EXPERTS = 2
D_FF_EXPERT = 2816
KV_SLOTS = 6
PAGED_SLOTS = 4
EPS = 1e-6
NEG = -1e30
TINY = 1e-30

kernel_name = "yoco_conformer_conv_nsa_decoder_step"


def rmsnorm(x, g):
    x32 = x.astype(jnp.float32)
    y = x32 * lax.rsqrt(jnp.mean(x32 * x32, axis=-1, keepdims=True) + EPS)
    return (y * g.astype(jnp.float32)).astype(x.dtype)


def layernorm(x, g, b):
    x32 = x.astype(jnp.float32)
    mu = jnp.mean(x32, axis=-1, keepdims=True)
    xc = x32 - mu
    y = xc * lax.rsqrt(jnp.mean(xc * xc, axis=-1, keepdims=True) + EPS)
    return (y * g.astype(jnp.float32) + b.astype(jnp.float32)).astype(x.dtype)


def rel_bucket(dist):
    n = jnp.maximum(dist, 0)
    nf = jnp.maximum(n, 1).astype(jnp.float32)
    large = MAX_EXACT + (jnp.log(nf / MAX_EXACT) / math.log(MAX_DISTANCE / MAX_EXACT)
                         * (N_BUCKETS - MAX_EXACT)).astype(jnp.int32)
    return jnp.where(n < MAX_EXACT, n, jnp.minimum(large, N_BUCKETS - 1))


def bias_2d(rel_bias, q_pos, k_pos):
    b = rel_bias.astype(jnp.float32)[rel_bucket(q_pos[:, None] - k_pos[None, :])]
    return jnp.transpose(b, (2, 0, 1)).reshape(N_KV_HEADS, GROUP, q_pos.shape[0], k_pos.shape[0])


def masked_softmax(s, mask):
    s = jnp.where(mask, s, NEG)
    m = jnp.max(s, axis=-1, keepdims=True)
    p = jnp.where(mask, jnp.exp(s - m), 0.0)
    return p / jnp.maximum(jnp.sum(p, axis=-1, keepdims=True), TINY)


def conv_module(xn, prefix, w1, b1, wdw, bdw, lng, lnb, w2, b2):
    u = xn @ w1 + b1
    u = u[..., :D_MODEL] * jax.nn.sigmoid(u[..., D_MODEL:])
    full = jnp.concatenate([prefix, u], axis=1)
    c = lax.conv_general_dilated(full, wdw[:, None, :], (1,), 'VALID',
                                 dimension_numbers=('NWC', 'WIO', 'NWC'),
                                 feature_group_count=D_MODEL) + bdw
    c = jax.nn.silu(layernorm(c, lng, lnb))
    return c @ w2 + b2, full[:, full.shape[1] - (CONV_WIDTH - 1):]


def swiglu(x, wg, wu, wd):
    return (jax.nn.silu(x @ wg) * (x @ wu)) @ wd


def moe_ffn(x, w_router, wg, wu, wd):
    B, T, D = x.shape
    xf = x.reshape(B * T, D)
    logits = (xf @ w_router).astype(jnp.float32)
    top_v, top_i = lax.top_k(logits, TOP_K_EXPERTS)
    top_w = jax.nn.softmax(top_v, axis=-1)
    combine = jnp.sum(jax.nn.one_hot(top_i, N_EXPERTS, dtype=jnp.float32) * top_w[..., None], axis=1)
    out = jnp.zeros_like(xf)
    for e in range(N_EXPERTS):
        out = out + combine[:, e:e + 1].astype(x.dtype) * swiglu(xf, wg[e], wu[e], wd[e])
    return out.reshape(B, T, D)


def compress(rows, pos_emb, w1, w2):
    B, L, KVH, dh = rows.shape
    nc = L // BLOCK
    blk = rows.reshape(B, nc, BLOCK, KVH, dh) + pos_emb[None, None, :, None, :]
    flat = jnp.transpose(blk, (0, 1, 3, 2, 4)).reshape(B, nc, KVH, BLOCK * dh)
    return jax.nn.silu(flat @ w1) @ w2


def shared_kv(h, past_kv, win_prefix, keep, norm_kv, w_kv, cmp_pos, cmp_w1, cmp_w2):
    B, T, _ = h.shape
    kv = (rmsnorm(h, norm_kv) @ w_kv).reshape(B, T, KV_SLOTS, N_KV_HEADS, HEAD_DIM)
    new_rows = kv[:, :, :PAGED_SLOTS]
    full = jnp.concatenate([past_kv, new_rows], axis=1)
    tk = full.shape[1]
    nc = tk // BLOCK
    ns = -(-tk // BLOCK)
    kc = compress(full[:, :nc * BLOCK, 0], cmp_pos[0], cmp_w1[0], cmp_w2[0])
    vc = compress(full[:, :nc * BLOCK, 1], cmp_pos[1], cmp_w1[1], cmp_w2[1])
    sel = jnp.pad(full[:, :, 2:4], ((0, 0), (0, ns * BLOCK - tk), (0, 0), (0, 0), (0, 0)))
    sel = jnp.transpose(sel.reshape(B, ns, BLOCK, 2, N_KV_HEADS, HEAD_DIM), (3, 0, 4, 1, 2, 5))
    kw_all = jnp.concatenate([win_prefix, kv[:, :, 4:]], axis=1)
    new_win = kw_all[:, kw_all.shape[1] - keep:]
    return (kc, vc, sel[0], sel[1], kw_all), new_rows, new_win


def nsa_attend(q, g, q_pos, kc, vc, ks, vs, kw, vw, kw_pos, rel_bias):
    B, Tq = q.shape[0], q.shape[1]
    f32 = jnp.float32
    nc = kc.shape[1]
    ns = ks.shape[2]
    cmp_end = jnp.arange(nc, dtype=jnp.int32) * BLOCK + (BLOCK - 1)
    s_c = jnp.einsum('bqkgd,bnkd->bkgqn', q, kc).astype(f32) + bias_2d(rel_bias, q_pos, cmp_end)
    p_c = masked_softmax(s_c, cmp_end[None, :] <= q_pos[:, None])
    o_c = jnp.einsum('bkgqn,bnkd->bqkgd', p_c.astype(vc.dtype), vc)
    imp = jnp.pad(jnp.sum(p_c, axis=2), ((0, 0), (0, 0), (0, 0), (0, ns - nc)))
    blk = jnp.arange(ns, dtype=jnp.int32)[None, :]
    cur = (q_pos // BLOCK)[:, None]
    forced = (blk == cur) | (blk == cur - 1) | (blk == 0)
    score = jnp.where(blk > cur, NEG, jnp.where(forced, FORCE_SCORE, imp))
    _, idx = lax.top_k(score, min(TOP_N, ns))
    bi = jnp.arange(B)[:, None, None, None]
    ki = jnp.arange(N_KV_HEADS)[None, :, None, None]
    k_sel = ks[bi, ki, idx].reshape(B, N_KV_HEADS, Tq, -1, HEAD_DIM)
    v_sel = vs[bi, ki, idx].reshape(B, N_KV_HEADS, Tq, -1, HEAD_DIM)
    pos_sel = (idx[..., None] * BLOCK + jnp.arange(BLOCK, dtype=jnp.int32)).reshape(B, N_KV_HEADS, Tq, -1)
    table = jnp.transpose(rel_bias.astype(f32).reshape(N_BUCKETS, N_KV_HEADS, GROUP), (1, 0, 2))
    b_s = jnp.moveaxis(table[ki, rel_bucket(q_pos[:, None] - pos_sel)], -1, 2)
    s_s = jnp.einsum('bqkgd,bkqmd->bkgqm', q, k_sel).astype(f32) + b_s
    p_s = masked_softmax(s_s, (pos_sel <= q_pos[:, None])[:, :, None])
    o_s = jnp.einsum('bkgqm,bkqmd->bqkgd', p_s.astype(v_sel.dtype), v_sel)
    d_w = q_pos[:, None] - kw_pos[None, :]
    mask_w = (d_w >= 0) & (d_w <= WINDOW) & (kw_pos[None, :] >= 0)
    s_w = jnp.einsum('bqkgd,bskd->bkgqs', q, kw).astype(f32) + bias_2d(rel_bias, q_pos, kw_pos)
    p_w = masked_softmax(s_w, mask_w)
    o_w = jnp.einsum('bkgqs,bskd->bqkgd', p_w.astype(vw.dtype), vw)
    return g[..., 0:1] * o_c + g[..., 1:2] * o_s + g[..., 2:3] * o_w


def nsa_layer(xn, w_in, w_out, ctx, rel_bias, pos0):
    B, T, _ = xn.shape
    kc, vc, ks, vs, kw_all = ctx
    wp = kw_all.shape[1] - T
    proj = xn @ w_in
    q = proj[..., :N_HEADS * HEAD_DIM].reshape(B, T, N_KV_HEADS, GROUP, HEAD_DIM) * (HEAD_DIM ** -0.5)
    g = jax.nn.sigmoid(proj[..., N_HEADS * HEAD_DIM:]).reshape(B, T, N_KV_HEADS, GROUP, 3)
    qb = Q_BLOCK if T % Q_BLOCK == 0 else T
    nb = T // qb

    def to_blocks(a):
        return jnp.moveaxis(a.reshape((B, nb, qb) + a.shape[2:]), 1, 0)

    def attend_block(args):
        q_b, g_b, start = args
        kwin = lax.dynamic_slice_in_dim(kw_all, start, qb + wp, axis=1)
        q_pos = pos0 + start + jnp.arange(qb, dtype=jnp.int32)
        kw_pos = pos0 - wp + start + jnp.arange(qb + wp, dtype=jnp.int32)
        return nsa_attend(q_b, g_b, q_pos, kc, vc, ks, vs, kwin[:, :, 0], kwin[:, :, 1], kw_pos, rel_bias)

    starts = jnp.arange(nb, dtype=jnp.int32) * qb
    o = lax.map(attend_block, (to_blocks(q), to_blocks(g), starts))
    o = jnp.moveaxis(o, 0, 1).reshape(B, T, N_HEADS * HEAD_DIM)
    return o @ w_out


def trunk(x, pos0, conv_prefix, past_kv, win_prefix, keep, p):
    h = x
    new_conv = []
    ctx, new_rows, new_win = None, None, None
    for i in range(DEPTH):
        if i == N_A_LAYERS:
            ctx, new_rows, new_win = shared_kv(h, past_kv, win_prefix, keep, p['norm_kv'], p['w_kv'],
                                               p['cmp_pos'], p['cmp_w1'], p['cmp_w2'])
        xn = rmsnorm(h, p['norm_mix'][i])
        if i < N_A_LAYERS:
            out, st = conv_module(xn, conv_prefix[i], p['conv_w_pw1'][i], p['conv_b_pw1'][i],
                                  p['conv_w_dw'][i], p['conv_b_dw'][i], p['conv_ln_g'][i],
                                  p['conv_ln_b'][i], p['conv_w_pw2'][i], p['conv_b_pw2'][i])
            new_conv.append(st)
        else:
            j = i - N_A_LAYERS
            out = nsa_layer(xn, p['nsa_w_in'][j], p['nsa_w_out'][j], ctx, p['rel_bias'], pos0)
        h = h + out
        xn = rmsnorm(h, p['norm_ffn'][i])
        if i % 2 == 0:
            e = i // 2
            h = h + swiglu(xn, p['ffn_w_gate'][e], p['ffn_w_up'][e], p['ffn_w_down'][e])
        else:
            e = i // 2
            h = h + moe_ffn(xn, p['moe_w_router'][e], p['moe_w_gate'][e], p['moe_w_up'][e], p['moe_w_down'][e])
    return rmsnorm(h, p['norm_final']), new_rows, new_win, jnp.stack(new_conv)


def setup_inputs(seed: int = 0) -> dict:
    key = jax.random.key(seed)
    ks = jax.random.split(key, 40)
    f32 = jnp.float32

    def nrm(k, shape, scale):
        return scale * jax.random.normal(k, shape, f32)

    D = D_MODEL
    HD = N_HEADS * HEAD_DIM
    n_pages = PAST_LEN // PAGE_SIZE
    n_used = DEC_BATCH * n_pages
    n_phys = n_used + max(1, n_used // 4)
    w_buf = min(WINDOW, PAST_LEN)
    page_table = jax.random.permutation(ks[0], n_phys)[:n_used].reshape(DEC_BATCH, n_pages).astype(jnp.int32)
    return {
        'x_prompt': nrm(ks[1], (BATCH, SEQ, D), 1.0),
        'x_sample': nrm(ks[2], (DEC_BATCH, DEC_SEQ, D), 1.0),
        'cache_kv': nrm(ks[3], (n_phys, PAGE_SIZE, PAGED_SLOTS, N_KV_HEADS, HEAD_DIM), 1.0),
        'state_win_kv': nrm(ks[4], (DEC_BATCH, w_buf, 2, N_KV_HEADS, HEAD_DIM), 1.0),
        'state_conv': nrm(ks[5], (N_A_LAYERS, DEC_BATCH, CONV_WIDTH - 1, D), 0.5),
        'page_table': page_table,
        'norm_mix': 1.0 + nrm(ks[6], (DEPTH, D), 0.02),
        'norm_ffn': 1.0 + nrm(ks[7], (DEPTH, D), 0.02),
        'norm_kv': 1.0 + nrm(ks[8], (D,), 0.02),
        'norm_final': 1.0 + nrm(ks[9], (D,), 0.02),
        'conv_w_pw1': nrm(ks[10], (N_A_LAYERS, D, 2 * D), D ** -0.5),
        'conv_b_pw1': nrm(ks[11], (N_A_LAYERS, 2 * D), 0.02),
        'conv_w_dw': nrm(ks[12], (N_A_LAYERS, CONV_WIDTH, D), CONV_WIDTH ** -0.5),
        'conv_b_dw': nrm(ks[13], (N_A_LAYERS, D), 0.02),
        'conv_ln_g': 1.0 + nrm(ks[14], (N_A_LAYERS, D), 0.02),
        'conv_ln_b': nrm(ks[15], (N_A_LAYERS, D), 0.02),
        'conv_w_pw2': nrm(ks[16], (N_A_LAYERS, D, D), D ** -0.5),
        'conv_b_pw2': nrm(ks[17], (N_A_LAYERS, D), 0.02),
        'w_kv': nrm(ks[18], (D, KV_SLOTS * N_KV_HEADS * HEAD_DIM), D ** -0.5),
        'cmp_pos': nrm(ks[19], (2, BLOCK, HEAD_DIM), 0.1),
        'cmp_w1': nrm(ks[20], (2, BLOCK * HEAD_DIM, CMP_HIDDEN), (BLOCK * HEAD_DIM) ** -0.5),
        'cmp_w2': nrm(ks[21], (2, CMP_HIDDEN, HEAD_DIM), CMP_HIDDEN ** -0.5),
        'nsa_w_in': nrm(ks[22], (N_B_LAYERS, D, HD + 3 * N_HEADS), D ** -0.5),
        'nsa_w_out': nrm(ks[23], (N_B_LAYERS, HD, D), HD ** -0.5),
        'rel_bias': nrm(ks[24], (N_BUCKETS, N_HEADS), 0.5),
        'ffn_w_gate': nrm(ks[25], (N_DENSE, D, D_FF), D ** -0.5),
        'ffn_w_up': nrm(ks[26], (N_DENSE, D, D_FF), D ** -0.5),
        'ffn_w_down': nrm(ks[27], (N_DENSE, D_FF, D), D_FF ** -0.5),
        'moe_w_router': nrm(ks[28], (N_MOE, D, N_EXPERTS), D ** -0.5),
        'moe_w_gate': nrm(ks[29], (N_MOE, N_EXPERTS, D, D_FF_EXPERT), D ** -0.5),
        'moe_w_up': nrm(ks[30], (N_MOE, N_EXPERTS, D, D_FF_EXPERT), D ** -0.5),
        'moe_w_down': nrm(ks[31], (N_MOE, N_EXPERTS, D_FF_EXPERT, D), D_FF_EXPERT ** -0.5),
    }


def reference(x_prompt, x_sample, cache_kv, state_win_kv, state_conv, page_table,
              norm_mix, norm_ffn, norm_kv, norm_final,
              conv_w_pw1, conv_b_pw1, conv_w_dw, conv_b_dw, conv_ln_g, conv_ln_b, conv_w_pw2, conv_b_pw2,
              w_kv, cmp_pos, cmp_w1, cmp_w2, nsa_w_in, nsa_w_out, rel_bias,
              ffn_w_gate, ffn_w_up, ffn_w_down,
              moe_w_router, moe_w_gate, moe_w_up, moe_w_down):
    p = dict(norm_mix=norm_mix, norm_ffn=norm_ffn, norm_kv=norm_kv, norm_final=norm_final,
             conv_w_pw1=conv_w_pw1, conv_b_pw1=conv_b_pw1, conv_w_dw=conv_w_dw, conv_b_dw=conv_b_dw,
             conv_ln_g=conv_ln_g, conv_ln_b=conv_ln_b, conv_w_pw2=conv_w_pw2, conv_b_pw2=conv_b_pw2,
             w_kv=w_kv, cmp_pos=cmp_pos, cmp_w1=cmp_w1, cmp_w2=cmp_w2,
             nsa_w_in=nsa_w_in, nsa_w_out=nsa_w_out, rel_bias=rel_bias,
             ffn_w_gate=ffn_w_gate, ffn_w_up=ffn_w_up, ffn_w_down=ffn_w_down,
             moe_w_router=moe_w_router, moe_w_gate=moe_w_gate, moe_w_up=moe_w_up, moe_w_down=moe_w_down)
    bp, tp, _ = x_prompt.shape
    dt = x_prompt.dtype
    conv0 = jnp.zeros((N_A_LAYERS, bp, CONV_WIDTH - 1, D_MODEL), dt)
    past0 = jnp.zeros((bp, 0, PAGED_SLOTS, N_KV_HEADS, HEAD_DIM), dt)
    win0 = jnp.zeros((bp, WINDOW, 2, N_KV_HEADS, HEAD_DIM), dt)
    y_p, kv_p, win_p, conv_p = trunk(x_prompt, 0, conv0, past0, win0, min(WINDOW, tp), p)
    bs, ts, _ = x_sample.shape
    past_len = page_table.shape[1] * cache_kv.shape[1]
    past = cache_kv[page_table].reshape(bs, past_len, PAGED_SLOTS, N_KV_HEADS, HEAD_DIM)
    y_s, kv_s, win_s, conv_s = trunk(x_sample, past_len, state_conv, past, state_win_kv,
                                     min(WINDOW, past_len + ts), p)
    return (y_p, y_s, kv_p, win_p, conv_p, kv_s, win_s, conv_s)
```

```python
import functools
import math

import numpy as np
import jax
import jax.numpy as jnp
from jax import lax
from jax.experimental import pallas as pl
from jax.experimental.pallas import tpu as pltpu

F32 = jnp.float32
BF16 = jnp.bfloat16

D_MODEL = 1024
N_HEADS = 16
HEAD_DIM = 64
N_KV_HEADS = 4
GROUP = N_HEADS // N_KV_HEADS
HEADS_LANES = N_KV_HEADS * HEAD_DIM
BLOCK = 64
TOP_N = 16
WINDOW = 512
CONV_WIDTH = 31
CONV_HALO = 32
CMP_PITCH = BLOCK + 8
N_BUCKETS = 32
MAX_EXACT = 16
MAX_DISTANCE = 128
N_EXPERTS = 8
FORCE_SCORE = 1e4
EPS = 1e-6
NEG = -1e30
TINY = 1e-30
LANES = 128
VMEM_LIMIT_BYTES = 56 * 1024 * 1024


def _bucket_thresholds():
    d = np.arange(0, 4 * MAX_DISTANCE)
    nf = np.maximum(d, 1).astype(np.float32)
    large = MAX_EXACT + (np.log(nf / np.float32(MAX_EXACT)) / np.float32(math.log(MAX_DISTANCE / MAX_EXACT))
                         * np.float32(N_BUCKETS - MAX_EXACT)).astype(np.int32)
    b = np.where(d < MAX_EXACT, d, np.minimum(large, N_BUCKETS - 1))
    assert np.all(np.diff(b) >= 0)
    return [int(np.min(d[b >= k])) for k in range(N_BUCKETS)]


BUCKET_THR = _bucket_thresholds()
FAR_DIST = BUCKET_THR[-1]


def _cparams(sem):
    return pltpu.CompilerParams(dimension_semantics=sem, vmem_limit_bytes=VMEM_LIMIT_BYTES)


def _sigmoid(x):
    return 1.0 / (1.0 + jnp.exp(-x))


def _silu(x):
    return x * _sigmoid(x)


def _rmsnorm(x, g):
    return x * lax.rsqrt(jnp.mean(x * x, axis=-1, keepdims=True) + EPS) * g


def _dot(a, b):
    return jnp.dot(a, b, preferred_element_type=F32)


def _dot_nt(a, b):
    return lax.dot_general(a, b, (((1,), (1,)), ((), ())), preferred_element_type=F32)


def _rel_bias(dist, rb_ref, head):
    val = jnp.full(dist.shape, rb_ref[0, head], F32)
    for b in range(1, N_BUCKETS):
        val = jnp.where(dist >= BUCKET_THR[b], rb_ref[b, head], val)
    return val


def _softmax_parts(s, mask):
    s = jnp.where(mask, s, NEG)
    m = jnp.max(s, axis=-1, keepdims=True)
    p = jnp.where(mask, jnp.exp(s - m), 0.0)
    return p, jnp.maximum(jnp.sum(p, axis=-1, keepdims=True), TINY)


def _mm_kernel(*refs, n_w, has_norm, has_bias, act, has_res, use_scratch):
    it = iter(refs)
    x_ref = next(it)
    g_ref = next(it) if has_norm else None
    w_refs = [next(it) for _ in range(n_w)]
    b_refs = [next(it) for _ in range(n_w)] if has_bias else []
    r_ref = next(it) if has_res else None
    o_ref = next(it)
    if use_scratch:
        xb_ref = next(it)

        @pl.when(pl.program_id(1) == 0)
        def _():
            x = x_ref[...].astype(F32)
            if has_norm:
                x = _rmsnorm(x, g_ref[...])
            xb_ref[...] = x.astype(BF16)

        xb = xb_ref[...]
    else:
        xb = x_ref[...]
    ys = [_dot(xb, w[...].astype(BF16)) for w in w_refs]
    if has_bias:
        ys = [y + b[...] for y, b in zip(ys, b_refs)]
    if act == "glu":
        y = ys[0] * _sigmoid(ys[1])
    elif act == "swiglu":
        y = _silu(ys[0]) * ys[1]
    else:
        y = ys[0]
    if has_res:
        y = y + r_ref[...]
    o_ref[...] = y.astype(o_ref.dtype)


def _mm(x, ws, n_out, *, tm, tn, norm_g=None, biases=None, act=None, residual=None, out_dtype=F32):
    m, k = x.shape
    assert m % tm == 0 and n_out % tn == 0
    use_scratch = norm_g is not None or x.dtype != BF16
    args, specs = [x], [pl.BlockSpec((tm, k), lambda i, j: (i, 0))]
    if norm_g is not None:
        args.append(norm_g.reshape(1, k))
        specs.append(pl.BlockSpec((1, k), lambda i, j: (0, 0)))
    for w, off in ws:
        assert off % tn == 0
        args.append(w)
        specs.append(pl.BlockSpec((k, tn), functools.partial(lambda i, j, o: (0, j + o), o=off // tn)))
    if biases is not None:
        for (bias, off) in biases:
            args.append(bias.reshape(1, -1))
            specs.append(pl.BlockSpec((1, tn), functools.partial(lambda i, j, o: (0, j + o), o=off // tn)))
    if residual is not None:
        args.append(residual)
        specs.append(pl.BlockSpec((tm, tn), lambda i, j: (i, j)))
    kern = functools.partial(_mm_kernel, n_w=len(ws), has_norm=norm_g is not None, has_bias=biases is not None,
                             act=act, has_res=residual is not None, use_scratch=use_scratch)
    return pl.pallas_call(
        kern,
        grid=(m // tm, n_out // tn),
        in_specs=specs,
        out_specs=pl.BlockSpec((tm, tn), lambda i, j: (i, j)),
        out_shape=jax.ShapeDtypeStruct((m, n_out), out_dtype),
        scratch_shapes=[pltpu.VMEM((tm, k), BF16)] if use_scratch else [],
        compiler_params=_cparams(("parallel", "arbitrary")),
    )(*args)


def _conv_kernel(main_ref, tail_ref, wdw_ref, bdw_ref, lng_ref, lnb_ref, o_ref, win_ref, c_ref, *, tt, rc):
    for c in range(D_MODEL // LANES):
        sl = slice(c * LANES, (c + 1) * LANES)
        win_ref[c, 0:tt, :] = main_ref[0, :, sl]
        win_ref[c, tt:tt + CONV_HALO, :] = tail_ref[0, :, sl]

    def row_chunk(r, carry):
        r0 = pl.multiple_of(r * rc, rc)
        for c in range(D_MODEL // LANES):
            sl = slice(c * LANES, (c + 1) * LANES)
            acc = jnp.zeros((rc, LANES), F32)
            for k in range(CONV_WIDTH):
                acc = acc + win_ref[c, pl.ds(r0 + (CONV_HALO - CONV_WIDTH + 1) + k, rc), :] * wdw_ref[k:k + 1, sl]
            c_ref[pl.ds(r0, rc), sl] = acc + bdw_ref[:, sl]
        return carry

    lax.fori_loop(0, tt // rc, row_chunk, 0)
    c = c_ref[...]
    mu = jnp.mean(c, axis=-1, keepdims=True)
    xc = c - mu
    y = xc * lax.rsqrt(jnp.mean(xc * xc, axis=-1, keepdims=True) + EPS)
    y = y * lng_ref[...] + lnb_ref[...]
    o_ref[0] = _silu(y).astype(o_ref.dtype)


def _conv_ln_swish(main, tail, main_map, tail_map, n_t, tt, rc, wdw, bdw, lng, lnb):
    bsz = main.shape[0]
    vec = pl.BlockSpec((1, D_MODEL), lambda b, i: (0, 0))
    return pl.pallas_call(
        functools.partial(_conv_kernel, tt=tt, rc=rc),
        grid=(bsz, n_t),
        in_specs=[pl.BlockSpec((1, tt, D_MODEL), main_map),
                  pl.BlockSpec((1, CONV_HALO, D_MODEL), tail_map),
                  pl.BlockSpec((CONV_WIDTH, D_MODEL), lambda b, i: (0, 0)), vec, vec, vec],
        out_specs=pl.BlockSpec((1, tt, D_MODEL), lambda b, i: (b, i, 0)),
        out_shape=jax.ShapeDtypeStruct((bsz, n_t * tt, D_MODEL), BF16),
        scratch_shapes=[pltpu.VMEM((D_MODEL // LANES, tt + CONV_HALO, LANES), F32), pltpu.VMEM((tt, D_MODEL), F32)],
        compiler_params=_cparams(("parallel", "parallel")),
    )(main, tail, wdw, bdw.reshape(1, -1), lng.reshape(1, -1), lnb.reshape(1, -1))


def _compress_kernel(pt_ref, *refs, n_pages):
    x_refs = refs[:n_pages]
    pe_ref, w1_ref, w2_ref, o_ref, xs_ref = refs[n_pages:]
    nb = 2 * n_pages
    for j in range(n_pages):
        x = x_refs[j][0] + pe_ref[...]
        for c in range(2 * HEADS_LANES // LANES):
            for half in range(2):
                xs_ref[c, pl.ds((2 * j + half) * CMP_PITCH, BLOCK), :] = (
                    x[half * BLOCK:(half + 1) * BLOCK, c * LANES:(c + 1) * LANES])
    lane = lax.broadcasted_iota(jnp.int32, (1, HEADS_LANES), 1)
    head_masks = [(lane // HEAD_DIM) == h for h in range(N_KV_HEADS)]
    for s in range(2):
        def pos_body(p, acc, s=s):
            xp = jnp.concatenate([xs_ref[2 * s + c, pl.ds(p, nb, stride=CMP_PITCH), :] for c in range(2)],
                                 axis=1)
            lhs = jnp.concatenate([jnp.where(hm, xp, 0.0) for hm in head_masks], axis=0).astype(BF16)
            w = w1_ref[s, pl.ds(pl.multiple_of(p * HEAD_DIM, HEAD_DIM), HEAD_DIM), :]
            w4 = jnp.concatenate([w] * N_KV_HEADS, axis=0).astype(BF16)
            return acc + _dot(lhs, w4)

        hid = lax.fori_loop(0, BLOCK, pos_body, jnp.zeros((N_KV_HEADS * nb, w1_ref.shape[2]), F32))
        hid = _silu(hid).astype(BF16)
        out = jnp.zeros((nb, HEADS_LANES), F32)
        for h in range(N_KV_HEADS):
            out = out + _dot(hid[h * nb:(h + 1) * nb], w2_ref[s, h].astype(BF16))
        o_ref[:, s * HEADS_LANES:(s + 1) * HEADS_LANES] = out


def _compress(src, page_ids, pe_tile, w1, w2pad, n_pages):
    n = page_ids.shape[0]
    assert n % n_pages == 0
    x_specs = [pl.BlockSpec((1, 128, 2 * HEADS_LANES),
                            functools.partial(lambda i, pt, j: (pt[i * n_pages + j], 0, 0), j=j))
               for j in range(n_pages)]
    grid_spec = pltpu.PrefetchScalarGridSpec(
        num_scalar_prefetch=1,
        grid=(n // n_pages,),
        in_specs=x_specs + [
            pl.BlockSpec((128, 2 * HEADS_LANES), lambda i, pt: (0, 0)),
            pl.BlockSpec(w1.shape, lambda i, pt: (0, 0, 0)),
            pl.BlockSpec(w2pad.shape, lambda i, pt: (0, 0, 0, 0)),
        ],
        out_specs=pl.BlockSpec((2 * n_pages, 2 * HEADS_LANES), lambda i, pt: (i, 0)),
        scratch_shapes=[pltpu.VMEM((2 * HEADS_LANES // LANES, 2 * n_pages * CMP_PITCH, LANES), F32)],
    )
    return pl.pallas_call(
        functools.partial(_compress_kernel, n_pages=n_pages),
        grid_spec=grid_spec,
        out_shape=jax.ShapeDtypeStruct((2 * n, 2 * HEADS_LANES), F32),
        compiler_params=_cparams(("parallel",)),
    )(page_ids, *([src] * n_pages), pe_tile, w1, w2pad)


def _head_gate(sig, lane, g, branch, k):
    col = (g * 3 + branch) * N_KV_HEADS + k
    return jnp.sum(jnp.where(lane == col, sig, 0.0), axis=-1, keepdims=True)


def _pattn_kernel(rb_ref, q_ref, kc_ref, vc_ref, ks_ref, vs_ref, kw_ref, vw_ref, o_ref, strips_ref, *, tq, nc):
    b, i, k = pl.program_id(0), pl.program_id(1), pl.program_id(2)
    rc = lax.broadcasted_iota(jnp.int32, (tq, tq), 0) - lax.broadcasted_iota(jnp.int32, (tq, tq), 1)

    @pl.when((b == 0) & (i == 0) & (k == 0))
    def _():
        def head_body(hh, carry):
            head = (hh & 3) * GROUP + (hh >> 2)
            strips_ref[hh * 3] = _rel_bias(rc, rb_ref, head)
            strips_ref[hh * 3 + 1] = _rel_bias(rc + tq, rb_ref, head)
            strips_ref[hh * 3 + 2] = jnp.full((tq, tq), rb_ref[N_BUCKETS - 1, head], F32)
            return carry

        lax.fori_loop(0, N_HEADS, head_body, 0)

    @pl.when(k == 0)
    def _():
        o_ref[...] = jnp.zeros(o_ref.shape, o_ref.dtype)

    lane = lax.broadcasted_iota(jnp.int32, (1, HEADS_LANES), 1)
    head_mask = (lane // HEAD_DIM) == k
    glane = lax.broadcasted_iota(jnp.int32, (1, LANES), 1)
    sig = _sigmoid(q_ref[0, :, D_MODEL:D_MODEL + LANES])
    kc = kc_ref[0].astype(BF16)
    vc = vc_ref[0].astype(BF16)
    eye = (rc == 0).astype(BF16)
    qs = [jnp.where(head_mask, q_ref[0, :, g * HEADS_LANES:(g + 1) * HEADS_LANES] * (HEAD_DIM ** -0.5), 0.0)
          .astype(BF16) for g in range(GROUP)]

    qpos_t = i * tq + lax.broadcasted_iota(jnp.int32, (1, tq), 1)
    blk_t = lax.broadcasted_iota(jnp.int32, (nc, 1), 0)
    cmp_end_t = blk_t * BLOCK + (BLOCK - 1)
    mask_c = cmp_end_t <= qpos_t
    dist_c = qpos_t - cmp_end_t
    imp = jnp.zeros((nc, tq), F32)
    o_cmp = []
    for g in range(GROUP):
        s = _dot_nt(kc, qs[g]) + _rel_bias(dist_c, rb_ref, k * GROUP + g)
        s = jnp.where(mask_c, s, NEG)
        m = jnp.max(s, axis=0, keepdims=True)
        p = jnp.where(mask_c, jnp.exp(s - m), 0.0)
        p = p / jnp.maximum(jnp.sum(p, axis=0, keepdims=True), TINY)
        imp = imp + p
        p_rows = _dot_nt(eye, p.astype(BF16)).astype(BF16)
        o_cmp.append(_dot(p_rows, vc))

    cur_t = qpos_t // BLOCK
    forced = (blk_t == cur_t) | (blk_t == cur_t - 1) | (blk_t == 0)
    score = jnp.where(blk_t > cur_t, NEG, jnp.where(forced, FORCE_SCORE, imp))
    sel_t = jnp.zeros((nc, tq), F32)
    for j in range(nc):
        row = score[j:j + 1, :]
        beats = (score > row) | ((score == row) & (blk_t < j))
        rank = jnp.sum(beats.astype(F32), axis=0, keepdims=True)
        sel_t = jnp.where(blk_t == j, (rank < TOP_N).astype(F32), sel_t)
    sel = _dot_nt(eye, sel_t.astype(BF16)).astype(BF16)

    blk_col = lax.broadcasted_iota(jnp.int32, (nc, 1), 0)
    key_blk = lax.broadcasted_iota(jnp.int32, (1, tq), 1) // BLOCK
    n_back = WINDOW // tq

    for g in range(GROUP):
        hh = g * N_KV_HEADS + k
        q = qs[g]

        def chunk_scores(j, k_ref):
            row0 = pl.multiple_of(j * tq, tq)
            delta = i - j
            s = _dot_nt(q, k_ref[0, pl.ds(row0, tq), :])
            return s + strips_ref[hh * 3 + jnp.minimum(delta, 2)], delta, row0

        def online(carry, s, mask, v_ref, row0):
            m, l, acc = carry
            s = jnp.where(mask, s, NEG)
            m_new = jnp.maximum(m, jnp.max(s, axis=-1, keepdims=True))
            alpha = jnp.exp(m - m_new)
            p = jnp.where(mask, jnp.exp(s - m_new), 0.0)
            l = alpha * l + jnp.sum(p, axis=-1, keepdims=True)
            acc = alpha * acc + _dot(p.astype(BF16), v_ref[0, pl.ds(row0, tq), :])
            return m_new, l, acc

        def sel_body(j, carry):
            s, delta, row0 = chunk_scores(j, ks_ref)
            expand = (blk_col == j * (tq // BLOCK) + key_blk).astype(BF16)
            mask = (_dot(sel, expand) > 0.5) & (rc + delta * tq >= 0)
            return online(carry, s, mask, vs_ref, row0)

        def win_body(j, carry):
            s, delta, row0 = chunk_scores(j, kw_ref)
            dist = rc + delta * tq
            mask = (dist >= 0) & (dist <= WINDOW)
            return online(carry, s, mask, vw_ref, row0)

        init = (jnp.full((tq, 1), NEG, F32), jnp.zeros((tq, 1), F32), jnp.zeros((tq, HEADS_LANES), F32))
        _, l_s, acc_s = lax.fori_loop(0, i + 1, sel_body, init)
        _, l_w, acc_w = lax.fori_loop(jnp.maximum(i - n_back, 0), i + 1, win_body, init)
        comb = (_head_gate(sig, glane, g, 0, k) * o_cmp[g]
                + _head_gate(sig, glane, g, 1, k) * (acc_s / jnp.maximum(l_s, TINY))
                + _head_gate(sig, glane, g, 2, k) * (acc_w / jnp.maximum(l_w, TINY)))
        sl = slice(g * HEADS_LANES, (g + 1) * HEADS_LANES)
        o_ref[0, :, sl] = o_ref[0, :, sl] + jnp.where(head_mask, comb, 0.0)


def _prompt_attention(rel_bias, proj, cmp_kv, kv_b, *, tq):
    bsz, t, _ = proj.shape
    nc = cmp_kv.shape[1]
    assert t % tq == 0 and tq % BLOCK == 0 and WINDOW % tq == 0 and nc * BLOCK == t
    assert tq + 1 >= FAR_DIST

    def slab(idx, rows, dtype_rows=None):
        return pl.BlockSpec((1, rows, HEADS_LANES), functools.partial(lambda b, i, k, c: (b, 0, c), c=idx))

    return pl.pallas_call(
        functools.partial(_pattn_kernel, tq=tq, nc=nc),
        grid=(bsz, t // tq, N_KV_HEADS),
        in_specs=[pl.BlockSpec(memory_space=pltpu.SMEM),
                  pl.BlockSpec((1, tq, proj.shape[2]), lambda b, i, k: (b, i, 0)),
                  slab(0, nc), slab(1, nc),
                  slab(2, t), slab(3, t), slab(4, t), slab(5, t)],
        out_specs=pl.BlockSpec((1, tq, D_MODEL), lambda b, i, k: (b, i, 0)),
        out_shape=jax.ShapeDtypeStruct((bsz, t, D_MODEL), F32),
        scratch_shapes=[pltpu.VMEM((3 * N_HEADS, tq, tq), F32)],
        compiler_params=_cparams(("arbitrary", "arbitrary", "arbitrary")),
    )(rel_bias, proj, cmp_kv, cmp_kv, kv_b, kv_b, kv_b, kv_b)


def _router_kernel(h_ref, g_ref, wr_ref, xn_ref, comb_ref):
    xn = _rmsnorm(h_ref[...], g_ref[...])
    xn_ref[...] = xn.astype(BF16)
    logits = jnp.dot(xn, wr_ref[...], preferred_element_type=F32, precision=lax.Precision.HIGHEST)
    lane = lax.broadcasted_iota(jnp.int32, logits.shape, 1)
    lg = jnp.where(lane < N_EXPERTS, logits, -jnp.inf)
    m1 = jnp.max(lg, axis=-1, keepdims=True)
    i1 = jnp.min(jnp.where(lg == m1, lane, LANES), axis=-1, keepdims=True)
    lg2 = jnp.where(lane == i1, -jnp.inf, lg)
    m2 = jnp.max(lg2, axis=-1, keepdims=True)
    i2 = jnp.min(jnp.where(lg2 == m2, lane, LANES), axis=-1, keepdims=True)
    e = jnp.exp(m2 - m1)
    comb_ref[...] = jnp.where(lane == i1, 1.0 / (1.0 + e), 0.0) + jnp.where(lane == i2, e / (1.0 + e), 0.0)


def _router(h, g, w_router_pad, tm):
    m = h.shape[0]
    return pl.pallas_call(
        _router_kernel,
        grid=(m // tm,),
        in_specs=[pl.BlockSpec((tm, D_MODEL), lambda i: (i, 0)),
                  pl.BlockSpec((1, D_MODEL), lambda i: (0, 0)),
                  pl.BlockSpec((D_MODEL, LANES), lambda i: (0, 0))],
        out_specs=[pl.BlockSpec((tm, D_MODEL), lambda i: (i, 0)), pl.BlockSpec((tm, LANES), lambda i: (i, 0))],
        out_shape=[jax.ShapeDtypeStruct((m, D_MODEL), BF16), jax.ShapeDtypeStruct((m, LANES), F32)],
        compiler_params=_cparams(("parallel",)),
    )(h, g.reshape(1, -1), w_router_pad)


def _moe_kernel(xn_ref, comb_ref, h_ref, wg_ref, wu_ref, wd_ref, gf_ref, y_ref, acc_ref):
    e, f = pl.program_id(1), pl.program_id(2)
    last_e, last_f = pl.num_programs(1) - 1, pl.num_programs(2) - 1

    @pl.when((e == 0) & (f == 0))
    def _():
        y_ref[...] = h_ref[...]

    @pl.when(f == 0)
    def _():
        acc_ref[...] = jnp.zeros(acc_ref.shape, F32)

    x = xn_ref[...]
    hid = _silu(_dot(x, wg_ref[0].astype(BF16))) * _dot(x, wu_ref[0].astype(BF16))
    acc_ref[...] += _dot(hid.astype(BF16), wd_ref[0].astype(BF16))

    @pl.when(f == last_f)
    def _():
        lane = lax.broadcasted_iota(jnp.int32, comb_ref.shape, 1)
        ce = jnp.sum(jnp.where(lane == e, comb_ref[...], 0.0), axis=-1, keepdims=True)
        y_ref[...] += ce * acc_ref[...]

    @pl.when((e == last_e) & (f == last_f))
    def _():
        y_ref[...] = _rmsnorm(y_ref[...], gf_ref[...])


def _moe(xn, comb, h, wg, wu, wd, g_final, tm, tf):
    m = h.shape[0]
    n_e, _, d_ff = wg.shape
    assert m % tm == 0 and d_ff % tf == 0
    row = lambda i, e, f: (i, 0)
    return pl.pallas_call(
        _moe_kernel,
        grid=(m // tm, n_e, d_ff // tf),
        in_specs=[pl.BlockSpec((tm, D_MODEL), row), pl.BlockSpec((tm, LANES), row), pl.BlockSpec((tm, D_MODEL), row),
                  pl.BlockSpec((1, D_MODEL, tf), lambda i, e, f: (e, 0, f)),
                  pl.BlockSpec((1, D_MODEL, tf), lambda i, e, f: (e, 0, f)),
                  pl.BlockSpec((1, tf, D_MODEL), lambda i, e, f: (e, f, 0)),
                  pl.BlockSpec((1, D_MODEL), lambda i, e, f: (0, 0))],
        out_specs=pl.BlockSpec((tm, D_MODEL), row),
        out_shape=jax.ShapeDtypeStruct((m, D_MODEL), F32),
        scratch_shapes=[pltpu.VMEM((tm, D_MODEL), F32)],
        compiler_params=_cparams(("parallel", "arbitrary", "arbitrary")),
    )(xn, comb, h, wg, wu, wd, g_final.reshape(1, -1))


SROWS = 8


def _group_rows_bias(dist, rb_ref, heads):
    return jnp.concatenate([_rel_bias(dist[r * SROWS:(r + 1) * SROWS], rb_ref, hd) for r, hd in enumerate(heads)],
                           axis=0)


def _sattn_a_kernel(rb_ref, q_ref, kc_ref, vc_ref, kw_ref, vw_ref, part_ref, sel_ref, *, pos0, ts, wp, nc):
    k = pl.program_id(1)

    @pl.when(k == 0)
    def _():
        part_ref[...] = jnp.zeros(part_ref.shape, F32)

    lane = lax.broadcasted_iota(jnp.int32, (1, HEADS_LANES), 1)
    head_mask = (lane // HEAD_DIM) == k
    glane = lax.broadcasted_iota(jnp.int32, (1, LANES), 1)
    sig = _sigmoid(q_ref[0, :, D_MODEL:D_MODEL + LANES])
    heads = [k * GROUP + g for g in range(GROUP)]
    rows = GROUP * SROWS
    q = jnp.concatenate(
        [jnp.where(head_mask, q_ref[0, :, g * HEADS_LANES:(g + 1) * HEADS_LANES] * (HEAD_DIM ** -0.5), 0.0)
         for g in range(GROUP)], axis=0).astype(BF16)
    qpos = pos0 + lax.broadcasted_iota(jnp.int32, (rows, 1), 0) % SROWS

    blk = lax.broadcasted_iota(jnp.int32, (1, nc), 1)
    cmp_end = blk * BLOCK + (BLOCK - 1)
    dist = qpos - cmp_end
    s = _dot_nt(q, kc_ref[0].astype(BF16)) + _group_rows_bias(dist, rb_ref, heads)
    p, l = _softmax_parts(s, cmp_end <= qpos)
    p = p / l
    o_c = _dot(p.astype(BF16), vc_ref[0].astype(BF16))
    imp = p[0:SROWS]
    for g in range(1, GROUP):
        imp = imp + p[g * SROWS:(g + 1) * SROWS]

    cur = qpos[0:SROWS] // BLOCK
    forced = (blk == cur) | (blk == cur - 1) | (blk == 0)
    score = jnp.where(blk > cur, NEG, jnp.where(forced, FORCE_SCORE, imp))
    sel = jnp.zeros((SROWS, nc), F32)
    for _ in range(TOP_N - 1):
        m = jnp.max(score, axis=-1, keepdims=True)
        idx = jnp.min(jnp.where(score == m, blk, nc), axis=-1, keepdims=True)
        hit = blk == idx
        sel = jnp.where(hit, 1.0, sel)
        score = jnp.where(hit, -jnp.inf, score)
    sel_ref[0] = sel

    wk = kw_ref.shape[1]
    j = lax.broadcasted_iota(jnp.int32, (1, wk), 1)
    kw_pos = pos0 - wp + j
    dist = qpos - kw_pos
    mask = (dist >= 0) & (dist <= WINDOW) & (kw_pos >= 0) & (j < wp + ts)
    s = _dot_nt(q, kw_ref[0].astype(BF16)) + _group_rows_bias(dist, rb_ref, heads)
    p, l = _softmax_parts(s, mask)
    o_w = _dot((p / l).astype(BF16), vw_ref[0].astype(BF16))

    for g in range(GROUP):
        r = slice(g * SROWS, (g + 1) * SROWS)
        comb = _head_gate(sig, glane, g, 0, k) * o_c[r] + _head_gate(sig, glane, g, 2, k) * o_w[r]
        sl = slice(g * HEADS_LANES, (g + 1) * HEADS_LANES)
        part_ref[0, :, sl] = part_ref[0, :, sl] + jnp.where(head_mask, comb, 0.0)


def _sample_attention_a(rel_bias, q, cmp_kv, kwin, *, pos0, ts, wp):
    bsz = q.shape[0]
    nc, wk = cmp_kv.shape[1], kwin.shape[1]
    assert pos0 % BLOCK == 0 and ts < BLOCK and pos0 // BLOCK == nc and ts <= SROWS

    def slab(idx, rows):
        return pl.BlockSpec((1, rows, HEADS_LANES), functools.partial(lambda b, k, c: (b, 0, c), c=idx))

    return pl.pallas_call(
        functools.partial(_sattn_a_kernel, pos0=pos0, ts=ts, wp=wp, nc=nc),
        grid=(bsz, N_KV_HEADS),
        in_specs=[pl.BlockSpec(memory_space=pltpu.SMEM),
                  pl.BlockSpec((1, SROWS, q.shape[2]), lambda b, k: (b, 0, 0)),
                  slab(0, nc), slab(1, nc), slab(0, wk), slab(1, wk)],
        out_specs=[pl.BlockSpec((1, SROWS, D_MODEL), lambda b, k: (b, 0, 0)),
                   pl.BlockSpec((1, SROWS, nc), lambda b, k: (b, 0, k))],
        out_shape=[jax.ShapeDtypeStruct((bsz, SROWS, D_MODEL), F32),
                   jax.ShapeDtypeStruct((bsz, SROWS, N_KV_HEADS * nc), F32)],
        compiler_params=_cparams(("parallel", "arbitrary")),
    )(rel_bias, q, cmp_kv, cmp_kv, kwin, kwin)


def _sattn_b_kernel(pt_ref, rb_ref, q_ref, sel_ref, part_ref, knew_ref, *refs, pos0, ts, nc, n_pg):
    k_refs, v_refs = refs[:n_pg], refs[n_pg:2 * n_pg]
    o_ref, qall_ref, selrows_ref, bfar_ref, m_ref, l_ref, acc_ref = refs[2 * n_pg:]
    p_idx = pl.program_id(1)
    rows = N_HEADS * SROWS
    heads = list(range(N_HEADS))
    lane = lax.broadcasted_iota(jnp.int32, (1, HEADS_LANES), 1)
    qpos = pos0 + lax.broadcasted_iota(jnp.int32, (rows, 1), 0) % SROWS

    @pl.when(p_idx == 0)
    def _():
        for k in range(N_KV_HEADS):
            for g in range(GROUP):
                r = slice((k * GROUP + g) * SROWS, (k * GROUP + g + 1) * SROWS)
                qg = q_ref[0, :, g * HEADS_LANES:(g + 1) * HEADS_LANES] * (HEAD_DIM ** -0.5)
                qall_ref[r, :] = jnp.where((lane // HEAD_DIM) == k, qg, 0.0)
                selrows_ref[r, :] = sel_ref[0, :, k * nc:(k + 1) * nc]
                bfar_ref[r, :] = jnp.full((SROWS, 1), rb_ref[N_BUCKETS - 1, k * GROUP + g], F32)
        m_ref[...] = jnp.full(m_ref.shape, NEG, F32)
        l_ref[...] = jnp.zeros(l_ref.shape, F32)
        acc_ref[...] = jnp.zeros(acc_ref.shape, F32)

    qall = qall_ref[...].astype(BF16)

    def online(s, mask, v):
        s = jnp.where(mask, s, NEG)
        m_old = m_ref[...]
        m_new = jnp.maximum(m_old, jnp.max(s, axis=-1, keepdims=True))
        alpha = jnp.exp(m_old - m_new)
        p = jnp.where(mask, jnp.exp(s - m_new), 0.0)
        l_ref[...] = alpha * l_ref[...] + jnp.sum(p, axis=-1, keepdims=True)
        acc_ref[...] = alpha * acc_ref[...] + _dot(p.astype(BF16), v)
        m_ref[...] = m_new

    selrows = selrows_ref[...].astype(BF16)
    col = lax.broadcasted_iota(jnp.int32, (1, 128), 1)
    blk_rows = lax.broadcasted_iota(jnp.int32, (nc, 1), 0)
    for j in range(n_pg):
        page = p_idx * n_pg + j
        dist = qpos - (page * 128 + col)
        s = _dot_nt(qall, k_refs[j][0].astype(BF16))
        far = pos0 - (page * 128 + 127) >= FAR_DIST
        bias = lax.cond(far, lambda: jnp.broadcast_to(bfar_ref[...], (rows, 128)),
                        functools.partial(_group_rows_bias, dist, rb_ref, heads))
        expand = (blk_rows == 2 * page + col // BLOCK).astype(BF16)
        mask = (_dot(selrows, expand) > 0.5) & (dist >= 0)
        online(s + bias, mask, v_refs[j][0].astype(BF16))

    @pl.when(p_idx == pl.num_programs(1) - 1)
    def _():
        n_new = knew_ref.shape[1]
        kn = knew_ref[0, :, 0:HEADS_LANES].astype(BF16)
        vn = knew_ref[0, :, HEADS_LANES:2 * HEADS_LANES].astype(BF16)
        ncol = lax.broadcasted_iota(jnp.int32, (1, n_new), 1)
        dist = qpos - (pos0 + ncol)
        s = _dot_nt(qall, kn) + _group_rows_bias(dist, rb_ref, heads)
        online(s, (dist >= 0) & (ncol < ts), vn)
        o_sel = acc_ref[...] / jnp.maximum(l_ref[...], TINY)
        sig = _sigmoid(q_ref[0, :, D_MODEL:D_MODEL + LANES])
        for g in range(GROUP):
            sl = slice(g * HEADS_LANES, (g + 1) * HEADS_LANES)
            chunk = part_ref[0, :, sl]
            for k in range(N_KV_HEADS):
                r = slice((k * GROUP + g) * SROWS, (k * GROUP + g + 1) * SROWS)
                c = (g * 3 + 1) * N_KV_HEADS + k
                chunk = chunk + jnp.where((lane // HEAD_DIM) == k, sig[:, c:c + 1] * o_sel[r], 0.0)
            o_ref[0, :, sl] = chunk


def _sample_attention_b(rel_bias, page_ids, q, sel, part, knew, cache, *, pos0, ts, n_pg):
    bsz = q.shape[0]
    n_pages = page_ids.shape[0] // bsz
    nc = sel.shape[2] // N_KV_HEADS
    assert n_pages % n_pg == 0 and n_pages * 2 == nc
    rows = N_HEADS * SROWS

    def page_spec(lane_blk, j):
        return pl.BlockSpec((1, 128, HEADS_LANES),
                            functools.partial(lambda b, p, pt, j, c: (pt[b * n_pages + p * n_pg + j], 0, c),
                                              j=j, c=lane_blk))

    whole = lambda arr: pl.BlockSpec((1,) + arr.shape[1:], lambda b, p, pt: (b, 0, 0))
    grid_spec = pltpu.PrefetchScalarGridSpec(
        num_scalar_prefetch=1,
        grid=(bsz, n_pages // n_pg),
        in_specs=[pl.BlockSpec(memory_space=pltpu.SMEM), whole(q), whole(sel), whole(part), whole(knew)]
        + [page_spec(2, j) for j in range(n_pg)] + [page_spec(3, j) for j in range(n_pg)],
        out_specs=pl.BlockSpec((1, SROWS, D_MODEL), lambda b, p, pt: (b, 0, 0)),
        scratch_shapes=[pltpu.VMEM((rows, HEADS_LANES), F32), pltpu.VMEM((rows, nc), F32),
                        pltpu.VMEM((rows, 1), F32), pltpu.VMEM((rows, 1), F32), pltpu.VMEM((rows, 1), F32),
                        pltpu.VMEM((rows, HEADS_LANES), F32)],
    )
    return pl.pallas_call(
        functools.partial(_sattn_b_kernel, pos0=pos0, ts=ts, nc=nc, n_pg=n_pg),
        grid_spec=grid_spec,
        out_shape=jax.ShapeDtypeStruct((bsz, SROWS, D_MODEL), F32),
        compiler_params=_cparams(("parallel", "arbitrary")),
    )(page_ids, rel_bias, q, sel, part, knew, *([cache] * (2 * n_pg)))


def _compress_params(cmp_pos, cmp_w2):
    pe = jnp.transpose(cmp_pos, (1, 0, 2))[:, :, None, :]
    pe = jnp.broadcast_to(pe, (BLOCK, 2, N_KV_HEADS, HEAD_DIM)).reshape(BLOCK, 2 * HEADS_LANES)
    pe_tile = jnp.concatenate([pe, pe], axis=0)
    hidden = cmp_w2.shape[1]
    w2pad = jnp.zeros((2, N_KV_HEADS, hidden, HEADS_LANES), F32)
    for h in range(N_KV_HEADS):
        w2pad = w2pad.at[:, h, :, h * HEAD_DIM:(h + 1) * HEAD_DIM].set(cmp_w2)
    return pe_tile, w2pad


def _attn_weights(w_in, w_out):
    hd = N_HEADS * HEAD_DIM
    wq = w_in[:, :hd].reshape(D_MODEL, N_KV_HEADS, GROUP, HEAD_DIM).transpose(0, 2, 1, 3).reshape(D_MODEL, hd)
    wg = w_in[:, hd:].reshape(D_MODEL, N_KV_HEADS, GROUP, 3).transpose(0, 2, 3, 1).reshape(D_MODEL, 3 * N_HEADS)
    w_in_pad = jnp.concatenate([wq, wg, jnp.zeros((D_MODEL, LANES - 3 * N_HEADS), F32)], axis=1)
    w_out_perm = w_out.reshape(N_KV_HEADS, GROUP, HEAD_DIM, D_MODEL).transpose(1, 0, 2, 3).reshape(hd, D_MODEL)
    return w_in_pad, w_out_perm


def _row_tile(m, cap):
    t = cap
    while m % t:
        t //= 2
    return t


def _layer0_front(x, p):
    m = x.shape[0]
    w1, b1 = p["conv_w_pw1"][0], p["conv_b_pw1"][0]
    return _mm(x, [(w1, 0), (w1, D_MODEL)], D_MODEL, tm=_row_tile(m, 1024), tn=512, norm_g=p["norm_mix"][0],
               biases=[(b1, 0), (b1, D_MODEL)], act="glu")


def _layer0_back(x, c_act, p):
    m = x.shape[0]
    tm = _row_tile(m, 1024)
    h1 = _mm(c_act, [(p["conv_w_pw2"][0], 0)], D_MODEL, tm=tm, tn=512, biases=[(p["conv_b_pw2"][0], 0)], residual=x)
    d_ff = p["ffn_w_gate"].shape[2]
    a = _mm(h1, [(p["ffn_w_gate"][0], 0), (p["ffn_w_up"][0], 0)], d_ff, tm=tm, tn=256, norm_g=p["norm_ffn"][0],
            act="swiglu", out_dtype=BF16)
    h2 = _mm(a, [(p["ffn_w_down"][0], 0)], D_MODEL, tm=_row_tile(m, 512), tn=512, residual=h1)
    kv = _mm(h2, [(p["w_kv"], 0)], p["w_kv"].shape[1], tm=tm, tn=512, norm_g=p["norm_kv"])
    proj = _mm(h2, [(p["w_in_pad"], 0)], p["w_in_pad"].shape[1], tm=tm, tn=384, norm_g=p["norm_mix"][1])
    return h2, kv, proj


def _layer1_back(o, h2, p):
    m = h2.shape[0]
    tm = _row_tile(m, 1024)
    h3 = _mm(o, [(p["w_out_perm"], 0)], D_MODEL, tm=tm, tn=512, residual=h2)
    xn, comb = _router(h3, p["norm_ffn"][1], p["w_router_pad"], tm)
    return _moe(xn, comb, h3, p["moe_w_gate"][0], p["moe_w_up"][0], p["moe_w_down"][0], p["norm_final"], tm, 256)


def kernel(x_prompt, x_sample, cache_kv, state_win_kv, state_conv, page_table, norm_mix, norm_ffn, norm_kv, norm_final, conv_w_pw1, conv_b_pw1, conv_w_dw, conv_b_dw, conv_ln_g, conv_ln_b, conv_w_pw2, conv_b_pw2, w_kv, cmp_pos, cmp_w1, cmp_w2, nsa_w_in, nsa_w_out, rel_bias, ffn_w_gate, ffn_w_up, ffn_w_down, moe_w_router, moe_w_gate, moe_w_up, moe_w_down):
    p = dict(norm_mix=norm_mix, norm_ffn=norm_ffn, norm_kv=norm_kv, norm_final=norm_final,
             conv_w_pw1=conv_w_pw1, conv_b_pw1=conv_b_pw1, conv_w_pw2=conv_w_pw2, conv_b_pw2=conv_b_pw2,
             w_kv=w_kv, ffn_w_gate=ffn_w_gate, ffn_w_up=ffn_w_up, ffn_w_down=ffn_w_down,
             moe_w_gate=moe_w_gate, moe_w_up=moe_w_up, moe_w_down=moe_w_down)
    p["w_in_pad"], p["w_out_perm"] = _attn_weights(nsa_w_in[0], nsa_w_out[0])
    p["w_router_pad"] = jnp.pad(moe_w_router[0], ((0, 0), (0, LANES - N_EXPERTS)))
    pe_tile, w2pad = _compress_params(cmp_pos, cmp_w2)
    conv_args = (conv_w_dw[0], conv_b_dw[0], conv_ln_g[0], conv_ln_b[0])
    n_state = CONV_WIDTH - 1
    kv_lanes = 4 * HEADS_LANES

    bp, tp, _ = x_prompt.shape
    mp = bp * tp
    xp = x_prompt.reshape(mp, D_MODEL)
    glu = _layer0_front(xp, p).reshape(bp, tp, D_MODEL)
    gpad = jnp.concatenate([jnp.zeros((bp, CONV_HALO, D_MODEL), F32), glu], axis=1)
    tt = 256
    c_act = _conv_ln_swish(gpad, gpad, lambda b, i: (b, i, 0), lambda b, i: (b, (i + 1) * (tt // CONV_HALO), 0),
                           tp // tt, tt, 32, *conv_args)
    h2, kv, proj = _layer0_back(xp, c_act.reshape(mp, D_MODEL), p)
    cmp_p = _compress(kv.reshape(mp // 128, 128, kv.shape[1]), jnp.arange(mp // 128, dtype=jnp.int32),
                      pe_tile, cmp_w1, w2pad, _row_tile(mp // 128, 16)).reshape(bp, tp // BLOCK, 2 * HEADS_LANES)
    o = _prompt_attention(rel_bias, proj.reshape(bp, tp, -1), cmp_p, kv.astype(BF16).reshape(bp, tp, -1), tq=256)
    y_p = _layer1_back(o.reshape(mp, D_MODEL), h2, p).reshape(bp, tp, D_MODEL)
    kv3 = kv.reshape(bp, tp, kv.shape[1])
    kv_p = kv3[:, :, :kv_lanes].reshape(bp, tp, 4, N_KV_HEADS, HEAD_DIM)
    keep_p = min(WINDOW, tp)
    win_p = kv3[:, tp - keep_p:, kv_lanes:].reshape(bp, keep_p, 2, N_KV_HEADS, HEAD_DIM)
    conv_p = glu[:, tp - n_state:][None]

    bs, ts, _ = x_sample.shape
    ms = bs * ts
    n_pages, page = page_table.shape[1], cache_kv.shape[1]
    pos0 = n_pages * page
    xs = x_sample.reshape(ms, D_MODEL)
    glu_s = _layer0_front(xs, p).reshape(bs, ts, D_MODEL)
    full_s = jnp.concatenate([state_conv[0], glu_s], axis=1)
    conv_s = full_s[:, full_s.shape[1] - n_state:][None]
    gpad_s = jnp.concatenate([jnp.zeros((bs, CONV_HALO - n_state, D_MODEL), F32), full_s,
                              jnp.zeros((bs, SROWS - ts, D_MODEL), F32)], axis=1)
    once = lambda b, i: (b, 0, 0)
    c_act_s = _conv_ln_swish(gpad_s[:, :SROWS], gpad_s[:, SROWS:], once, once, 1, SROWS, SROWS, *conv_args)
    h2_s, kv_s, proj_s = _layer0_back(xs, c_act_s[:, :ts].reshape(ms, D_MODEL), p)
    cache3 = cache_kv.reshape(cache_kv.shape[0], page, 4 * HEADS_LANES)
    page_ids = page_table.reshape(-1)
    cmp_s = _compress(cache3, page_ids, pe_tile, cmp_w1, w2pad, _row_tile(page_ids.shape[0], 16)).reshape(bs, pos0 // BLOCK, 2 * HEADS_LANES)
    kv_s3 = kv_s.reshape(bs, ts, kv_s.shape[1])
    wp = state_win_kv.shape[1]
    win_all = jnp.concatenate([state_win_kv.reshape(bs, wp, 2 * HEADS_LANES), kv_s3[:, :, kv_lanes:]], axis=1)
    keep_s = min(WINDOW, pos0 + ts)
    win_s = win_all[:, wp + ts - keep_s:].reshape(bs, keep_s, 2, N_KV_HEADS, HEAD_DIM)
    pad_rows = lambda a, n: jnp.pad(a, ((0, 0), (0, n - a.shape[1]), (0, 0)))
    q_s = pad_rows(proj_s.reshape(bs, ts, -1), SROWS)
    part, sel = _sample_attention_a(rel_bias, q_s, cmp_s, pad_rows(win_all, -(-(wp + ts) // 8) * 8),
                                    pos0=pos0, ts=ts, wp=wp)
    knew = pad_rows(kv_s3[:, :, 2 * HEADS_LANES:kv_lanes], 2 * SROWS)
    o_s = _sample_attention_b(rel_bias, page_ids, q_s, sel, part, knew, cache3, pos0=pos0, ts=ts, n_pg=8)
    y_s = _layer1_back(o_s[:, :ts].reshape(ms, D_MODEL), h2_s, p).reshape(bs, ts, D_MODEL)
    kv_s_out = kv_s3[:, :, :kv_lanes].reshape(bs, ts, 4, N_KV_HEADS, HEAD_DIM)
    return (y_p, y_s, kv_p, win_p, conv_p, kv_s_out, win_s, conv_s)
```

```python
import functools
import math

import numpy as np
import jax
import jax.numpy as jnp
from jax import lax
from jax.experimental import pallas as pl
from jax.experimental.pallas import tpu as pltpu

F32 = jnp.float32
BF16 = jnp.bfloat16

D_MODEL = 1024
N_HEADS = 16
HEAD_DIM = 64
N_KV_HEADS = 4
GROUP = N_HEADS // N_KV_HEADS
HEADS_LANES = N_KV_HEADS * HEAD_DIM
PAGED_ROWS = 4 * HEADS_LANES
BLOCK = 64
TOP_N = 16
WINDOW = 512
CONV_WIDTH = 31
CONV_HALO = 32
CMP_PITCH = BLOCK + 8
N_BUCKETS = 32
MAX_EXACT = 16
MAX_DISTANCE = 128
N_EXPERTS = 8
FORCE_SCORE = 1e4
EPS = 1e-6
NEG = -1e30
MASK_BIG = 2.0 ** 100
TINY = 1e-30
LANES = 128
PAGE = 128
VMEM_LIMIT_BYTES = 56 * 1024 * 1024


def _bucket_thresholds():
    d = np.arange(0, 4 * MAX_DISTANCE)
    nf = np.maximum(d, 1).astype(np.float32)
    large = MAX_EXACT + (np.log(nf / np.float32(MAX_EXACT)) / np.float32(math.log(MAX_DISTANCE / MAX_EXACT))
                         * np.float32(N_BUCKETS - MAX_EXACT)).astype(np.int32)
    b = np.where(d < MAX_EXACT, d, np.minimum(large, N_BUCKETS - 1))
    assert np.all(np.diff(b) >= 0)
    return [int(np.min(d[b >= k])) for k in range(N_BUCKETS)]


BUCKET_THR = _bucket_thresholds()
FAR_DIST = BUCKET_THR[-1]


def _cparams(sem):
    return pltpu.CompilerParams(dimension_semantics=sem, vmem_limit_bytes=VMEM_LIMIT_BYTES)


def _sigmoid(x):
    return 1.0 / (1.0 + jnp.exp(-x))


def _silu(x):
    return x * _sigmoid(x)


def _rmsnorm(x, g):
    return x * lax.rsqrt(jnp.mean(x * x, axis=-1, keepdims=True) + EPS) * g


def _dot(a, b):
    return jnp.dot(a, b, preferred_element_type=F32)


def _dot_nt(a, b):
    return lax.dot_general(a, b, (((1,), (1,)), ((), ())), preferred_element_type=F32)


def _rel_bias(dist, rb_ref, head):
    val = jnp.full(dist.shape, rb_ref[0, head], F32)
    for b in range(1, N_BUCKETS):
        val = jnp.where(dist >= BUCKET_THR[b], rb_ref[b, head], val)
    return val


def _rel_bias_heads(dist, rb_ref, heads):
    ge = [dist >= BUCKET_THR[b] for b in range(1, N_BUCKETS)]
    out = []
    for hd in heads:
        val = jnp.full(dist.shape, rb_ref[0, hd], F32)
        for b in range(1, N_BUCKETS):
            val = jnp.where(ge[b - 1], rb_ref[b, hd], val)
        out.append(val)
    return out


def _softmax_parts(s, mask):
    s = jnp.where(mask, s, NEG)
    m = jnp.max(s, axis=-1, keepdims=True)
    p = jnp.where(mask, jnp.exp(s - m), 0.0)
    return p, jnp.maximum(jnp.sum(p, axis=-1, keepdims=True), TINY)


def _online_update(carry, s, vt):
    m, l, acc = carry
    m_new = jnp.maximum(m, jnp.max(s, axis=-1, keepdims=True))
    alpha = jnp.exp(m - m_new)
    p = jnp.exp(s - m_new)
    return m_new, alpha * l + jnp.sum(p, axis=-1, keepdims=True), alpha * acc + _dot_nt(p.astype(BF16), vt)


def _mm_kernel(*refs, n_w, has_norm, has_bias, act, has_res, use_scratch):
    it = iter(refs)
    x_ref = next(it)
    g_ref = next(it) if has_norm else None
    w_refs = [next(it) for _ in range(n_w)]
    b_refs = [next(it) for _ in range(n_w)] if has_bias else []
    r_ref = next(it) if has_res else None
    o_ref = next(it)
    if use_scratch:
        xb_ref = next(it)

        @pl.when(pl.program_id(1) == 0)
        def _():
            x = x_ref[...].astype(F32)
            if has_norm:
                x = _rmsnorm(x, g_ref[...])
            xb_ref[...] = x.astype(BF16)

        xb = xb_ref[...]
    else:
        xb = x_ref[...]
    ys = [_dot(xb, w[...].astype(BF16)) for w in w_refs]
    if has_bias:
        ys = [y + b[...] for y, b in zip(ys, b_refs)]
    if act == "glu":
        y = ys[0] * _sigmoid(ys[1])
    elif act == "swiglu":
        y = _silu(ys[0]) * ys[1]
    else:
        y = ys[0]
    if has_res:
        y = y + r_ref[...]
    o_ref[...] = y.astype(o_ref.dtype)


def _mm(x, ws, n_out, *, tm, tn, name, norm_g=None, biases=None, act=None, residual=None, out_dtype=F32):
    m, k = x.shape
    assert m % tm == 0 and n_out % tn == 0
    use_scratch = norm_g is not None or x.dtype != BF16
    args, specs = [x], [pl.BlockSpec((tm, k), lambda i, j: (i, 0))]
    if norm_g is not None:
        args.append(norm_g.reshape(1, k))
        specs.append(pl.BlockSpec((1, k), lambda i, j: (0, 0)))
    for w, off in ws:
        assert off % tn == 0
        args.append(w)
        specs.append(pl.BlockSpec((k, tn), functools.partial(lambda i, j, o: (0, j + o), o=off // tn)))
    if biases is not None:
        for (bias, off) in biases:
            args.append(bias.reshape(1, -1))
            specs.append(pl.BlockSpec((1, tn), functools.partial(lambda i, j, o: (0, j + o), o=off // tn)))
    if residual is not None:
        args.append(residual)
        specs.append(pl.BlockSpec((tm, tn), lambda i, j: (i, j)))
    kern = functools.partial(_mm_kernel, n_w=len(ws), has_norm=norm_g is not None, has_bias=biases is not None,
                             act=act, has_res=residual is not None, use_scratch=use_scratch)
    return pl.pallas_call(
        kern,
        grid=(m // tm, n_out // tn),
        in_specs=specs,
        out_specs=pl.BlockSpec((tm, tn), lambda i, j: (i, j)),
        out_shape=jax.ShapeDtypeStruct((m, n_out), out_dtype),
        scratch_shapes=[pltpu.VMEM((tm, k), BF16)] if use_scratch else [],
        compiler_params=_cparams(("parallel", "arbitrary")),
        name=name,
    )(*args)


def _mm_t_kernel(x_ref, g_ref, wt_ref, o_ref, xb_ref):
    @pl.when(pl.program_id(1) == 0)
    def _():
        xb_ref[...] = _rmsnorm(x_ref[...], g_ref[...]).astype(BF16)

    o_ref[0] = _dot_nt(wt_ref[...].astype(BF16), xb_ref[...])


def _mm_t(x, wt, norm_g, *, seq, tm, tn, name):
    m, k = x.shape
    n = wt.shape[0]
    assert m % seq == 0 and seq % tm == 0 and n % tn == 0
    per = seq // tm
    return pl.pallas_call(
        _mm_t_kernel,
        grid=(m // tm, n // tn),
        in_specs=[pl.BlockSpec((tm, k), lambda i, j: (i, 0)),
                  pl.BlockSpec((1, k), lambda i, j: (0, 0)),
                  pl.BlockSpec((tn, k), lambda i, j: (j, 0))],
        out_specs=pl.BlockSpec((1, tn, tm), lambda i, j: (i // per, j, i % per)),
        out_shape=jax.ShapeDtypeStruct((m // seq, n, seq), F32),
        scratch_shapes=[pltpu.VMEM((tm, k), BF16)],
        compiler_params=_cparams(("parallel", "arbitrary")),
        name=name,
    )(x, norm_g.reshape(1, k), wt)


def _conv_kernel(main_ref, tail_ref, wdw_ref, bdw_ref, lng_ref, lnb_ref, o_ref, win_ref, c_ref, *, tt, rc):
    for c in range(D_MODEL // LANES):
        sl = slice(c * LANES, (c + 1) * LANES)
        win_ref[c, 0:tt, :] = main_ref[0, :, sl]
        win_ref[c, tt:tt + CONV_HALO, :] = tail_ref[0, :, sl]

    def row_chunk(r, carry):
        r0 = pl.multiple_of(r * rc, rc)
        for c in range(D_MODEL // LANES):
            sl = slice(c * LANES, (c + 1) * LANES)
            acc = jnp.zeros((rc, LANES), F32)
            for k in range(CONV_WIDTH):
                acc = acc + win_ref[c, pl.ds(r0 + (CONV_HALO - CONV_WIDTH + 1) + k, rc), :] * wdw_ref[k:k + 1, sl]
            c_ref[pl.ds(r0, rc), sl] = acc + bdw_ref[:, sl]
        return carry

    lax.fori_loop(0, tt // rc, row_chunk, 0)
    c = c_ref[...]
    mu = jnp.mean(c, axis=-1, keepdims=True)
    xc = c - mu
    y = xc * lax.rsqrt(jnp.mean(xc * xc, axis=-1, keepdims=True) + EPS)
    y = y * lng_ref[...] + lnb_ref[...]
    o_ref[0] = _silu(y).astype(o_ref.dtype)


def _conv_ln_swish(main, tail, main_map, tail_map, n_t, tt, rc, wdw, bdw, lng, lnb, name):
    bsz = main.shape[0]
    vec = pl.BlockSpec((1, D_MODEL), lambda b, i: (0, 0))
    return pl.pallas_call(
        functools.partial(_conv_kernel, tt=tt, rc=rc),
        grid=(bsz, n_t),
        in_specs=[pl.BlockSpec((1, tt, D_MODEL), main_map),
                  pl.BlockSpec((1, CONV_HALO, D_MODEL), tail_map),
                  pl.BlockSpec((CONV_WIDTH, D_MODEL), lambda b, i: (0, 0)), vec, vec, vec],
        out_specs=pl.BlockSpec((1, tt, D_MODEL), lambda b, i: (b, i, 0)),
        out_shape=jax.ShapeDtypeStruct((bsz, n_t * tt, D_MODEL), BF16),
        scratch_shapes=[pltpu.VMEM((D_MODEL // LANES, tt + CONV_HALO, LANES), F32), pltpu.VMEM((tt, D_MODEL), F32)],
        compiler_params=_cparams(("parallel", "parallel")),
        name=name,
    )(main, tail, wdw, bdw.reshape(1, -1), lng.reshape(1, -1), lnb.reshape(1, -1))


def _compress_kernel(pt_ref, *refs, n_pages):
    x_refs = refs[:n_pages]
    pe_ref, w1t_ref, w2_ref, o_ref, xs_ref = refs[n_pages:]
    hidden = w1t_ref.shape[3]
    rows = N_KV_HEADS * n_pages
    for j in range(n_pages):
        x = x_refs[j][0] + pe_ref[...]
        for s in range(2):
            for h in range(N_KV_HEADS):
                r0 = (s * N_KV_HEADS + h) * HEAD_DIM
                xs_ref[s, pl.ds((h * n_pages + j) * CMP_PITCH, HEAD_DIM), :] = x[r0:r0 + HEAD_DIM, :]
    zeros = jnp.zeros((BLOCK, hidden), F32)
    for s in range(2):
        def dim_body(d, acc, s=s):
            lhs = xs_ref[s, pl.ds(d, rows, stride=CMP_PITCH), :].astype(BF16)
            w = w1t_ref[s, d]
            w2blk = jnp.concatenate([jnp.concatenate([w, zeros], axis=1), jnp.concatenate([zeros, w], axis=1)],
                                    axis=0).astype(BF16)
            return acc + _dot(lhs, w2blk)

        hid = lax.fori_loop(0, HEAD_DIM, dim_body, jnp.zeros((rows, 2 * hidden), F32), unroll=8)
        hid = _silu(hid).astype(BF16)
        for blk in range(2):
            out = jnp.zeros((n_pages, HEADS_LANES), F32)
            for h in range(N_KV_HEADS):
                out = out + _dot(hid[h * n_pages:(h + 1) * n_pages, blk * hidden:(blk + 1) * hidden],
                                 w2_ref[s, h].astype(BF16))
            c0 = blk * 2 * HEADS_LANES + s * HEADS_LANES
            o_ref[:, c0:c0 + HEADS_LANES] = out


def _compress(src, page_ids, page_index, pe_t, w1t, w2pad, n_pages, name):
    n = page_ids.shape[0]
    assert n % n_pages == 0 and n_pages % 8 == 0
    x_specs = [pl.BlockSpec((1, 2 * HEADS_LANES, PAGE),
                            functools.partial(lambda i, pt, j: page_index(pt[i * n_pages + j]), j=j))
               for j in range(n_pages)]
    grid_spec = pltpu.PrefetchScalarGridSpec(
        num_scalar_prefetch=1,
        grid=(n // n_pages,),
        in_specs=x_specs + [
            pl.BlockSpec((2 * HEADS_LANES, PAGE), lambda i, pt: (0, 0)),
            pl.BlockSpec(w1t.shape, lambda i, pt: (0, 0, 0, 0)),
            pl.BlockSpec(w2pad.shape, lambda i, pt: (0, 0, 0, 0)),
        ],
        out_specs=pl.BlockSpec((n_pages, 4 * HEADS_LANES), lambda i, pt: (i, 0)),
        scratch_shapes=[pltpu.VMEM((2, N_KV_HEADS * n_pages * CMP_PITCH, LANES), F32)],
    )
    out = pl.pallas_call(
        functools.partial(_compress_kernel, n_pages=n_pages),
        grid_spec=grid_spec,
        out_shape=jax.ShapeDtypeStruct((n, 4 * HEADS_LANES), F32),
        compiler_params=_cparams(("parallel",)),
        name=name,
    )(page_ids, *([src] * n_pages), pe_t, w1t, w2pad)
    return out.reshape(2 * n, 2 * HEADS_LANES)


def _head_gate(sig, lane, g, branch, k):
    col = (g * 3 + branch) * N_KV_HEADS + k
    return jnp.sum(jnp.where(lane == col, sig, 0.0), axis=-1, keepdims=True)


def _pattn_kernel(rb_ref, q_ref, kc_ref, vc_ref, ks_ref, vs_ref, kw_ref, vw_ref, o_ref, strips_ref, *, tq, nc):
    b, i, k = pl.program_id(0), pl.program_id(1), pl.program_id(2)
    rc = lax.broadcasted_iota(jnp.int32, (tq, tq), 0) - lax.broadcasted_iota(jnp.int32, (tq, tq), 1)
    n_back = WINDOW // tq

    @pl.when((b == 0) & (i == 0) & (k == 0))
    def _():
        def head_body(hh, carry):
            head = (hh & 3) * GROUP + (hh >> 2)
            far = rb_ref[N_BUCKETS - 1, head]
            strips_ref[hh * 4] = jnp.where(rc >= 0, _rel_bias(rc, rb_ref, head), -MASK_BIG)
            strips_ref[hh * 4 + 1] = _rel_bias(rc + tq, rb_ref, head)
            strips_ref[hh * 4 + 2] = jnp.full((tq, tq), far, F32)
            strips_ref[hh * 4 + 3] = jnp.where(rc + n_back * tq <= WINDOW, far, -MASK_BIG)
            return carry

        lax.fori_loop(0, N_HEADS, head_body, 0)

    @pl.when(k == 0)
    def _():
        o_ref[...] = jnp.zeros(o_ref.shape, o_ref.dtype)

    lane = lax.broadcasted_iota(jnp.int32, (1, HEADS_LANES), 1)
    head_mask = (lane // HEAD_DIM) == k
    glane = lax.broadcasted_iota(jnp.int32, (1, LANES), 1)
    sig = _sigmoid(q_ref[0, :, D_MODEL:D_MODEL + LANES])
    kc = kc_ref[0].astype(BF16)
    vc = vc_ref[0].astype(BF16)
    eye = (rc == 0).astype(BF16)
    qs = [jnp.where(head_mask, q_ref[0, :, g * HEADS_LANES:(g + 1) * HEADS_LANES] * (HEAD_DIM ** -0.5), 0.0)
          .astype(BF16) for g in range(GROUP)]

    qpos_t = i * tq + lax.broadcasted_iota(jnp.int32, (1, tq), 1)
    blk_t = lax.broadcasted_iota(jnp.int32, (nc, 1), 0)
    cmp_end_t = blk_t * BLOCK + (BLOCK - 1)
    mask_c = cmp_end_t <= qpos_t
    bias_c = _rel_bias_heads(qpos_t - cmp_end_t, rb_ref, [k * GROUP + g for g in range(GROUP)])
    imp = jnp.zeros((nc, tq), F32)
    o_cmp = []
    for g in range(GROUP):
        s = jnp.where(mask_c, _dot_nt(kc, qs[g]) + bias_c[g], NEG)
        m = jnp.max(s, axis=0, keepdims=True)
        p = jnp.where(mask_c, jnp.exp(s - m), 0.0)
        p = p / jnp.maximum(jnp.sum(p, axis=0, keepdims=True), TINY)
        imp = imp + p
        p_rows = _dot_nt(eye, p.astype(BF16)).astype(BF16)
        o_cmp.append(_dot(p_rows, vc))

    cur_t = qpos_t // BLOCK
    forced = (blk_t == cur_t) | (blk_t == cur_t - 1) | (blk_t == 0)
    score = jnp.where(blk_t > cur_t, NEG, jnp.where(forced, FORCE_SCORE, imp))
    sel_t = jnp.zeros((nc, tq), F32)
    for j in range(nc):
        row = score[j:j + 1, :]
        beats = (score > row) | ((score == row) & (blk_t < j))
        rank = jnp.sum(beats.astype(F32), axis=0, keepdims=True)
        sel_t = jnp.where(blk_t == j, (rank < TOP_N).astype(F32), sel_t)
    sel = _dot_nt(eye, sel_t.astype(BF16)).astype(BF16)

    key_blk = lax.broadcasted_iota(jnp.int32, (1, tq), 1) // BLOCK

    def key_tile(ref, j):
        return ref[0, :, pl.ds(pl.multiple_of(j * tq, tq), tq)].astype(BF16)

    def sel_body(jj, carry):
        j = i - jj
        kt, vt = key_tile(ks_ref, j), key_tile(vs_ref, j)
        expand = jnp.where(blk_t == j * (tq // BLOCK) + key_blk, MASK_BIG, 0.0).astype(BF16)
        unselected = _dot(sel, expand) - MASK_BIG
        sidx = jnp.minimum(jj, 2)
        return tuple(
            _online_update(carry[g], _dot(qs[g], kt) + strips_ref[(g * N_KV_HEADS + k) * 4 + sidx] + unselected, vt)
            for g in range(GROUP))

    def win_body(jj, carry):
        j = i - jj
        kt, vt = key_tile(kw_ref, j), key_tile(vw_ref, j)
        sidx = jnp.where(jj == n_back, 3, jnp.minimum(jj, 2))
        return tuple(
            _online_update(carry[g], _dot(qs[g], kt) + strips_ref[(g * N_KV_HEADS + k) * 4 + sidx], vt)
            for g in range(GROUP))

    init = tuple((jnp.full((tq, 1), NEG, F32), jnp.zeros((tq, 1), F32), jnp.zeros((tq, HEADS_LANES), F32))
                 for _ in range(GROUP))
    res_s = lax.fori_loop(0, i + 1, sel_body, init)
    res_w = lax.fori_loop(0, jnp.minimum(i, n_back) + 1, win_body, init)
    for g in range(GROUP):
        comb = (_head_gate(sig, glane, g, 0, k) * o_cmp[g]
                + _head_gate(sig, glane, g, 1, k) * (res_s[g][2] / jnp.maximum(res_s[g][1], TINY))
                + _head_gate(sig, glane, g, 2, k) * (res_w[g][2] / jnp.maximum(res_w[g][1], TINY)))
        sl = slice(g * HEADS_LANES, (g + 1) * HEADS_LANES)
        o_ref[0, :, sl] = o_ref[0, :, sl] + jnp.where(head_mask, comb, 0.0)


def _prompt_attention(rel_bias, proj, cmp_kv, paged_t, win_t, *, tq):
    bsz, t, _ = proj.shape
    nc = cmp_kv.shape[1]
    assert t % tq == 0 and tq % BLOCK == 0 and WINDOW == 2 * tq and nc * BLOCK == t
    assert tq + 1 >= FAR_DIST

    def rows(idx, n):
        return pl.BlockSpec((1, n, HEADS_LANES), functools.partial(lambda b, i, k, c: (b, 0, c), c=idx))

    def cols(idx):
        return pl.BlockSpec((1, HEADS_LANES, t), functools.partial(lambda b, i, k, c: (b, c, 0), c=idx))

    return pl.pallas_call(
        functools.partial(_pattn_kernel, tq=tq, nc=nc),
        grid=(bsz, t // tq, N_KV_HEADS),
        in_specs=[pl.BlockSpec(memory_space=pltpu.SMEM),
                  pl.BlockSpec((1, tq, proj.shape[2]), lambda b, i, k: (b, i, 0)),
                  rows(0, nc), rows(1, nc), cols(2), cols(3), cols(0), cols(1)],
        out_specs=pl.BlockSpec((1, tq, D_MODEL), lambda b, i, k: (b, i, 0)),
        out_shape=jax.ShapeDtypeStruct((bsz, t, D_MODEL), F32),
        scratch_shapes=[pltpu.VMEM((4 * N_HEADS, tq, tq), F32)],
        compiler_params=_cparams(("arbitrary", "arbitrary", "arbitrary")),
        name="prompt_attention",
    )(rel_bias, proj, cmp_kv, cmp_kv, paged_t, paged_t, win_t, win_t)


SROWS = 8
ALL_ROWS = N_HEADS * SROWS


def _all_head_queries(q_ref):
    lane = lax.broadcasted_iota(jnp.int32, (1, HEADS_LANES), 1)
    parts = []
    for k in range(N_KV_HEADS):
        for g in range(GROUP):
            qg = q_ref[0, :, g * HEADS_LANES:(g + 1) * HEADS_LANES] * (HEAD_DIM ** -0.5)
            parts.append(jnp.where((lane // HEAD_DIM) == k, qg, 0.0))
    return jnp.concatenate(parts, axis=0).astype(BF16)


def _all_head_bias(dist, rb_ref):
    return jnp.concatenate(_rel_bias_heads(dist, rb_ref, list(range(N_HEADS))), axis=0)


def _sattn_a_kernel(rb_ref, q_ref, cmp_ref, win_ref, part_ref, sel_ref, *, pos0, ts, wp, nc):
    lane = lax.broadcasted_iota(jnp.int32, (1, HEADS_LANES), 1)
    sig = _sigmoid(q_ref[0, :, D_MODEL:D_MODEL + LANES])
    q = _all_head_queries(q_ref)
    qpos = pos0 + lax.broadcasted_iota(jnp.int32, (SROWS, 1), 0)
    tile_rows = lambda a: jnp.concatenate([a] * N_HEADS, axis=0)

    blk = lax.broadcasted_iota(jnp.int32, (1, nc), 1)
    cmp_end = blk * BLOCK + (BLOCK - 1)
    s = _dot_nt(q, cmp_ref[0, :, 0:HEADS_LANES].astype(BF16)) + _all_head_bias(qpos - cmp_end, rb_ref)
    p, l = _softmax_parts(s, tile_rows(cmp_end <= qpos))
    p = p / l
    o_c = _dot(p.astype(BF16), cmp_ref[0, :, HEADS_LANES:2 * HEADS_LANES].astype(BF16))

    imp = jnp.concatenate(
        [sum(p[(k * GROUP + g) * SROWS:(k * GROUP + g + 1) * SROWS] for g in range(GROUP))
         for k in range(N_KV_HEADS)], axis=0)
    cur = jnp.concatenate([qpos // BLOCK] * N_KV_HEADS, axis=0)
    forced = (blk == cur) | (blk == cur - 1) | (blk == 0)
    score = jnp.where(blk > cur, NEG, jnp.where(forced, FORCE_SCORE, imp))
    sel = jnp.zeros(score.shape, F32)
    for _ in range(TOP_N - 1):
        m = jnp.max(score, axis=-1, keepdims=True)
        idx = jnp.min(jnp.where(score == m, blk, nc), axis=-1, keepdims=True)
        hit = blk == idx
        sel = jnp.where(hit, 1.0, sel)
        score = jnp.where(hit, -jnp.inf, score)
    for k in range(N_KV_HEADS):
        sel_ref[0, :, k * nc:(k + 1) * nc] = sel[k * SROWS:(k + 1) * SROWS]

    wk = win_ref.shape[2]
    j = lax.broadcasted_iota(jnp.int32, (1, wk), 1)
    kw_pos = pos0 - wp + j
    dist = qpos - kw_pos
    mask = (dist >= 0) & (dist <= WINDOW) & (kw_pos >= 0) & (j < wp + ts)
    s = _dot(q, win_ref[0, 0:HEADS_LANES, :].astype(BF16)) + _all_head_bias(dist, rb_ref)
    p, l = _softmax_parts(s, tile_rows(mask))
    o_w = _dot_nt((p / l).astype(BF16), win_ref[0, HEADS_LANES:2 * HEADS_LANES, :].astype(BF16))

    for g in range(GROUP):
        chunk = jnp.zeros((SROWS, HEADS_LANES), F32)
        for k in range(N_KV_HEADS):
            r = slice((k * GROUP + g) * SROWS, (k * GROUP + g + 1) * SROWS)
            c_cmp, c_win = (g * 3 + 0) * N_KV_HEADS + k, (g * 3 + 2) * N_KV_HEADS + k
            comb = sig[:, c_cmp:c_cmp + 1] * o_c[r] + sig[:, c_win:c_win + 1] * o_w[r]
            chunk = chunk + jnp.where((lane // HEAD_DIM) == k, comb, 0.0)
        part_ref[0, :, g * HEADS_LANES:(g + 1) * HEADS_LANES] = chunk


def _sample_attention_a(rel_bias, q, cmp_kv, win_t, *, pos0, ts, wp):
    bsz = q.shape[0]
    nc = cmp_kv.shape[1]
    assert pos0 % BLOCK == 0 and ts < BLOCK and pos0 // BLOCK == nc and ts <= SROWS and TOP_N >= 3
    whole = lambda arr: pl.BlockSpec((1,) + arr.shape[1:], lambda b: (b, 0, 0))
    return pl.pallas_call(
        functools.partial(_sattn_a_kernel, pos0=pos0, ts=ts, wp=wp, nc=nc),
        grid=(bsz,),
        in_specs=[pl.BlockSpec(memory_space=pltpu.SMEM), whole(q), whole(cmp_kv), whole(win_t)],
        out_specs=[pl.BlockSpec((1, SROWS, D_MODEL), lambda b: (b, 0, 0)),
                   pl.BlockSpec((1, SROWS, N_KV_HEADS * nc), lambda b: (b, 0, 0))],
        out_shape=[jax.ShapeDtypeStruct((bsz, SROWS, D_MODEL), F32),
                   jax.ShapeDtypeStruct((bsz, SROWS, N_KV_HEADS * nc), F32)],
        compiler_params=_cparams(("parallel",)),
        name="sample_attention_a",
    )(rel_bias, q, cmp_kv, win_t)


def _sattn_b_kernel(pt_ref, rb_ref, q_ref, sel_ref, part_ref, knew_ref, *refs, pos0, ts, nc, n_pg):
    k_refs, v_refs = refs[:n_pg], refs[n_pg:2 * n_pg]
    o_ref, qall_ref, selrows_ref, bfar_ref, m_ref, l_ref, acc_ref = refs[2 * n_pg:]
    p_idx = pl.program_id(1)
    keys = n_pg * PAGE
    qpos = pos0 + lax.broadcasted_iota(jnp.int32, (SROWS, 1), 0)

    @pl.when(p_idx == 0)
    def _():
        qall_ref[...] = _all_head_queries(q_ref)
        for k in range(N_KV_HEADS):
            for g in range(GROUP):
                r = slice((k * GROUP + g) * SROWS, (k * GROUP + g + 1) * SROWS)
                selrows_ref[r, :] = sel_ref[0, :, k * nc:(k + 1) * nc]
                bfar_ref[r, :] = jnp.full((SROWS, 1), rb_ref[N_BUCKETS - 1, k * GROUP + g], F32)
        m_ref[...] = jnp.full(m_ref.shape, NEG, F32)
        l_ref[...] = jnp.zeros(l_ref.shape, F32)
        acc_ref[...] = jnp.zeros(acc_ref.shape, F32)

    qall = qall_ref[...]

    def update(s, vt):
        m, l, acc = _online_update((m_ref[...], l_ref[...], acc_ref[...]), s, vt)
        m_ref[...], l_ref[...], acc_ref[...] = m, l, acc

    page0 = p_idx * n_pg
    kt = jnp.concatenate([r[0] for r in k_refs], axis=1).astype(BF16)
    vt = jnp.concatenate([r[0] for r in v_refs], axis=1).astype(BF16)
    col = lax.broadcasted_iota(jnp.int32, (1, keys), 1)
    far = pos0 - (page0 * PAGE + keys - 1) >= FAR_DIST
    bias = lax.cond(far, lambda: jnp.broadcast_to(bfar_ref[...], (ALL_ROWS, keys)),
                    lambda: _all_head_bias(qpos - (page0 * PAGE + col), rb_ref))
    blk_rows = lax.broadcasted_iota(jnp.int32, (nc, 1), 0)
    expand = jnp.where(blk_rows == 2 * page0 + col // BLOCK, MASK_BIG, 0.0).astype(BF16)
    unselected = _dot(selrows_ref[...].astype(BF16), expand) - MASK_BIG
    update(_dot(qall, kt) + bias + unselected, vt)

    @pl.when(p_idx == pl.num_programs(1) - 1)
    def _():
        ncol = lax.broadcasted_iota(jnp.int32, (1, PAGE), 1)
        dist = qpos - (pos0 + ncol)
        add = _all_head_bias(dist, rb_ref) + jnp.concatenate(
            [jnp.where((dist >= 0) & (ncol < ts), 0.0, -MASK_BIG)] * N_HEADS, axis=0)
        update(_dot(qall, knew_ref[0, 0:HEADS_LANES, :].astype(BF16)) + add,
               knew_ref[0, HEADS_LANES:2 * HEADS_LANES, :].astype(BF16))
        o_sel = acc_ref[...] / jnp.maximum(l_ref[...], TINY)
        sig = _sigmoid(q_ref[0, :, D_MODEL:D_MODEL + LANES])
        lane = lax.broadcasted_iota(jnp.int32, (1, HEADS_LANES), 1)
        for g in range(GROUP):
            sl = slice(g * HEADS_LANES, (g + 1) * HEADS_LANES)
            chunk = part_ref[0, :, sl]
            for k in range(N_KV_HEADS):
                r = slice((k * GROUP + g) * SROWS, (k * GROUP + g + 1) * SROWS)
                c = (g * 3 + 1) * N_KV_HEADS + k
                chunk = chunk + jnp.where((lane // HEAD_DIM) == k, sig[:, c:c + 1] * o_sel[r], 0.0)
            o_ref[0, :, sl] = chunk


def _sample_attention_b(rel_bias, page_ids, q, sel, part, knew_t, cache_t, *, pos0, ts, n_pg):
    bsz = q.shape[0]
    n_pages = page_ids.shape[0] // bsz
    nc = sel.shape[2] // N_KV_HEADS
    assert n_pages % n_pg == 0 and n_pages * 2 == nc and pos0 == n_pages * PAGE and TOP_N >= 3

    def page_spec(slot, j):
        return pl.BlockSpec((1, HEADS_LANES, PAGE),
                            functools.partial(lambda b, p, pt, j, c: (pt[b * n_pages + p * n_pg + j], c, 0),
                                              j=j, c=slot))

    whole = lambda arr: pl.BlockSpec((1,) + arr.shape[1:], lambda b, p, pt: (b, 0, 0))
    grid_spec = pltpu.PrefetchScalarGridSpec(
        num_scalar_prefetch=1,
        grid=(bsz, n_pages // n_pg),
        in_specs=[pl.BlockSpec(memory_space=pltpu.SMEM), whole(q), whole(sel), whole(part), whole(knew_t)]
        + [page_spec(2, j) for j in range(n_pg)] + [page_spec(3, j) for j in range(n_pg)],
        out_specs=pl.BlockSpec((1, SROWS, D_MODEL), lambda b, p, pt: (b, 0, 0)),
        scratch_shapes=[pltpu.VMEM((ALL_ROWS, HEADS_LANES), BF16), pltpu.VMEM((ALL_ROWS, nc), F32),
                        pltpu.VMEM((ALL_ROWS, 1), F32), pltpu.VMEM((ALL_ROWS, 1), F32),
                        pltpu.VMEM((ALL_ROWS, 1), F32), pltpu.VMEM((ALL_ROWS, HEADS_LANES), F32)],
    )
    return pl.pallas_call(
        functools.partial(_sattn_b_kernel, pos0=pos0, ts=ts, nc=nc, n_pg=n_pg),
        grid_spec=grid_spec,
        out_shape=jax.ShapeDtypeStruct((bsz, SROWS, D_MODEL), F32),
        compiler_params=_cparams(("parallel", "arbitrary")),
        name="sample_attention_b",
    )(page_ids, rel_bias, q, sel, part, knew_t, *([cache_t] * (2 * n_pg)))


def _router_kernel(h_ref, g_ref, wr_ref, xn_ref, comb_ref):
    xn = _rmsnorm(h_ref[...], g_ref[...]).astype(BF16)
    xn_ref[...] = xn
    logits = _dot(xn, wr_ref[...].astype(BF16))
    lane = lax.broadcasted_iota(jnp.int32, logits.shape, 1)
    lg = jnp.where(lane < N_EXPERTS, logits, -jnp.inf)
    m1 = jnp.max(lg, axis=-1, keepdims=True)
    i1 = jnp.min(jnp.where(lg == m1, lane, LANES), axis=-1, keepdims=True)
    lg2 = jnp.where(lane == i1, -jnp.inf, lg)
    m2 = jnp.max(lg2, axis=-1, keepdims=True)
    i2 = jnp.min(jnp.where(lg2 == m2, lane, LANES), axis=-1, keepdims=True)
    e = jnp.exp(m2 - m1)
    comb_ref[...] = jnp.where(lane == i1, 1.0 / (1.0 + e), 0.0) + jnp.where(lane == i2, e / (1.0 + e), 0.0)


def _router(h, g, w_router_pad, tm, name):
    m = h.shape[0]
    return pl.pallas_call(
        _router_kernel,
        grid=(m // tm,),
        in_specs=[pl.BlockSpec((tm, D_MODEL), lambda i: (i, 0)),
                  pl.BlockSpec((1, D_MODEL), lambda i: (0, 0)),
                  pl.BlockSpec((D_MODEL, LANES), lambda i: (0, 0))],
        out_specs=[pl.BlockSpec((tm, D_MODEL), lambda i: (i, 0)), pl.BlockSpec((tm, LANES), lambda i: (i, 0))],
        out_shape=[jax.ShapeDtypeStruct((m, D_MODEL), BF16), jax.ShapeDtypeStruct((m, LANES), F32)],
        compiler_params=_cparams(("parallel",)),
        name=name,
    )(h, g.reshape(1, -1), w_router_pad)


def _moe_kernel(xn_ref, comb_ref, h_ref, wg_ref, wu_ref, wd_ref, gf_ref, y_ref, acc_ref):
    e, f = pl.program_id(1), pl.program_id(2)
    last_e, last_f = pl.num_programs(1) - 1, pl.num_programs(2) - 1

    @pl.when((e == 0) & (f == 0))
    def _():
        y_ref[...] = h_ref[...]

    @pl.when(f == 0)
    def _():
        acc_ref[...] = jnp.zeros(acc_ref.shape, F32)

    x = xn_ref[...]
    hid = _silu(_dot(x, wg_ref[0].astype(BF16))) * _dot(x, wu_ref[0].astype(BF16))
    acc_ref[...] += _dot(hid.astype(BF16), wd_ref[0].astype(BF16))

    @pl.when(f == last_f)
    def _():
        lane = lax.broadcasted_iota(jnp.int32, comb_ref.shape, 1)
        ce = jnp.sum(jnp.where(lane == e, comb_ref[...], 0.0), axis=-1, keepdims=True)
        y_ref[...] += ce * acc_ref[...]

    @pl.when((e == last_e) & (f == last_f))
    def _():
        y_ref[...] = _rmsnorm(y_ref[...], gf_ref[...])


def _moe(xn, comb, h, wg, wu, wd, g_final, tm, tf, name):
    m = h.shape[0]
    n_e, _, d_ff = wg.shape
    assert m % tm == 0 and d_ff % tf == 0
    row = lambda i, e, f: (i, 0)
    return pl.pallas_call(
        _moe_kernel,
        grid=(m // tm, n_e, d_ff // tf),
        in_specs=[pl.BlockSpec((tm, D_MODEL), row), pl.BlockSpec((tm, LANES), row), pl.BlockSpec((tm, D_MODEL), row),
                  pl.BlockSpec((1, D_MODEL, tf), lambda i, e, f: (e, 0, f)),
                  pl.BlockSpec((1, D_MODEL, tf), lambda i, e, f: (e, 0, f)),
                  pl.BlockSpec((1, tf, D_MODEL), lambda i, e, f: (e, f, 0)),
                  pl.BlockSpec((1, D_MODEL), lambda i, e, f: (0, 0))],
        out_specs=pl.BlockSpec((tm, D_MODEL), row),
        out_shape=jax.ShapeDtypeStruct((m, D_MODEL), F32),
        scratch_shapes=[pltpu.VMEM((tm, D_MODEL), F32)],
        compiler_params=_cparams(("parallel", "arbitrary", "arbitrary")),
        name=name,
    )(xn, comb, h, wg, wu, wd, g_final.reshape(1, -1))


def _compress_params(cmp_pos, cmp_w1, cmp_w2):
    pe = jnp.transpose(cmp_pos, (0, 2, 1))[:, None, :, :]
    pe = jnp.broadcast_to(pe, (2, N_KV_HEADS, HEAD_DIM, BLOCK))
    pe_t = jnp.concatenate([pe, pe], axis=-1).reshape(2 * HEADS_LANES, PAGE)
    hidden = cmp_w1.shape[2]
    w1t = cmp_w1.reshape(2, BLOCK, HEAD_DIM, hidden).transpose(0, 2, 1, 3)
    w2pad = jnp.zeros((2, N_KV_HEADS, hidden, HEADS_LANES), F32)
    for h in range(N_KV_HEADS):
        w2pad = w2pad.at[:, h, :, h * HEAD_DIM:(h + 1) * HEAD_DIM].set(cmp_w2)
    return pe_t, w1t, w2pad


def _attn_weights(w_in, w_out):
    hd = N_HEADS * HEAD_DIM
    wq = w_in[:, :hd].reshape(D_MODEL, N_KV_HEADS, GROUP, HEAD_DIM).transpose(0, 2, 1, 3).reshape(D_MODEL, hd)
    wg = w_in[:, hd:].reshape(D_MODEL, N_KV_HEADS, GROUP, 3).transpose(0, 2, 3, 1).reshape(D_MODEL, 3 * N_HEADS)
    w_in_pad = jnp.concatenate([wq, wg, jnp.zeros((D_MODEL, LANES - 3 * N_HEADS), F32)], axis=1)
    w_out_perm = w_out.reshape(N_KV_HEADS, GROUP, HEAD_DIM, D_MODEL).transpose(1, 0, 2, 3).reshape(hd, D_MODEL)
    return w_in_pad, w_out_perm


def _row_tile(m, cap):
    t = cap
    while m % t:
        t //= 2
    return t


def _from_rows_t(a_t, n_slots):
    bsz, _, t = a_t.shape
    return a_t.reshape(bsz, n_slots, N_KV_HEADS, HEAD_DIM, t).transpose(0, 4, 1, 2, 3)


def _to_rows_t(a):
    bsz, t, n_slots = a.shape[:3]
    return a.transpose(0, 2, 3, 4, 1).reshape(bsz, n_slots * HEADS_LANES, t)


def _layer0_front(x, p, tag):
    m = x.shape[0]
    w1, b1 = p["conv_w_pw1"][0], p["conv_b_pw1"][0]
    return _mm(x, [(w1, 0), (w1, D_MODEL)], D_MODEL, tm=_row_tile(m, 1024), tn=512, norm_g=p["norm_mix"][0],
               biases=[(b1, 0), (b1, D_MODEL)], act="glu", name="pw1_glu_" + tag)


def _layer0_back(x, c_act, p, seq, tag):
    m = x.shape[0]
    tm = _row_tile(m, 1024)
    h1 = _mm(c_act, [(p["conv_w_pw2"][0], 0)], D_MODEL, tm=tm, tn=512, biases=[(p["conv_b_pw2"][0], 0)], residual=x,
             name="pw2_" + tag)
    d_ff = p["ffn_w_gate"].shape[2]
    a = _mm(h1, [(p["ffn_w_gate"][0], 0), (p["ffn_w_up"][0], 0)], d_ff, tm=tm, tn=256, norm_g=p["norm_ffn"][0],
            act="swiglu", out_dtype=BF16, name="ffn_up_" + tag)
    h2 = _mm(a, [(p["ffn_w_down"][0], 0)], D_MODEL, tm=_row_tile(m, 512), tn=512, residual=h1, name="ffn_down_" + tag)
    tmt = _row_tile(seq, 1024)
    paged_t = _mm_t(h2, p["w_kv_t"][:PAGED_ROWS], p["norm_kv"], seq=seq, tm=tmt, tn=512, name="kv_paged_" + tag)
    win_t = _mm_t(h2, p["w_kv_t"][PAGED_ROWS:], p["norm_kv"], seq=seq, tm=tmt, tn=512, name="kv_win_" + tag)
    proj = _mm(h2, [(p["w_in_pad"], 0)], p["w_in_pad"].shape[1], tm=tm, tn=384, norm_g=p["norm_mix"][1],
               name="q_proj_" + tag)
    return h2, paged_t, win_t, proj


def _layer1_back(o, h2, p, tag):
    m = h2.shape[0]
    tm = _row_tile(m, 1024)
    h3 = _mm(o, [(p["w_out_perm"], 0)], D_MODEL, tm=tm, tn=512, residual=h2, name="attn_out_" + tag)
    xn, comb = _router(h3, p["norm_ffn"][1], p["w_router_pad"], tm, "router_" + tag)
    return _moe(xn, comb, h3, p["moe_w_gate"][0], p["moe_w_up"][0], p["moe_w_down"][0], p["norm_final"], tm, 256,
                "moe_" + tag)


def kernel(x_prompt, x_sample, cache_kv, state_win_kv, state_conv, page_table, norm_mix, norm_ffn, norm_kv, norm_final, conv_w_pw1, conv_b_pw1, conv_w_dw, conv_b_dw, conv_ln_g, conv_ln_b, conv_w_pw2, conv_b_pw2, w_kv, cmp_pos, cmp_w1, cmp_w2, nsa_w_in, nsa_w_out, rel_bias, ffn_w_gate, ffn_w_up, ffn_w_down, moe_w_router, moe_w_gate, moe_w_up, moe_w_down):
    p = dict(norm_mix=norm_mix, norm_ffn=norm_ffn, norm_kv=norm_kv, norm_final=norm_final,
             conv_w_pw1=conv_w_pw1, conv_b_pw1=conv_b_pw1, conv_w_pw2=conv_w_pw2, conv_b_pw2=conv_b_pw2,
             ffn_w_gate=ffn_w_gate, ffn_w_up=ffn_w_up, ffn_w_down=ffn_w_down,
             moe_w_gate=moe_w_gate, moe_w_up=moe_w_up, moe_w_down=moe_w_down)
    p["w_kv_t"] = w_kv.T
    p["w_in_pad"], p["w_out_perm"] = _attn_weights(nsa_w_in[0], nsa_w_out[0])
    p["w_router_pad"] = jnp.pad(moe_w_router[0], ((0, 0), (0, LANES - N_EXPERTS)))
    pe_t, w1t, w2pad = _compress_params(cmp_pos, cmp_w1, cmp_w2)
    conv_args = (conv_w_dw[0], conv_b_dw[0], conv_ln_g[0], conv_ln_b[0])
    n_state = CONV_WIDTH - 1

    bp, tp, _ = x_prompt.shape
    mp = bp * tp
    xp = x_prompt.reshape(mp, D_MODEL)
    glu = _layer0_front(xp, p, "p").reshape(bp, tp, D_MODEL)
    gpad = jnp.concatenate([jnp.zeros((bp, CONV_HALO, D_MODEL), F32), glu], axis=1)
    tt = 256
    c_act = _conv_ln_swish(gpad, gpad, lambda b, i: (b, i, 0), lambda b, i: (b, (i + 1) * (tt // CONV_HALO), 0),
                           tp // tt, tt, 32, *conv_args, "conv_p")
    h2, paged_t, win_t, proj = _layer0_back(xp, c_act.reshape(mp, D_MODEL), p, tp, "p")
    ppb = tp // PAGE
    cmp_p = _compress(paged_t, jnp.arange(bp * ppb, dtype=jnp.int32), lambda pid: (pid // ppb, 0, pid % ppb),
                      pe_t, w1t, w2pad, _row_tile(bp * ppb, 32), "compress_p").reshape(bp, tp // BLOCK, -1)
    o = _prompt_attention(rel_bias, proj.reshape(bp, tp, -1), cmp_p, paged_t, win_t, tq=256)
    y_p = _layer1_back(o.reshape(mp, D_MODEL), h2, p, "p").reshape(bp, tp, D_MODEL)
    kv_p = _from_rows_t(paged_t, 4)
    keep_p = min(WINDOW, tp)
    win_p = _from_rows_t(win_t[:, :, tp - keep_p:], 2)
    conv_p = glu[:, tp - n_state:][None]

    bs, ts, _ = x_sample.shape
    ms = bs * ts
    n_pages = page_table.shape[1]
    assert cache_kv.shape[1] == PAGE
    pos0 = n_pages * PAGE
    xs = x_sample.reshape(ms, D_MODEL)
    glu_s = _layer0_front(xs, p, "s").reshape(bs, ts, D_MODEL)
    full_s = jnp.concatenate([state_conv[0], glu_s], axis=1)
    conv_s = full_s[:, full_s.shape[1] - n_state:][None]
    gpad_s = jnp.concatenate([jnp.zeros((bs, CONV_HALO - n_state, D_MODEL), F32), full_s,
                              jnp.zeros((bs, SROWS - ts, D_MODEL), F32)], axis=1)
    once = lambda b, i: (b, 0, 0)
    c_act_s = _conv_ln_swish(gpad_s[:, :SROWS], gpad_s[:, SROWS:], once, once, 1, SROWS, SROWS, *conv_args, "conv_s")
    h2_s, paged_ts, win_ts, proj_s = _layer0_back(xs, c_act_s[:, :ts].reshape(ms, D_MODEL), p, ms, "s")
    new_paged_t = paged_ts[0].reshape(PAGED_ROWS, bs, ts).transpose(1, 0, 2)
    new_win_t = win_ts[0].reshape(2 * HEADS_LANES, bs, ts).transpose(1, 0, 2)
    cache_t = _to_rows_t(cache_kv)
    page_ids = page_table.reshape(-1)
    cmp_s = _compress(cache_t, page_ids, lambda pid: (pid, 0, 0), pe_t, w1t, w2pad,
                      _row_tile(page_ids.shape[0], 32), "compress_s").reshape(bs, pos0 // BLOCK, -1)
    wp = state_win_kv.shape[1]
    win_all_t = jnp.concatenate([_to_rows_t(state_win_kv), new_win_t], axis=2)
    keep_s = min(WINDOW, pos0 + ts)
    win_s = _from_rows_t(win_all_t[:, :, wp + ts - keep_s:], 2)
    pad_last = lambda a, n: jnp.pad(a, ((0, 0), (0, 0), (0, n - a.shape[2])))
    q_s = jnp.pad(proj_s.reshape(bs, ts, -1), ((0, 0), (0, SROWS - ts), (0, 0)))
    part, sel = _sample_attention_a(rel_bias, q_s, cmp_s, pad_last(win_all_t, -(-(wp + ts) // LANES) * LANES),
                                    pos0=pos0, ts=ts, wp=wp)
    knew_t = pad_last(new_paged_t[:, 2 * HEADS_LANES:], PAGE)
    o_s = _sample_attention_b(rel_bias, page_ids, q_s, sel, part, knew_t, cache_t, pos0=pos0, ts=ts, n_pg=8)
    y_s = _layer1_back(o_s[:, :ts].reshape(ms, D_MODEL), h2_s, p, "s").reshape(bs, ts, D_MODEL)
    kv_s_out = _from_rows_t(new_paged_t, 4)
    return (y_p, y_s, kv_p, win_p, conv_p, kv_s_out, win_s, conv_s)
```

```python
import functools
import math

import numpy as np
import jax
import jax.numpy as jnp
from jax import lax
from jax.experimental import pallas as pl
from jax.experimental.pallas import tpu as pltpu

F32 = jnp.float32
BF16 = jnp.bfloat16

D_MODEL = 1024
N_HEADS = 16
HEAD_DIM = 64
N_KV_HEADS = 4
GROUP = N_HEADS // N_KV_HEADS
HEADS_LANES = N_KV_HEADS * HEAD_DIM
PAGED_ROWS = 4 * HEADS_LANES
BLOCK = 64
TOP_N = 16
WINDOW = 512
CONV_WIDTH = 31
CONV_HALO = 32
CMP_PITCH = BLOCK + 8
N_BUCKETS = 32
MAX_EXACT = 16
MAX_DISTANCE = 128
N_EXPERTS = 8
GROUPED_MOE_MIN_TOKENS = 1024
FORCE_SCORE = 1e4
EPS = 1e-6
NEG = -1e30
MASK_BIG = 2.0 ** 100
TINY = 1e-30
LANES = 128
PAGE = 128
VMEM_LIMIT_BYTES = 56 * 1024 * 1024


def _bucket_thresholds():
    d = np.arange(0, 4 * MAX_DISTANCE)
    nf = np.maximum(d, 1).astype(np.float32)
    large = MAX_EXACT + (np.log(nf / np.float32(MAX_EXACT)) / np.float32(math.log(MAX_DISTANCE / MAX_EXACT))
                         * np.float32(N_BUCKETS - MAX_EXACT)).astype(np.int32)
    b = np.where(d < MAX_EXACT, d, np.minimum(large, N_BUCKETS - 1))
    assert np.all(np.diff(b) >= 0)
    return [int(np.min(d[b >= k])) for k in range(N_BUCKETS)]


BUCKET_THR = _bucket_thresholds()
FAR_DIST = BUCKET_THR[-1]


def _cparams(sem):
    return pltpu.CompilerParams(dimension_semantics=sem, vmem_limit_bytes=VMEM_LIMIT_BYTES)


def _sigmoid(x):
    return 1.0 / (1.0 + jnp.exp(-x))


def _silu(x):
    return x * _sigmoid(x)


def _rmsnorm(x, g):
    return x * lax.rsqrt(jnp.mean(x * x, axis=-1, keepdims=True) + EPS) * g


def _dot(a, b):
    return jnp.dot(a, b, preferred_element_type=F32)


def _dot_nt(a, b):
    return lax.dot_general(a, b, (((1,), (1,)), ((), ())), preferred_element_type=F32)


def _rel_bias(dist, rb_ref, head):
    val = jnp.full(dist.shape, rb_ref[0, head], F32)
    for b in range(1, N_BUCKETS):
        val = jnp.where(dist >= BUCKET_THR[b], rb_ref[b, head], val)
    return val


def _rel_bias_heads(dist, rb_ref, heads):
    ge = [dist >= BUCKET_THR[b] for b in range(1, N_BUCKETS)]
    out = []
    for hd in heads:
        val = jnp.full(dist.shape, rb_ref[0, hd], F32)
        for b in range(1, N_BUCKETS):
            val = jnp.where(ge[b - 1], rb_ref[b, hd], val)
        out.append(val)
    return out


def _softmax_parts(s, mask):
    s = jnp.where(mask, s, NEG)
    m = jnp.max(s, axis=-1, keepdims=True)
    p = jnp.where(mask, jnp.exp(s - m), 0.0)
    return p, jnp.maximum(jnp.sum(p, axis=-1, keepdims=True), TINY)


def _online_update(carry, s, vt):
    m, l, acc = carry
    m_new = jnp.maximum(m, jnp.max(s, axis=-1, keepdims=True))
    alpha = jnp.exp(m - m_new)
    p = jnp.exp(s - m_new)
    return m_new, alpha * l + jnp.sum(p, axis=-1, keepdims=True), alpha * acc + _dot_nt(p.astype(BF16), vt)


def _mm_kernel(*refs, n_w, has_norm, has_bias, act, has_res, use_scratch):
    it = iter(refs)
    x_ref = next(it)
    g_ref = next(it) if has_norm else None
    w_refs = [next(it) for _ in range(n_w)]
    b_refs = [next(it) for _ in range(n_w)] if has_bias else []
    r_ref = next(it) if has_res else None
    o_ref = next(it)
    if use_scratch:
        xb_ref = next(it)

        @pl.when(pl.program_id(1) == 0)
        def _():
            x = x_ref[...].astype(F32)
            if has_norm:
                x = _rmsnorm(x, g_ref[...])
            xb_ref[...] = x.astype(BF16)

        xb = xb_ref[...]
    else:
        xb = x_ref[...]
    ys = [_dot(xb, w[...].astype(BF16)) for w in w_refs]
    if has_bias:
        ys = [y + b[...] for y, b in zip(ys, b_refs)]
    if act == "glu":
        y = ys[0] * _sigmoid(ys[1])
    elif act == "swiglu":
        y = _silu(ys[0]) * ys[1]
    else:
        y = ys[0]
    if has_res:
        y = y + r_ref[...]
    o_ref[...] = y.astype(o_ref.dtype)


def _mm(x, ws, n_out, *, tm, tn, name, norm_g=None, biases=None, act=None, residual=None, out_dtype=F32):
    m, k = x.shape
    assert m % tm == 0 and n_out % tn == 0
    use_scratch = norm_g is not None or x.dtype != BF16
    args, specs = [x], [pl.BlockSpec((tm, k), lambda i, j: (i, 0))]
    if norm_g is not None:
        args.append(norm_g.reshape(1, k))
        specs.append(pl.BlockSpec((1, k), lambda i, j: (0, 0)))
    for w, off in ws:
        assert off % tn == 0
        args.append(w)
        specs.append(pl.BlockSpec((k, tn), functools.partial(lambda i, j, o: (0, j + o), o=off // tn)))
    if biases is not None:
        for (bias, off) in biases:
            args.append(bias.reshape(1, -1))
            specs.append(pl.BlockSpec((1, tn), functools.partial(lambda i, j, o: (0, j + o), o=off // tn)))
    if residual is not None:
        args.append(residual)
        specs.append(pl.BlockSpec((tm, tn), lambda i, j: (i, j)))
    kern = functools.partial(_mm_kernel, n_w=len(ws), has_norm=norm_g is not None, has_bias=biases is not None,
                             act=act, has_res=residual is not None, use_scratch=use_scratch)
    return pl.pallas_call(
        kern,
        grid=(m // tm, n_out // tn),
        in_specs=specs,
        out_specs=pl.BlockSpec((tm, tn), lambda i, j: (i, j)),
        out_shape=jax.ShapeDtypeStruct((m, n_out), out_dtype),
        scratch_shapes=[pltpu.VMEM((tm, k), BF16)] if use_scratch else [],
        compiler_params=_cparams(("parallel", "arbitrary")),
        name=name,
    )(*args)


def _mm_t_kernel(x_ref, g_ref, wt_ref, o_ref, xb_ref):
    @pl.when(pl.program_id(1) == 0)
    def _():
        xb_ref[...] = _rmsnorm(x_ref[...], g_ref[...]).astype(BF16)

    o_ref[0] = _dot_nt(wt_ref[...].astype(BF16), xb_ref[...])


def _mm_t(x, wt, norm_g, *, seq, tm, tn, name):
    m, k = x.shape
    n = wt.shape[0]
    assert m % seq == 0 and seq % tm == 0 and n % tn == 0
    per = seq // tm
    return pl.pallas_call(
        _mm_t_kernel,
        grid=(m // tm, n // tn),
        in_specs=[pl.BlockSpec((tm, k), lambda i, j: (i, 0)),
                  pl.BlockSpec((1, k), lambda i, j: (0, 0)),
                  pl.BlockSpec((tn, k), lambda i, j: (j, 0))],
        out_specs=pl.BlockSpec((1, tn, tm), lambda i, j: (i // per, j, i % per)),
        out_shape=jax.ShapeDtypeStruct((m // seq, n, seq), F32),
        scratch_shapes=[pltpu.VMEM((tm, k), BF16)],
        compiler_params=_cparams(("parallel", "arbitrary")),
        name=name,
    )(x, norm_g.reshape(1, k), wt)


def _conv_kernel(main_ref, tail_ref, wdw_ref, bdw_ref, lng_ref, lnb_ref, o_ref, win_ref, c_ref, *, tt, rc):
    for c in range(D_MODEL // LANES):
        sl = slice(c * LANES, (c + 1) * LANES)
        win_ref[c, 0:tt, :] = main_ref[0, :, sl]
        win_ref[c, tt:tt + CONV_HALO, :] = tail_ref[0, :, sl]

    def row_chunk(r, carry):
        r0 = pl.multiple_of(r * rc, rc)
        for c in range(D_MODEL // LANES):
            sl = slice(c * LANES, (c + 1) * LANES)
            acc = jnp.zeros((rc, LANES), F32)
            for k in range(CONV_WIDTH):
                acc = acc + win_ref[c, pl.ds(r0 + (CONV_HALO - CONV_WIDTH + 1) + k, rc), :] * wdw_ref[k:k + 1, sl]
            c_ref[pl.ds(r0, rc), sl] = acc + bdw_ref[:, sl]
        return carry

    lax.fori_loop(0, tt // rc, row_chunk, 0)
    c = c_ref[...]
    mu = jnp.mean(c, axis=-1, keepdims=True)
    xc = c - mu
    y = xc * lax.rsqrt(jnp.mean(xc * xc, axis=-1, keepdims=True) + EPS)
    y = y * lng_ref[...] + lnb_ref[...]
    o_ref[0] = _silu(y).astype(o_ref.dtype)


def _conv_ln_swish(main, tail, main_map, tail_map, n_t, tt, rc, wdw, bdw, lng, lnb, name):
    bsz = main.shape[0]
    vec = pl.BlockSpec((1, D_MODEL), lambda b, i: (0, 0))
    return pl.pallas_call(
        functools.partial(_conv_kernel, tt=tt, rc=rc),
        grid=(bsz, n_t),
        in_specs=[pl.BlockSpec((1, tt, D_MODEL), main_map),
                  pl.BlockSpec((1, CONV_HALO, D_MODEL), tail_map),
                  pl.BlockSpec((CONV_WIDTH, D_MODEL), lambda b, i: (0, 0)), vec, vec, vec],
        out_specs=pl.BlockSpec((1, tt, D_MODEL), lambda b, i: (b, i, 0)),
        out_shape=jax.ShapeDtypeStruct((bsz, n_t * tt, D_MODEL), BF16),
        scratch_shapes=[pltpu.VMEM((D_MODEL // LANES, tt + CONV_HALO, LANES), F32), pltpu.VMEM((tt, D_MODEL), F32)],
        compiler_params=_cparams(("parallel", "parallel")),
        name=name,
    )(main, tail, wdw, bdw.reshape(1, -1), lng.reshape(1, -1), lnb.reshape(1, -1))


def _compress_kernel(pt_ref, *refs, n_pages):
    x_refs = refs[:n_pages]
    pe_ref, w1t_ref, w2_ref, o_ref, xs_ref = refs[n_pages:]
    hidden = w1t_ref.shape[3]
    rows = N_KV_HEADS * n_pages
    for j in range(n_pages):
        x = x_refs[j][0] + pe_ref[...]
        for s in range(2):
            for h in range(N_KV_HEADS):
                r0 = (s * N_KV_HEADS + h) * HEAD_DIM
                xs_ref[s, pl.ds((h * n_pages + j) * CMP_PITCH, HEAD_DIM), :] = x[r0:r0 + HEAD_DIM, :]
    zeros = jnp.zeros((BLOCK, hidden), F32)
    for s in range(2):
        def dim_body(d, acc, s=s):
            lhs = xs_ref[s, pl.ds(d, rows, stride=CMP_PITCH), :].astype(BF16)
            w = w1t_ref[s, d]
            w2blk = jnp.concatenate([jnp.concatenate([w, zeros], axis=1), jnp.concatenate([zeros, w], axis=1)],
                                    axis=0).astype(BF16)
            return acc + _dot(lhs, w2blk)

        hid = lax.fori_loop(0, HEAD_DIM, dim_body, jnp.zeros((rows, 2 * hidden), F32), unroll=8)
        hid = _silu(hid).astype(BF16)
        for blk in range(2):
            out = jnp.zeros((n_pages, HEADS_LANES), F32)
            for h in range(N_KV_HEADS):
                out = out + _dot(hid[h * n_pages:(h + 1) * n_pages, blk * hidden:(blk + 1) * hidden],
                                 w2_ref[s, h].astype(BF16))
            c0 = blk * 2 * HEADS_LANES + s * HEADS_LANES
            o_ref[:, c0:c0 + HEADS_LANES] = out


def _compress(src, page_ids, page_index, pe_t, w1t, w2pad, n_pages, name):
    n = page_ids.shape[0]
    assert n % n_pages == 0 and n_pages % 8 == 0
    x_specs = [pl.BlockSpec((1, 2 * HEADS_LANES, PAGE),
                            functools.partial(lambda i, pt, j: page_index(pt[i * n_pages + j]), j=j))
               for j in range(n_pages)]
    grid_spec = pltpu.PrefetchScalarGridSpec(
        num_scalar_prefetch=1,
        grid=(n // n_pages,),
        in_specs=x_specs + [
            pl.BlockSpec((2 * HEADS_LANES, PAGE), lambda i, pt: (0, 0)),
            pl.BlockSpec(w1t.shape, lambda i, pt: (0, 0, 0, 0)),
            pl.BlockSpec(w2pad.shape, lambda i, pt: (0, 0, 0, 0)),
        ],
        out_specs=pl.BlockSpec((n_pages, 4 * HEADS_LANES), lambda i, pt: (i, 0)),
        scratch_shapes=[pltpu.VMEM((2, N_KV_HEADS * n_pages * CMP_PITCH, LANES), F32)],
    )
    out = pl.pallas_call(
        functools.partial(_compress_kernel, n_pages=n_pages),
        grid_spec=grid_spec,
        out_shape=jax.ShapeDtypeStruct((n, 4 * HEADS_LANES), F32),
        compiler_params=_cparams(("parallel",)),
        name=name,
    )(page_ids, *([src] * n_pages), pe_t, w1t, w2pad)
    return out.reshape(2 * n, 2 * HEADS_LANES)


def _head_gate(sig, lane, g, branch, k):
    col = (g * 3 + branch) * N_KV_HEADS + k
    return jnp.sum(jnp.where(lane == col, sig, 0.0), axis=-1, keepdims=True)


def _pattn_kernel(rb_ref, q_ref, kc_ref, vc_ref, ks_ref, vs_ref, kw_ref, vw_ref, o_ref, strips_ref, *scratch,
                  tq, nc):
    qx_ref, p_ref, mv_ref, acc_ref, s_ref, mb_ref = (scratch[n * GROUP:(n + 1) * GROUP] for n in range(6))
    b, i, k = pl.program_id(0), pl.program_id(1), pl.program_id(2)
    rc = lax.broadcasted_iota(jnp.int32, (tq, tq), 0) - lax.broadcasted_iota(jnp.int32, (tq, tq), 1)
    n_back = WINDOW // tq

    @pl.when((b == 0) & (i == 0) & (k == 0))
    def _():
        def head_body(hh, carry):
            head = (hh & 3) * GROUP + (hh >> 2)
            far = rb_ref[N_BUCKETS - 1, head]
            strips_ref[hh * 4] = jnp.where(rc >= 0, _rel_bias(rc, rb_ref, head), -MASK_BIG)
            strips_ref[hh * 4 + 1] = _rel_bias(rc + tq, rb_ref, head)
            strips_ref[hh * 4 + 2] = jnp.full((tq, tq), far, F32)
            strips_ref[hh * 4 + 3] = jnp.where(rc + n_back * tq <= WINDOW, far, -MASK_BIG)
            return carry

        lax.fori_loop(0, N_HEADS, head_body, 0)

    @pl.when(k == 0)
    def _():
        o_ref[...] = jnp.zeros(o_ref.shape, o_ref.dtype)

    lane = lax.broadcasted_iota(jnp.int32, (1, HEADS_LANES), 1)
    head_mask = (lane // HEAD_DIM) == k
    glane = lax.broadcasted_iota(jnp.int32, (1, LANES), 1)
    sig = _sigmoid(q_ref[0, :, D_MODEL:D_MODEL + LANES])
    kc = kc_ref[0].astype(BF16)
    vc = vc_ref[0].astype(BF16)
    eye = (rc == 0).astype(BF16)
    qs = [jnp.where(head_mask, q_ref[0, :, g * HEADS_LANES:(g + 1) * HEADS_LANES] * (HEAD_DIM ** -0.5), 0.0)
          .astype(BF16) for g in range(GROUP)]

    qpos_t = i * tq + lax.broadcasted_iota(jnp.int32, (1, tq), 1)
    blk_t = lax.broadcasted_iota(jnp.int32, (nc, 1), 0)
    cmp_end_t = blk_t * BLOCK + (BLOCK - 1)
    mask_c = cmp_end_t <= qpos_t
    bias_c = _rel_bias_heads(qpos_t - cmp_end_t, rb_ref, [k * GROUP + g for g in range(GROUP)])
    imp = jnp.zeros((nc, tq), F32)
    for g in range(GROUP):
        s = jnp.where(mask_c, _dot_nt(kc, qs[g]) + bias_c[g], NEG)
        m = jnp.max(s, axis=0, keepdims=True)
        p = jnp.where(mask_c, jnp.exp(s - m), 0.0)
        p = p / jnp.maximum(jnp.sum(p, axis=0, keepdims=True), TINY)
        imp = imp + p
        p_rows = _dot_nt(eye, p.astype(BF16)).astype(BF16)
        sl = slice(g * HEADS_LANES, (g + 1) * HEADS_LANES)
        o_ref[0, :, sl] = o_ref[0, :, sl] + jnp.where(
            head_mask, _head_gate(sig, glane, g, 0, k) * _dot(p_rows, vc), 0.0)

    cur_t = qpos_t // BLOCK
    forced = (blk_t == cur_t) | (blk_t == cur_t - 1) | (blk_t == 0)
    score = jnp.where(blk_t > cur_t, NEG, jnp.where(forced, FORCE_SCORE, imp))
    sel_t = jnp.zeros((nc, tq), F32)
    for j in range(nc):
        row = score[j:j + 1, :]
        beats = (score > row) | ((score == row) & (blk_t < j))
        rank = jnp.sum(beats.astype(F32), axis=0, keepdims=True)
        sel_t = jnp.where(blk_t == j, (rank < TOP_N).astype(F32), sel_t)
    sel = _dot_nt(eye, sel_t.astype(BF16)).astype(BF16)

    kx = (k + 1) % N_KV_HEADS
    lsum_lane = lane == kx * HEAD_DIM
    place = (lax.broadcasted_iota(jnp.int32, (nc, HEADS_LANES), 1)
             == kx * HEAD_DIM + lax.broadcasted_iota(jnp.int32, (nc, HEADS_LANES), 0)).astype(BF16)
    unsel_lanes = _dot(1.0 - sel, place).astype(BF16)
    row = lax.broadcasted_iota(jnp.int32, (HEADS_LANES, 1), 0)
    slot_blk = row - kx * HEAD_DIM
    in_slot = (slot_blk >= 0) & (slot_blk < nc)
    key_blk = lax.broadcasted_iota(jnp.int32, (1, tq), 1) // BLOCK

    def tile(ref, j):
        return ref[0, :, pl.ds(pl.multiple_of(j * tq, tq), tq)]

    def k_sel(j):
        mask_rows = jnp.where(slot_blk == j * (tq // BLOCK) + key_blk, -MASK_BIG, 0.0)
        return jnp.where(in_slot, mask_rows, tile(ks_ref, j)).astype(BF16)

    def k_win(j):
        return tile(kw_ref, j).astype(BF16)

    def scores(g, kt, sidx):
        return _dot(qx_ref[g][...], kt) + strips_ref[(g * N_KV_HEADS + k) * 4 + sidx]

    def run_branch(queries, n_visits, k_tile, v_ref, strip_index, branch):
        for g in range(GROUP):
            qx_ref[g][...] = queries[g]
            mv_ref[g][...] = jnp.full((tq, tq // 2), NEG, F32)
            acc_ref[g][...] = jnp.zeros((tq, HEADS_LANES), F32)

        def max_visit(jj, carry):
            kt, sidx = k_tile(i - jj), strip_index(jj)
            for g in range(GROUP):
                s = scores(g, kt, sidx)
                s_ref[g][jj] = s
                mv_ref[g][...] = jnp.maximum(mv_ref[g][...], jnp.maximum(s[:, :tq // 2], s[:, tq // 2:]))
            return carry

        lax.fori_loop(0, n_visits, max_visit, 0)
        for g in range(GROUP):
            mb_ref[g][...] = jnp.broadcast_to(jnp.max(mv_ref[g][...], axis=-1, keepdims=True), (tq, tq))

        def acc_visit(jj, carry):
            vt = jnp.where(row == kx * HEAD_DIM, 1.0, tile(v_ref, i - jj)).astype(BF16)
            for g in range(GROUP):
                p_ref[g][...] = jnp.exp(s_ref[g][jj] - mb_ref[g][...]).astype(BF16)
            for g in range(GROUP):
                acc_ref[g][...] += _dot_nt(p_ref[g][...], vt)
            return carry

        lax.fori_loop(0, n_visits, acc_visit, 0)
        for g in range(GROUP):
            acc = acc_ref[g][...]
            l = jnp.sum(jnp.where(lsum_lane, acc, 0.0), axis=-1, keepdims=True)
            scale = _head_gate(sig, glane, g, branch, k) / jnp.maximum(l, TINY)
            sl = slice(g * HEADS_LANES, (g + 1) * HEADS_LANES)
            o_ref[0, :, sl] = o_ref[0, :, sl] + jnp.where(head_mask, scale * acc, 0.0)

    run_branch([qs[g] + unsel_lanes for g in range(GROUP)], i + 1, k_sel, vs_ref,
               lambda jj: jnp.minimum(jj, 2), 1)
    run_branch(qs, jnp.minimum(i, n_back) + 1, k_win, vw_ref,
               lambda jj: jnp.where(jj == n_back, 3, jnp.minimum(jj, 2)), 2)


def _prompt_attention(rel_bias, proj, cmp_kv, paged_t, win_t, *, tq):
    bsz, t, _ = proj.shape
    nc = cmp_kv.shape[1]
    assert t % tq == 0 and tq % BLOCK == 0 and WINDOW == 2 * tq and nc * BLOCK == t
    assert tq + 1 >= FAR_DIST

    def rows(idx, n):
        return pl.BlockSpec((1, n, HEADS_LANES), functools.partial(lambda b, i, k, c: (b, 0, c), c=idx))

    def cols(idx):
        return pl.BlockSpec((1, HEADS_LANES, t), functools.partial(lambda b, i, k, c: (b, c, 0), c=idx))

    return pl.pallas_call(
        functools.partial(_pattn_kernel, tq=tq, nc=nc),
        grid=(bsz, t // tq, N_KV_HEADS),
        in_specs=[pl.BlockSpec(memory_space=pltpu.SMEM),
                  pl.BlockSpec((1, tq, proj.shape[2]), lambda b, i, k: (b, i, 0)),
                  rows(0, nc), rows(1, nc), cols(2), cols(3), cols(0), cols(1)],
        out_specs=pl.BlockSpec((1, tq, D_MODEL), lambda b, i, k: (b, i, 0)),
        out_shape=jax.ShapeDtypeStruct((bsz, t, D_MODEL), F32),
        scratch_shapes=[pltpu.VMEM((4 * N_HEADS, tq, tq), F32)]
        + [pltpu.VMEM(shape, dtype) for shape, dtype in (
            ((tq, HEADS_LANES), BF16), ((tq, tq), BF16), ((tq, tq // 2), F32), ((tq, HEADS_LANES), F32),
            ((t // tq, tq, tq), F32), ((tq, tq), F32)) for _ in range(GROUP)],
        compiler_params=_cparams(("arbitrary", "arbitrary", "arbitrary")),
        name="prompt_attention",
    )(rel_bias, proj, cmp_kv, cmp_kv, paged_t, paged_t, win_t, win_t)


SROWS = 8
ALL_ROWS = N_HEADS * SROWS


def _all_head_queries(q_ref):
    lane = lax.broadcasted_iota(jnp.int32, (1, HEADS_LANES), 1)
    parts = []
    for k in range(N_KV_HEADS):
        for g in range(GROUP):
            qg = q_ref[0, :, g * HEADS_LANES:(g + 1) * HEADS_LANES] * (HEAD_DIM ** -0.5)
            parts.append(jnp.where((lane // HEAD_DIM) == k, qg, 0.0))
    return jnp.concatenate(parts, axis=0).astype(BF16)


def _all_head_bias(dist, rb_ref):
    return jnp.concatenate(_rel_bias_heads(dist, rb_ref, list(range(N_HEADS))), axis=0)


def _sattn_a_kernel(rb_ref, q_ref, cmp_ref, win_ref, part_ref, sel_ref, *, pos0, ts, wp, nc):
    lane = lax.broadcasted_iota(jnp.int32, (1, HEADS_LANES), 1)
    sig = _sigmoid(q_ref[0, :, D_MODEL:D_MODEL + LANES])
    q = _all_head_queries(q_ref)
    qpos = pos0 + lax.broadcasted_iota(jnp.int32, (SROWS, 1), 0)
    tile_rows = lambda a: jnp.concatenate([a] * N_HEADS, axis=0)

    blk = lax.broadcasted_iota(jnp.int32, (1, nc), 1)
    cmp_end = blk * BLOCK + (BLOCK - 1)
    s = _dot_nt(q, cmp_ref[0, :, 0:HEADS_LANES].astype(BF16)) + _all_head_bias(qpos - cmp_end, rb_ref)
    p, l = _softmax_parts(s, tile_rows(cmp_end <= qpos))
    p = p / l
    o_c = _dot(p.astype(BF16), cmp_ref[0, :, HEADS_LANES:2 * HEADS_LANES].astype(BF16))

    imp = jnp.concatenate(
        [sum(p[(k * GROUP + g) * SROWS:(k * GROUP + g + 1) * SROWS] for g in range(GROUP))
         for k in range(N_KV_HEADS)], axis=0)
    cur = jnp.concatenate([qpos // BLOCK] * N_KV_HEADS, axis=0)
    forced = (blk == cur) | (blk == cur - 1) | (blk == 0)
    score = jnp.where(blk > cur, NEG, jnp.where(forced, FORCE_SCORE, imp))
    sel = jnp.zeros(score.shape, F32)
    for _ in range(TOP_N - 1):
        m = jnp.max(score, axis=-1, keepdims=True)
        idx = jnp.min(jnp.where(score == m, blk, nc), axis=-1, keepdims=True)
        hit = blk == idx
        sel = jnp.where(hit, 1.0, sel)
        score = jnp.where(hit, -jnp.inf, score)
    for k in range(N_KV_HEADS):
        sel_ref[0, :, k * nc:(k + 1) * nc] = sel[k * SROWS:(k + 1) * SROWS]

    wk = win_ref.shape[2]
    j = lax.broadcasted_iota(jnp.int32, (1, wk), 1)
    kw_pos = pos0 - wp + j
    dist = qpos - kw_pos
    mask = (dist >= 0) & (dist <= WINDOW) & (kw_pos >= 0) & (j < wp + ts)
    s = _dot(q, win_ref[0, 0:HEADS_LANES, :].astype(BF16)) + _all_head_bias(dist, rb_ref)
    p, l = _softmax_parts(s, tile_rows(mask))
    o_w = _dot_nt((p / l).astype(BF16), win_ref[0, HEADS_LANES:2 * HEADS_LANES, :].astype(BF16))

    for g in range(GROUP):
        chunk = jnp.zeros((SROWS, HEADS_LANES), F32)
        for k in range(N_KV_HEADS):
            r = slice((k * GROUP + g) * SROWS, (k * GROUP + g + 1) * SROWS)
            c_cmp, c_win = (g * 3 + 0) * N_KV_HEADS + k, (g * 3 + 2) * N_KV_HEADS + k
            comb = sig[:, c_cmp:c_cmp + 1] * o_c[r] + sig[:, c_win:c_win + 1] * o_w[r]
            chunk = chunk + jnp.where((lane // HEAD_DIM) == k, comb, 0.0)
        part_ref[0, :, g * HEADS_LANES:(g + 1) * HEADS_LANES] = chunk


def _sample_attention_a(rel_bias, q, cmp_kv, win_t, *, pos0, ts, wp):
    bsz = q.shape[0]
    nc = cmp_kv.shape[1]
    assert pos0 % BLOCK == 0 and ts < BLOCK and pos0 // BLOCK == nc and ts <= SROWS and TOP_N >= 3
    whole = lambda arr: pl.BlockSpec((1,) + arr.shape[1:], lambda b: (b, 0, 0))
    return pl.pallas_call(
        functools.partial(_sattn_a_kernel, pos0=pos0, ts=ts, wp=wp, nc=nc),
        grid=(bsz,),
        in_specs=[pl.BlockSpec(memory_space=pltpu.SMEM), whole(q), whole(cmp_kv), whole(win_t)],
        out_specs=[pl.BlockSpec((1, SROWS, D_MODEL), lambda b: (b, 0, 0)),
                   pl.BlockSpec((1, SROWS, N_KV_HEADS * nc), lambda b: (b, 0, 0))],
        out_shape=[jax.ShapeDtypeStruct((bsz, SROWS, D_MODEL), F32),
                   jax.ShapeDtypeStruct((bsz, SROWS, N_KV_HEADS * nc), F32)],
        compiler_params=_cparams(("parallel",)),
        name="sample_attention_a",
    )(rel_bias, q, cmp_kv, win_t)


def _sattn_b_kernel(pt_ref, rb_ref, q_ref, sel_ref, part_ref, knew_ref, *refs, pos0, ts, nc, n_pg):
    k_refs, v_refs = refs[:n_pg], refs[n_pg:2 * n_pg]
    o_ref, qall_ref, selrows_ref, bfar_ref, m_ref, l_ref, acc_ref = refs[2 * n_pg:]
    p_idx = pl.program_id(1)
    keys = n_pg * PAGE
    qpos = pos0 + lax.broadcasted_iota(jnp.int32, (SROWS, 1), 0)

    @pl.when(p_idx == 0)
    def _():
        qall_ref[...] = _all_head_queries(q_ref)
        for k in range(N_KV_HEADS):
            for g in range(GROUP):
                r = slice((k * GROUP + g) * SROWS, (k * GROUP + g + 1) * SROWS)
                selrows_ref[r, :] = sel_ref[0, :, k * nc:(k + 1) * nc]
                bfar_ref[r, :] = jnp.full((SROWS, 1), rb_ref[N_BUCKETS - 1, k * GROUP + g], F32)
        m_ref[...] = jnp.full(m_ref.shape, NEG, F32)
        l_ref[...] = jnp.zeros(l_ref.shape, F32)
        acc_ref[...] = jnp.zeros(acc_ref.shape, F32)

    qall = qall_ref[...]

    def update(s, vt):
        m, l, acc = _online_update((m_ref[...], l_ref[...], acc_ref[...]), s, vt)
        m_ref[...], l_ref[...], acc_ref[...] = m, l, acc

    page0 = p_idx * n_pg
    kt = jnp.concatenate([r[0] for r in k_refs], axis=1).astype(BF16)
    vt = jnp.concatenate([r[0] for r in v_refs], axis=1).astype(BF16)
    col = lax.broadcasted_iota(jnp.int32, (1, keys), 1)
    far = pos0 - (page0 * PAGE + keys - 1) >= FAR_DIST
    bias = lax.cond(far, lambda: jnp.broadcast_to(bfar_ref[...], (ALL_ROWS, keys)),
                    lambda: _all_head_bias(qpos - (page0 * PAGE + col), rb_ref))
    blk_rows = lax.broadcasted_iota(jnp.int32, (nc, 1), 0)
    expand = jnp.where(blk_rows == 2 * page0 + col // BLOCK, MASK_BIG, 0.0).astype(BF16)
    unselected = _dot(selrows_ref[...].astype(BF16), expand) - MASK_BIG
    update(_dot(qall, kt) + bias + unselected, vt)

    @pl.when(p_idx == pl.num_programs(1) - 1)
    def _():
        ncol = lax.broadcasted_iota(jnp.int32, (1, PAGE), 1)
        dist = qpos - (pos0 + ncol)
        add = _all_head_bias(dist, rb_ref) + jnp.concatenate(
            [jnp.where((dist >= 0) & (ncol < ts), 0.0, -MASK_BIG)] * N_HEADS, axis=0)
        update(_dot(qall, knew_ref[0, 0:HEADS_LANES, :].astype(BF16)) + add,
               knew_ref[0, HEADS_LANES:2 * HEADS_LANES, :].astype(BF16))
        o_sel = acc_ref[...] / jnp.maximum(l_ref[...], TINY)
        sig = _sigmoid(q_ref[0, :, D_MODEL:D_MODEL + LANES])
        lane = lax.broadcasted_iota(jnp.int32, (1, HEADS_LANES), 1)
        for g in range(GROUP):
            sl = slice(g * HEADS_LANES, (g + 1) * HEADS_LANES)
            chunk = part_ref[0, :, sl]
            for k in range(N_KV_HEADS):
                r = slice((k * GROUP + g) * SROWS, (k * GROUP + g + 1) * SROWS)
                c = (g * 3 + 1) * N_KV_HEADS + k
                chunk = chunk + jnp.where((lane // HEAD_DIM) == k, sig[:, c:c + 1] * o_sel[r], 0.0)
            o_ref[0, :, sl] = chunk


def _sample_attention_b(rel_bias, page_ids, q, sel, part, knew_t, cache_t, *, pos0, ts, n_pg):
    bsz = q.shape[0]
    n_pages = page_ids.shape[0] // bsz
    nc = sel.shape[2] // N_KV_HEADS
    assert n_pages % n_pg == 0 and n_pages * 2 == nc and pos0 == n_pages * PAGE and TOP_N >= 3

    def page_spec(slot, j):
        return pl.BlockSpec((1, HEADS_LANES, PAGE),
                            functools.partial(lambda b, p, pt, j, c: (pt[b * n_pages + p * n_pg + j], c, 0),
                                              j=j, c=slot))

    whole = lambda arr: pl.BlockSpec((1,) + arr.shape[1:], lambda b, p, pt: (b, 0, 0))
    grid_spec = pltpu.PrefetchScalarGridSpec(
        num_scalar_prefetch=1,
        grid=(bsz, n_pages // n_pg),
        in_specs=[pl.BlockSpec(memory_space=pltpu.SMEM), whole(q), whole(sel), whole(part), whole(knew_t)]
        + [page_spec(2, j) for j in range(n_pg)] + [page_spec(3, j) for j in range(n_pg)],
        out_specs=pl.BlockSpec((1, SROWS, D_MODEL), lambda b, p, pt: (b, 0, 0)),
        scratch_shapes=[pltpu.VMEM((ALL_ROWS, HEADS_LANES), BF16), pltpu.VMEM((ALL_ROWS, nc), F32),
                        pltpu.VMEM((ALL_ROWS, 1), F32), pltpu.VMEM((ALL_ROWS, 1), F32),
                        pltpu.VMEM((ALL_ROWS, 1), F32), pltpu.VMEM((ALL_ROWS, HEADS_LANES), F32)],
    )
    return pl.pallas_call(
        functools.partial(_sattn_b_kernel, pos0=pos0, ts=ts, nc=nc, n_pg=n_pg),
        grid_spec=grid_spec,
        out_shape=jax.ShapeDtypeStruct((bsz, SROWS, D_MODEL), F32),
        compiler_params=_cparams(("parallel", "arbitrary")),
        name="sample_attention_b",
    )(page_ids, rel_bias, q, sel, part, knew_t, *([cache_t] * (2 * n_pg)))


def _router_kernel(h_ref, g_ref, wr_ref, xn_ref, comb_ref):
    xn = _rmsnorm(h_ref[...], g_ref[...]).astype(BF16)
    xn_ref[...] = xn
    logits = _dot(xn, wr_ref[...].astype(BF16))
    lane = lax.broadcasted_iota(jnp.int32, logits.shape, 1)
    lg = jnp.where(lane < N_EXPERTS, logits, -jnp.inf)
    m1 = jnp.max(lg, axis=-1, keepdims=True)
    i1 = jnp.min(jnp.where(lg == m1, lane, LANES), axis=-1, keepdims=True)
    lg2 = jnp.where(lane == i1, -jnp.inf, lg)
    m2 = jnp.max(lg2, axis=-1, keepdims=True)
    i2 = jnp.min(jnp.where(lg2 == m2, lane, LANES), axis=-1, keepdims=True)
    e = jnp.exp(m2 - m1)
    comb_ref[...] = jnp.where(lane == i1, 1.0 / (1.0 + e), 0.0) + jnp.where(lane == i2, e / (1.0 + e), 0.0)


def _router(h, g, w_router_pad, tm, name):
    m = h.shape[0]
    return pl.pallas_call(
        _router_kernel,
        grid=(m // tm,),
        in_specs=[pl.BlockSpec((tm, D_MODEL), lambda i: (i, 0)),
                  pl.BlockSpec((1, D_MODEL), lambda i: (0, 0)),
                  pl.BlockSpec((D_MODEL, LANES), lambda i: (0, 0))],
        out_specs=[pl.BlockSpec((tm, D_MODEL), lambda i: (i, 0)), pl.BlockSpec((tm, LANES), lambda i: (i, 0))],
        out_shape=[jax.ShapeDtypeStruct((m, D_MODEL), BF16), jax.ShapeDtypeStruct((m, LANES), F32)],
        compiler_params=_cparams(("parallel",)),
        name=name,
    )(h, g.reshape(1, -1), w_router_pad)


def _moe_kernel(xn_ref, comb_ref, h_ref, wg_ref, wu_ref, wd_ref, gf_ref, y_ref, acc_ref):
    e, f = pl.program_id(1), pl.program_id(2)
    last_e, last_f = pl.num_programs(1) - 1, pl.num_programs(2) - 1

    @pl.when((e == 0) & (f == 0))
    def _():
        y_ref[...] = h_ref[...]

    @pl.when(f == 0)
    def _():
        acc_ref[...] = jnp.zeros(acc_ref.shape, F32)

    x = xn_ref[...]
    hid = _silu(_dot(x, wg_ref[0].astype(BF16))) * _dot(x, wu_ref[0].astype(BF16))
    acc_ref[...] += _dot(hid.astype(BF16), wd_ref[0].astype(BF16))

    @pl.when(f == last_f)
    def _():
        lane = lax.broadcasted_iota(jnp.int32, comb_ref.shape, 1)
        ce = jnp.sum(jnp.where(lane == e, comb_ref[...], 0.0), axis=-1, keepdims=True)
        y_ref[...] += ce * acc_ref[...]

    @pl.when((e == last_e) & (f == last_f))
    def _():
        y_ref[...] = _rmsnorm(y_ref[...], gf_ref[...])


def _moe(xn, comb, h, wg, wu, wd, g_final, tm, tf, name):
    m = h.shape[0]
    n_e, _, d_ff = wg.shape
    assert m % tm == 0 and d_ff % tf == 0
    row = lambda i, e, f: (i, 0)
    return pl.pallas_call(
        _moe_kernel,
        grid=(m // tm, n_e, d_ff // tf),
        in_specs=[pl.BlockSpec((tm, D_MODEL), row), pl.BlockSpec((tm, LANES), row), pl.BlockSpec((tm, D_MODEL), row),
                  pl.BlockSpec((1, D_MODEL, tf), lambda i, e, f: (e, 0, f)),
                  pl.BlockSpec((1, D_MODEL, tf), lambda i, e, f: (e, 0, f)),
                  pl.BlockSpec((1, tf, D_MODEL), lambda i, e, f: (e, f, 0)),
                  pl.BlockSpec((1, D_MODEL), lambda i, e, f: (0, 0))],
        out_specs=pl.BlockSpec((tm, D_MODEL), row),
        out_shape=jax.ShapeDtypeStruct((m, D_MODEL), F32),
        scratch_shapes=[pltpu.VMEM((tm, D_MODEL), F32)],
        compiler_params=_cparams(("parallel", "arbitrary", "arbitrary")),
        name=name,
    )(xn, comb, h, wg, wu, wd, g_final.reshape(1, -1))


def _route_kernel(h_ref, g_ref, wr_ref, xn_ref, idx_ref, w_ref):
    xn = _rmsnorm(h_ref[...], g_ref[...])
    xn_ref[...] = xn
    logits = _dot(xn.astype(BF16), wr_ref[...].astype(BF16))
    lane = lax.broadcasted_iota(jnp.int32, logits.shape, 1)
    lg = jnp.where(lane < N_EXPERTS, logits, -jnp.inf)
    m1 = jnp.max(lg, axis=-1, keepdims=True)
    i1 = jnp.min(jnp.where(lg == m1, lane, LANES), axis=-1, keepdims=True)
    lg2 = jnp.where(lane == i1, -jnp.inf, lg)
    m2 = jnp.max(lg2, axis=-1, keepdims=True)
    i2 = jnp.min(jnp.where(lg2 == m2, lane, LANES), axis=-1, keepdims=True)
    e = jnp.exp(m2 - m1)
    idx_ref[...] = jnp.where(lane == 0, i1, jnp.where(lane == 1, i2, 0))
    w_ref[...] = jnp.where(lane == 0, 1.0 / (1.0 + e), jnp.where(lane == 1, e / (1.0 + e), 0.0))


def _route(h, g, w_router_pad, tm, name):
    m = h.shape[0]
    row = lambda n: pl.BlockSpec((tm, n), lambda i: (i, 0))
    return pl.pallas_call(
        _route_kernel,
        grid=(m // tm,),
        in_specs=[row(D_MODEL), pl.BlockSpec((1, D_MODEL), lambda i: (0, 0)),
                  pl.BlockSpec((D_MODEL, LANES), lambda i: (0, 0))],
        out_specs=[row(D_MODEL), row(LANES), row(LANES)],
        out_shape=[jax.ShapeDtypeStruct((m, D_MODEL), F32), jax.ShapeDtypeStruct((m, LANES), jnp.int32),
                   jax.ShapeDtypeStruct((m, LANES), F32)],
        compiler_params=_cparams(("parallel",)),
        name=name,
    )(h, g.reshape(1, -1), w_router_pad)


def _gather_rows_kernel(idx_ref, src_ref, o_ref, sem, *, rows):
    base = pl.program_id(0) * rows

    def row_copy(r, src_row):
        return pltpu.make_async_copy(src_ref.at[pl.ds(src_row, 1)], o_ref.at[pl.ds(r, 1)], sem)

    def issue(r, carry):
        row_copy(r, idx_ref[base + r]).start()
        return carry

    def drain(r, carry):
        row_copy(r, 0).wait()
        return carry

    lax.fori_loop(0, rows, issue, 0, unroll=8)
    lax.fori_loop(0, rows, drain, 0, unroll=8)


def _gather_rows(src, idx, rows, name):
    n = idx.shape[0]
    assert n % rows == 0
    width = src.shape[1]
    grid_spec = pltpu.PrefetchScalarGridSpec(
        num_scalar_prefetch=1,
        grid=(n // rows,),
        in_specs=[pl.BlockSpec(memory_space=pl.ANY)],
        out_specs=pl.BlockSpec((rows, width), lambda i, idx_ref: (i, 0)),
        scratch_shapes=[pltpu.SemaphoreType.DMA(())],
    )
    return pl.pallas_call(
        functools.partial(_gather_rows_kernel, rows=rows),
        grid_spec=grid_spec,
        out_shape=jax.ShapeDtypeStruct((n, width), src.dtype),
        compiler_params=_cparams(("arbitrary",)),
        name=name,
    )(idx, src)


def _experts_kernel(te_ref, tv_ref, x_ref, wg_ref, wu_ref, wd_ref, o_ref, *, tf):
    i = pl.program_id(0)

    @pl.when(tv_ref[i] == 0)
    def _():
        o_ref[...] = jnp.zeros(o_ref.shape, F32)

    @pl.when(tv_ref[i] != 0)
    def _():
        x = x_ref[...].astype(BF16)
        acc = jnp.zeros(o_ref.shape, F32)
        for f in range(wg_ref.shape[2] // tf):
            sl = slice(f * tf, (f + 1) * tf)
            hid = _silu(_dot(x, wg_ref[0, :, sl].astype(BF16))) * _dot(x, wu_ref[0, :, sl].astype(BF16))
            acc = acc + _dot(hid.astype(BF16), wd_ref[0, sl, :].astype(BF16))
        o_ref[...] = acc


def _experts(xg, tile_expert, tile_valid, wg, wu, wd, rows, tf, name):
    n = xg.shape[0]
    n_e, _, d_ff = wg.shape
    assert n % rows == 0 and d_ff % tf == 0
    once = pl.Buffered(1)
    grid_spec = pltpu.PrefetchScalarGridSpec(
        num_scalar_prefetch=2,
        grid=(n // rows,),
        in_specs=[pl.BlockSpec((rows, D_MODEL), lambda i, te, tv: (i, 0)),
                  pl.BlockSpec((1, D_MODEL, d_ff), lambda i, te, tv: (te[i], 0, 0), pipeline_mode=once),
                  pl.BlockSpec((1, D_MODEL, d_ff), lambda i, te, tv: (te[i], 0, 0), pipeline_mode=once),
                  pl.BlockSpec((1, d_ff, D_MODEL), lambda i, te, tv: (te[i], 0, 0), pipeline_mode=once)],
        out_specs=pl.BlockSpec((rows, D_MODEL), lambda i, te, tv: (i, 0)),
    )
    return pl.pallas_call(
        functools.partial(_experts_kernel, tf=tf),
        grid_spec=grid_spec,
        out_shape=jax.ShapeDtypeStruct((n, D_MODEL), F32),
        compiler_params=_cparams(("arbitrary",)),
        name=name,
    )(tile_expert, tile_valid, xg, wg, wu, wd)


def _moe_combine_kernel(h_ref, y2_ref, w_ref, gf_ref, o_ref):
    w = w_ref[...]
    y = h_ref[...] + w[:, 0:1] * y2_ref[:, 0:D_MODEL] + w[:, 1:2] * y2_ref[:, D_MODEL:2 * D_MODEL]
    o_ref[...] = _rmsnorm(y, gf_ref[...])


def _moe_combine(h, y2, w, g_final, tm, name):
    m = h.shape[0]
    row = lambda n: pl.BlockSpec((tm, n), lambda i: (i, 0))
    return pl.pallas_call(
        _moe_combine_kernel,
        grid=(m // tm,),
        in_specs=[row(D_MODEL), row(2 * D_MODEL), row(LANES), pl.BlockSpec((1, D_MODEL), lambda i: (0, 0))],
        out_specs=row(D_MODEL),
        out_shape=jax.ShapeDtypeStruct((m, D_MODEL), F32),
        compiler_params=_cparams(("parallel",)),
        name=name,
    )(h, y2, w, g_final.reshape(1, -1))


def _moe_grouped(h, p, tag, rows=512):
    m = h.shape[0]
    tm = _row_tile(m, 1024)
    xn, idx, w = _route(h, p["norm_ffn"][1], p["w_router_pad"], tm, "route_" + tag)
    n_e = p["moe_w_gate"].shape[1]
    expert = idx[:, :2].reshape(-1)
    onehot = (expert[:, None] == jnp.arange(n_e, dtype=jnp.int32)[None, :]).astype(jnp.int32)
    pos = jnp.sum((jnp.cumsum(onehot, axis=0) - 1) * onehot, axis=1)
    counts = jnp.sum(onehot, axis=0)
    padded = (counts + rows - 1) // rows * rows
    ends = jnp.cumsum(padded)
    row_of = (ends - padded)[expert] + pos
    n_tiles = (2 * m + n_e * (rows - 1)) // rows + 1
    src = jnp.zeros((n_tiles * rows,), jnp.int32).at[row_of].set(jnp.arange(2 * m, dtype=jnp.int32) // 2)
    tile_start = jnp.arange(n_tiles, dtype=jnp.int32) * rows
    tile_expert = jnp.minimum(jnp.sum((tile_start[:, None] >= ends[None, :]).astype(jnp.int32), axis=1), n_e - 1)
    tile_valid = (tile_start < ends[-1]).astype(jnp.int32)
    xg = _gather_rows(xn, src, rows, "moe_gather_" + tag)
    yg = _experts(xg, tile_expert, tile_valid, p["moe_w_gate"][0], p["moe_w_up"][0], p["moe_w_down"][0], rows, 256,
                  "moe_experts_" + tag)
    y2 = _gather_rows(yg, row_of, rows, "moe_ungather_" + tag).reshape(m, 2 * D_MODEL)
    return _moe_combine(h, y2, w, p["norm_final"], tm, "moe_combine_" + tag)


def _compress_params(cmp_pos, cmp_w1, cmp_w2):
    pe = jnp.transpose(cmp_pos, (0, 2, 1))[:, None, :, :]
    pe = jnp.broadcast_to(pe, (2, N_KV_HEADS, HEAD_DIM, BLOCK))
    pe_t = jnp.concatenate([pe, pe], axis=-1).reshape(2 * HEADS_LANES, PAGE)
    hidden = cmp_w1.shape[2]
    w1t = cmp_w1.reshape(2, BLOCK, HEAD_DIM, hidden).transpose(0, 2, 1, 3)
    w2pad = jnp.zeros((2, N_KV_HEADS, hidden, HEADS_LANES), F32)
    for h in range(N_KV_HEADS):
        w2pad = w2pad.at[:, h, :, h * HEAD_DIM:(h + 1) * HEAD_DIM].set(cmp_w2)
    return pe_t, w1t, w2pad


def _attn_weights(w_in, w_out):
    hd = N_HEADS * HEAD_DIM
    wq = w_in[:, :hd].reshape(D_MODEL, N_KV_HEADS, GROUP, HEAD_DIM).transpose(0, 2, 1, 3).reshape(D_MODEL, hd)
    wg = w_in[:, hd:].reshape(D_MODEL, N_KV_HEADS, GROUP, 3).transpose(0, 2, 3, 1).reshape(D_MODEL, 3 * N_HEADS)
    w_in_pad = jnp.concatenate([wq, wg, jnp.zeros((D_MODEL, LANES - 3 * N_HEADS), F32)], axis=1)
    w_out_perm = w_out.reshape(N_KV_HEADS, GROUP, HEAD_DIM, D_MODEL).transpose(1, 0, 2, 3).reshape(hd, D_MODEL)
    return w_in_pad, w_out_perm


def _row_tile(m, cap):
    t = cap
    while m % t:
        t //= 2
    return t


def _from_rows_t(a_t, n_slots):
    bsz, _, t = a_t.shape
    return a_t.reshape(bsz, n_slots, N_KV_HEADS, HEAD_DIM, t).transpose(0, 4, 1, 2, 3)


def _to_rows_t(a):
    bsz, t, n_slots = a.shape[:3]
    return a.transpose(0, 2, 3, 4, 1).reshape(bsz, n_slots * HEADS_LANES, t)


def _layer0_front(x, p, tag):
    m = x.shape[0]
    w1, b1 = p["conv_w_pw1"][0], p["conv_b_pw1"][0]
    return _mm(x, [(w1, 0), (w1, D_MODEL)], D_MODEL, tm=_row_tile(m, 1024), tn=512, norm_g=p["norm_mix"][0],
               biases=[(b1, 0), (b1, D_MODEL)], act="glu", name="pw1_glu_" + tag)


def _layer0_back(x, c_act, p, seq, tag):
    m = x.shape[0]
    tm = _row_tile(m, 1024)
    h1 = _mm(c_act, [(p["conv_w_pw2"][0], 0)], D_MODEL, tm=tm, tn=512, biases=[(p["conv_b_pw2"][0], 0)], residual=x,
             name="pw2_" + tag)
    d_ff = p["ffn_w_gate"].shape[2]
    a = _mm(h1, [(p["ffn_w_gate"][0], 0), (p["ffn_w_up"][0], 0)], d_ff, tm=tm, tn=256, norm_g=p["norm_ffn"][0],
            act="swiglu", out_dtype=BF16, name="ffn_up_" + tag)
    h2 = _mm(a, [(p["ffn_w_down"][0], 0)], D_MODEL, tm=_row_tile(m, 512), tn=512, residual=h1, name="ffn_down_" + tag)
    tmt = _row_tile(seq, 1024)
    paged_t = _mm_t(h2, p["w_kv_t"][:PAGED_ROWS], p["norm_kv"], seq=seq, tm=tmt, tn=512, name="kv_paged_" + tag)
    win_t = _mm_t(h2, p["w_kv_t"][PAGED_ROWS:], p["norm_kv"], seq=seq, tm=tmt, tn=512, name="kv_win_" + tag)
    proj = _mm(h2, [(p["w_in_pad"], 0)], p["w_in_pad"].shape[1], tm=tm, tn=384, norm_g=p["norm_mix"][1],
               name="q_proj_" + tag)
    return h2, paged_t, win_t, proj


def _layer1_back(o, h2, p, tag):
    m = h2.shape[0]
    tm = _row_tile(m, 1024)
    h3 = _mm(o, [(p["w_out_perm"], 0)], D_MODEL, tm=tm, tn=512, residual=h2, name="attn_out_" + tag)
    if m >= GROUPED_MOE_MIN_TOKENS:
        return _moe_grouped(h3, p, tag)
    xn, comb = _router(h3, p["norm_ffn"][1], p["w_router_pad"], tm, "router_" + tag)
    return _moe(xn, comb, h3, p["moe_w_gate"][0], p["moe_w_up"][0], p["moe_w_down"][0], p["norm_final"], tm, 256,
                "moe_" + tag)


def kernel(x_prompt, x_sample, cache_kv, state_win_kv, state_conv, page_table, norm_mix, norm_ffn, norm_kv, norm_final, conv_w_pw1, conv_b_pw1, conv_w_dw, conv_b_dw, conv_ln_g, conv_ln_b, conv_w_pw2, conv_b_pw2, w_kv, cmp_pos, cmp_w1, cmp_w2, nsa_w_in, nsa_w_out, rel_bias, ffn_w_gate, ffn_w_up, ffn_w_down, moe_w_router, moe_w_gate, moe_w_up, moe_w_down):
    p = dict(norm_mix=norm_mix, norm_ffn=norm_ffn, norm_kv=norm_kv, norm_final=norm_final,
             conv_w_pw1=conv_w_pw1, conv_b_pw1=conv_b_pw1, conv_w_pw2=conv_w_pw2, conv_b_pw2=conv_b_pw2,
             ffn_w_gate=ffn_w_gate, ffn_w_up=ffn_w_up, ffn_w_down=ffn_w_down,
             moe_w_gate=moe_w_gate, moe_w_up=moe_w_up, moe_w_down=moe_w_down)
    p["w_kv_t"] = w_kv.T
    p["w_in_pad"], p["w_out_perm"] = _attn_weights(nsa_w_in[0], nsa_w_out[0])
    p["w_router_pad"] = jnp.pad(moe_w_router[0], ((0, 0), (0, LANES - N_EXPERTS)))
    pe_t, w1t, w2pad = _compress_params(cmp_pos, cmp_w1, cmp_w2)
    conv_args = (conv_w_dw[0], conv_b_dw[0], conv_ln_g[0], conv_ln_b[0])
    n_state = CONV_WIDTH - 1

    bp, tp, _ = x_prompt.shape
    mp = bp * tp
    xp = x_prompt.reshape(mp, D_MODEL)
    glu = _layer0_front(xp, p, "p").reshape(bp, tp, D_MODEL)
    gpad = jnp.concatenate([jnp.zeros((bp, CONV_HALO, D_MODEL), F32), glu], axis=1)
    tt = 256
    c_act = _conv_ln_swish(gpad, gpad, lambda b, i: (b, i, 0), lambda b, i: (b, (i + 1) * (tt // CONV_HALO), 0),
                           tp // tt, tt, 32, *conv_args, "conv_p")
    h2, paged_t, win_t, proj = _layer0_back(xp, c_act.reshape(mp, D_MODEL), p, tp, "p")
    ppb = tp // PAGE
    cmp_p = _compress(paged_t, jnp.arange(bp * ppb, dtype=jnp.int32), lambda pid: (pid // ppb, 0, pid % ppb),
                      pe_t, w1t, w2pad, _row_tile(bp * ppb, 32), "compress_p").reshape(bp, tp // BLOCK, -1)
    o = _prompt_attention(rel_bias, proj.reshape(bp, tp, -1), cmp_p, paged_t, win_t, tq=256)
    y_p = _layer1_back(o.reshape(mp, D_MODEL), h2, p, "p").reshape(bp, tp, D_MODEL)
    kv_p = _from_rows_t(paged_t, 4)
    keep_p = min(WINDOW, tp)
    win_p = _from_rows_t(win_t[:, :, tp - keep_p:], 2)
    conv_p = glu[:, tp - n_state:][None]

    bs, ts, _ = x_sample.shape
    ms = bs * ts
    n_pages = page_table.shape[1]
    assert cache_kv.shape[1] == PAGE
    pos0 = n_pages * PAGE
    xs = x_sample.reshape(ms, D_MODEL)
    glu_s = _layer0_front(xs, p, "s").reshape(bs, ts, D_MODEL)
    full_s = jnp.concatenate([state_conv[0], glu_s], axis=1)
    conv_s = full_s[:, full_s.shape[1] - n_state:][None]
    gpad_s = jnp.concatenate([jnp.zeros((bs, CONV_HALO - n_state, D_MODEL), F32), full_s,
                              jnp.zeros((bs, SROWS - ts, D_MODEL), F32)], axis=1)
    once = lambda b, i: (b, 0, 0)
    c_act_s = _conv_ln_swish(gpad_s[:, :SROWS], gpad_s[:, SROWS:], once, once, 1, SROWS, SROWS, *conv_args, "conv_s")
    h2_s, paged_ts, win_ts, proj_s = _layer0_back(xs, c_act_s[:, :ts].reshape(ms, D_MODEL), p, ms, "s")
    new_paged_t = paged_ts[0].reshape(PAGED_ROWS, bs, ts).transpose(1, 0, 2)
    new_win_t = win_ts[0].reshape(2 * HEADS_LANES, bs, ts).transpose(1, 0, 2)
    cache_t = _to_rows_t(cache_kv)
    page_ids = page_table.reshape(-1)
    cmp_s = _compress(cache_t, page_ids, lambda pid: (pid, 0, 0), pe_t, w1t, w2pad,
                      _row_tile(page_ids.shape[0], 32), "compress_s").reshape(bs, pos0 // BLOCK, -1)
    wp = state_win_kv.shape[1]
    win_all_t = jnp.concatenate([_to_rows_t(state_win_kv), new_win_t], axis=2)
    keep_s = min(WINDOW, pos0 + ts)
    win_s = _from_rows_t(win_all_t[:, :, wp + ts - keep_s:], 2)
    pad_last = lambda a, n: jnp.pad(a, ((0, 0), (0, 0), (0, n - a.shape[2])))
    q_s = jnp.pad(proj_s.reshape(bs, ts, -1), ((0, 0), (0, SROWS - ts), (0, 0)))
    part, sel = _sample_attention_a(rel_bias, q_s, cmp_s, pad_last(win_all_t, -(-(wp + ts) // LANES) * LANES),
                                    pos0=pos0, ts=ts, wp=wp)
    knew_t = pad_last(new_paged_t[:, 2 * HEADS_LANES:], PAGE)
    o_s = _sample_attention_b(rel_bias, page_ids, q_s, sel, part, knew_t, cache_t, pos0=pos0, ts=ts, n_pg=8)
    y_s = _layer1_back(o_s[:, :ts].reshape(ms, D_MODEL), h2_s, p, "s").reshape(bs, ts, D_MODEL)
    kv_s_out = _from_rows_t(new_paged_t, 4)
    return (y_p, y_s, kv_p, win_p, conv_p, kv_s_out, win_s, conv_s)
```

```python
import functools
import math

import numpy as np
import jax
import jax.numpy as jnp
from jax import lax
from jax.experimental import pallas as pl
from jax.experimental.pallas import tpu as pltpu

F32 = jnp.float32
BF16 = jnp.bfloat16

D_MODEL = 1024
N_HEADS = 16
HEAD_DIM = 64
N_KV_HEADS = 4
GROUP = N_HEADS // N_KV_HEADS
HEADS_LANES = N_KV_HEADS * HEAD_DIM
PAGED_ROWS = 4 * HEADS_LANES
BLOCK = 64
TOP_N = 16
WINDOW = 512
CONV_WIDTH = 31
CONV_HALO = 32
CMP_PITCH = BLOCK + 8
N_BUCKETS = 32
MAX_EXACT = 16
MAX_DISTANCE = 128
N_EXPERTS = 8
GROUPED_MOE_MIN_TOKENS = 1024
FORCE_SCORE = 1e4
EPS = 1e-6
NEG = -1e30
MASK_BIG = 2.0 ** 100
TINY = 1e-30
LANES = 128
PAGE = 128
VMEM_LIMIT_BYTES = 56 * 1024 * 1024


def _bucket_thresholds():
    d = np.arange(0, 4 * MAX_DISTANCE)
    nf = np.maximum(d, 1).astype(np.float32)
    large = MAX_EXACT + (np.log(nf / np.float32(MAX_EXACT)) / np.float32(math.log(MAX_DISTANCE / MAX_EXACT))
                         * np.float32(N_BUCKETS - MAX_EXACT)).astype(np.int32)
    b = np.where(d < MAX_EXACT, d, np.minimum(large, N_BUCKETS - 1))
    assert np.all(np.diff(b) >= 0)
    return [int(np.min(d[b >= k])) for k in range(N_BUCKETS)]


BUCKET_THR = _bucket_thresholds()
FAR_DIST = BUCKET_THR[-1]


def _cparams(sem):
    return pltpu.CompilerParams(dimension_semantics=sem, vmem_limit_bytes=VMEM_LIMIT_BYTES)


def _sigmoid(x):
    return 1.0 / (1.0 + jnp.exp(-x))


def _silu(x):
    return x * _sigmoid(x)


def _rmsnorm(x, g):
    return x * lax.rsqrt(jnp.mean(x * x, axis=-1, keepdims=True) + EPS) * g


def _dot(a, b):
    return jnp.dot(a, b, preferred_element_type=F32)


def _dot_nt(a, b):
    return lax.dot_general(a, b, (((1,), (1,)), ((), ())), preferred_element_type=F32)


def _rel_bias(dist, rb_ref, head):
    val = jnp.full(dist.shape, rb_ref[0, head], F32)
    for b in range(1, N_BUCKETS):
        val = jnp.where(dist >= BUCKET_THR[b], rb_ref[b, head], val)
    return val


def _rel_bias_heads(dist, rb_ref, heads):
    ge = [dist >= BUCKET_THR[b] for b in range(1, N_BUCKETS)]
    out = []
    for hd in heads:
        val = jnp.full(dist.shape, rb_ref[0, hd], F32)
        for b in range(1, N_BUCKETS):
            val = jnp.where(ge[b - 1], rb_ref[b, hd], val)
        out.append(val)
    return out


def _softmax_parts(s, mask):
    s = jnp.where(mask, s, NEG)
    m = jnp.max(s, axis=-1, keepdims=True)
    p = jnp.where(mask, jnp.exp(s - m), 0.0)
    return p, jnp.maximum(jnp.sum(p, axis=-1, keepdims=True), TINY)


def _online_update(carry, s, vt):
    m, l, acc = carry
    m_new = jnp.maximum(m, jnp.max(s, axis=-1, keepdims=True))
    alpha = jnp.exp(m - m_new)
    p = jnp.exp(s - m_new)
    return m_new, alpha * l + jnp.sum(p, axis=-1, keepdims=True), alpha * acc + _dot_nt(p.astype(BF16), vt)


def _mm_kernel(*refs, n_w, has_norm, has_bias, act, has_res, use_scratch):
    it = iter(refs)
    x_ref = next(it)
    g_ref = next(it) if has_norm else None
    w_refs = [next(it) for _ in range(n_w)]
    b_refs = [next(it) for _ in range(n_w)] if has_bias else []
    r_ref = next(it) if has_res else None
    o_ref = next(it)
    if use_scratch:
        xb_ref = next(it)

        @pl.when(pl.program_id(1) == 0)
        def _():
            x = x_ref[...].astype(F32)
            if has_norm:
                x = _rmsnorm(x, g_ref[...])
            xb_ref[...] = x.astype(BF16)

        xb = xb_ref[...]
    else:
        xb = x_ref[...]
    ys = [_dot(xb, w[...].astype(BF16)) for w in w_refs]
    if has_bias:
        ys = [y + b[...] for y, b in zip(ys, b_refs)]
    if act == "glu":
        y = ys[0] * _sigmoid(ys[1])
    elif act == "swiglu":
        y = _silu(ys[0]) * ys[1]
    else:
        y = ys[0]
    if has_res:
        y = y + r_ref[...]
    o_ref[...] = y.astype(o_ref.dtype)


def _mm(x, ws, n_out, *, tm, tn, name, norm_g=None, biases=None, act=None, residual=None, out_dtype=F32):
    m, k = x.shape
    assert m % tm == 0 and n_out % tn == 0
    use_scratch = norm_g is not None or x.dtype != BF16
    args, specs = [x], [pl.BlockSpec((tm, k), lambda i, j: (i, 0))]
    if norm_g is not None:
        args.append(norm_g.reshape(1, k))
        specs.append(pl.BlockSpec((1, k), lambda i, j: (0, 0)))
    for w, off in ws:
        assert off % tn == 0
        args.append(w)
        specs.append(pl.BlockSpec((k, tn), functools.partial(lambda i, j, o: (0, j + o), o=off // tn)))
    if biases is not None:
        for (bias, off) in biases:
            args.append(bias.reshape(1, -1))
            specs.append(pl.BlockSpec((1, tn), functools.partial(lambda i, j, o: (0, j + o), o=off // tn)))
    if residual is not None:
        args.append(residual)
        specs.append(pl.BlockSpec((tm, tn), lambda i, j: (i, j)))
    kern = functools.partial(_mm_kernel, n_w=len(ws), has_norm=norm_g is not None, has_bias=biases is not None,
                             act=act, has_res=residual is not None, use_scratch=use_scratch)
    return pl.pallas_call(
        kern,
        grid=(m // tm, n_out // tn),
        in_specs=specs,
        out_specs=pl.BlockSpec((tm, tn), lambda i, j: (i, j)),
        out_shape=jax.ShapeDtypeStruct((m, n_out), out_dtype),
        scratch_shapes=[pltpu.VMEM((tm, k), BF16)] if use_scratch else [],
        compiler_params=_cparams(("parallel", "arbitrary")),
        name=name,
    )(*args)


def _mm_t_kernel(x_ref, g_ref, wt_ref, o_ref, xb_ref):
    @pl.when(pl.program_id(1) == 0)
    def _():
        xb_ref[...] = _rmsnorm(x_ref[...], g_ref[...]).astype(BF16)

    o_ref[0] = _dot_nt(wt_ref[...].astype(BF16), xb_ref[...])


def _mm_t(x, wt, norm_g, *, seq, tm, tn, name):
    m, k = x.shape
    n = wt.shape[0]
    assert m % seq == 0 and seq % tm == 0 and n % tn == 0
    per = seq // tm
    return pl.pallas_call(
        _mm_t_kernel,
        grid=(m // tm, n // tn),
        in_specs=[pl.BlockSpec((tm, k), lambda i, j: (i, 0)),
                  pl.BlockSpec((1, k), lambda i, j: (0, 0)),
                  pl.BlockSpec((tn, k), lambda i, j: (j, 0))],
        out_specs=pl.BlockSpec((1, tn, tm), lambda i, j: (i // per, j, i % per)),
        out_shape=jax.ShapeDtypeStruct((m // seq, n, seq), F32),
        scratch_shapes=[pltpu.VMEM((tm, k), BF16)],
        compiler_params=_cparams(("parallel", "arbitrary")),
        name=name,
    )(x, norm_g.reshape(1, k), wt)


def _conv_kernel(main_ref, tail_ref, wdw_ref, bdw_ref, lng_ref, lnb_ref, o_ref, win_ref, c_ref, *, tt, rc):
    for c in range(D_MODEL // LANES):
        sl = slice(c * LANES, (c + 1) * LANES)
        win_ref[c, 0:tt, :] = main_ref[0, :, sl]
        win_ref[c, tt:tt + CONV_HALO, :] = tail_ref[0, :, sl]

    def row_chunk(r, carry):
        r0 = pl.multiple_of(r * rc, rc)
        for c in range(D_MODEL // LANES):
            sl = slice(c * LANES, (c + 1) * LANES)
            acc = jnp.zeros((rc, LANES), F32)
            for k in range(CONV_WIDTH):
                acc = acc + win_ref[c, pl.ds(r0 + (CONV_HALO - CONV_WIDTH + 1) + k, rc), :] * wdw_ref[k:k + 1, sl]
            c_ref[pl.ds(r0, rc), sl] = acc + bdw_ref[:, sl]
        return carry

    lax.fori_loop(0, tt // rc, row_chunk, 0)
    c = c_ref[...]
    mu = jnp.mean(c, axis=-1, keepdims=True)
    xc = c - mu
    y = xc * lax.rsqrt(jnp.mean(xc * xc, axis=-1, keepdims=True) + EPS)
    y = y * lng_ref[...] + lnb_ref[...]
    o_ref[0] = _silu(y).astype(o_ref.dtype)


def _conv_ln_swish(main, tail, main_map, tail_map, n_t, tt, rc, wdw, bdw, lng, lnb, name):
    bsz = main.shape[0]
    vec = pl.BlockSpec((1, D_MODEL), lambda b, i: (0, 0))
    return pl.pallas_call(
        functools.partial(_conv_kernel, tt=tt, rc=rc),
        grid=(bsz, n_t),
        in_specs=[pl.BlockSpec((1, tt, D_MODEL), main_map),
                  pl.BlockSpec((1, CONV_HALO, D_MODEL), tail_map),
                  pl.BlockSpec((CONV_WIDTH, D_MODEL), lambda b, i: (0, 0)), vec, vec, vec],
        out_specs=pl.BlockSpec((1, tt, D_MODEL), lambda b, i: (b, i, 0)),
        out_shape=jax.ShapeDtypeStruct((bsz, n_t * tt, D_MODEL), BF16),
        scratch_shapes=[pltpu.VMEM((D_MODEL // LANES, tt + CONV_HALO, LANES), F32), pltpu.VMEM((tt, D_MODEL), F32)],
        compiler_params=_cparams(("parallel", "parallel")),
        name=name,
    )(main, tail, wdw, bdw.reshape(1, -1), lng.reshape(1, -1), lnb.reshape(1, -1))


def _compress_kernel(pt_ref, *refs, n_pages):
    x_refs = refs[:n_pages]
    pe_ref, w1t_ref, w2_ref, o_ref, xs_ref = refs[n_pages:]
    hidden = w1t_ref.shape[3]
    rows = N_KV_HEADS * n_pages
    for j in range(n_pages):
        x = x_refs[j][0] + pe_ref[...]
        for s in range(2):
            for h in range(N_KV_HEADS):
                r0 = (s * N_KV_HEADS + h) * HEAD_DIM
                xs_ref[s, pl.ds((h * n_pages + j) * CMP_PITCH, HEAD_DIM), :] = x[r0:r0 + HEAD_DIM, :]
    zeros = jnp.zeros((BLOCK, hidden), F32)
    for s in range(2):
        def dim_body(d, acc, s=s):
            lhs = xs_ref[s, pl.ds(d, rows, stride=CMP_PITCH), :].astype(BF16)
            w = w1t_ref[s, d]
            w2blk = jnp.concatenate([jnp.concatenate([w, zeros], axis=1), jnp.concatenate([zeros, w], axis=1)],
                                    axis=0).astype(BF16)
            return acc + _dot(lhs, w2blk)

        hid = lax.fori_loop(0, HEAD_DIM, dim_body, jnp.zeros((rows, 2 * hidden), F32), unroll=8)
        hid = _silu(hid).astype(BF16)
        for blk in range(2):
            out = jnp.zeros((n_pages, HEADS_LANES), F32)
            for h in range(N_KV_HEADS):
                out = out + _dot(hid[h * n_pages:(h + 1) * n_pages, blk * hidden:(blk + 1) * hidden],
                                 w2_ref[s, h].astype(BF16))
            c0 = blk * 2 * HEADS_LANES + s * HEADS_LANES
            o_ref[:, c0:c0 + HEADS_LANES] = out


def _compress(src, page_ids, page_index, pe_t, w1t, w2pad, n_pages, name):
    n = page_ids.shape[0]
    assert n % n_pages == 0 and n_pages % 8 == 0
    x_specs = [pl.BlockSpec((1, 2 * HEADS_LANES, PAGE),
                            functools.partial(lambda i, pt, j: page_index(pt[i * n_pages + j]), j=j))
               for j in range(n_pages)]
    grid_spec = pltpu.PrefetchScalarGridSpec(
        num_scalar_prefetch=1,
        grid=(n // n_pages,),
        in_specs=x_specs + [
            pl.BlockSpec((2 * HEADS_LANES, PAGE), lambda i, pt: (0, 0)),
            pl.BlockSpec(w1t.shape, lambda i, pt: (0, 0, 0, 0)),
            pl.BlockSpec(w2pad.shape, lambda i, pt: (0, 0, 0, 0)),
        ],
        out_specs=pl.BlockSpec((n_pages, 4 * HEADS_LANES), lambda i, pt: (i, 0)),
        scratch_shapes=[pltpu.VMEM((2, N_KV_HEADS * n_pages * CMP_PITCH, LANES), F32)],
    )
    out = pl.pallas_call(
        functools.partial(_compress_kernel, n_pages=n_pages),
        grid_spec=grid_spec,
        out_shape=jax.ShapeDtypeStruct((n, 4 * HEADS_LANES), F32),
        compiler_params=_cparams(("parallel",)),
        name=name,
    )(page_ids, *([src] * n_pages), pe_t, w1t, w2pad)
    return out.reshape(2 * n, 2 * HEADS_LANES)


def _head_gate(sig, lane, g, branch, k):
    col = (g * 3 + branch) * N_KV_HEADS + k
    return jnp.sum(jnp.where(lane == col, sig, 0.0), axis=-1, keepdims=True)


def _pattn_kernel(rb_ref, q_ref, kc_ref, vc_ref, ks_ref, vs_ref, kw_ref, vw_ref, o_ref, strips_ref, *scratch,
                  tq, nc):
    qx_ref, p_ref, mv_ref, acc_ref, s_ref, mb_ref = (scratch[n * GROUP:(n + 1) * GROUP] for n in range(6))
    b, i, k = pl.program_id(0), pl.program_id(1), pl.program_id(2)
    rc = lax.broadcasted_iota(jnp.int32, (tq, tq), 0) - lax.broadcasted_iota(jnp.int32, (tq, tq), 1)
    n_back = WINDOW // tq

    @pl.when((b == 0) & (i == 0) & (k == 0))
    def _():
        def head_body(hh, carry):
            head = (hh & 3) * GROUP + (hh >> 2)
            far = rb_ref[N_BUCKETS - 1, head]
            strips_ref[hh * 4] = jnp.where(rc >= 0, _rel_bias(rc, rb_ref, head), -MASK_BIG)
            strips_ref[hh * 4 + 1] = _rel_bias(rc + tq, rb_ref, head)
            strips_ref[hh * 4 + 2] = jnp.full((tq, tq), far, F32)
            strips_ref[hh * 4 + 3] = jnp.where(rc + n_back * tq <= WINDOW, far, -MASK_BIG)
            return carry

        lax.fori_loop(0, N_HEADS, head_body, 0)

    @pl.when(k == 0)
    def _():
        o_ref[...] = jnp.zeros(o_ref.shape, o_ref.dtype)

    lane = lax.broadcasted_iota(jnp.int32, (1, HEADS_LANES), 1)
    head_mask = (lane // HEAD_DIM) == k
    glane = lax.broadcasted_iota(jnp.int32, (1, LANES), 1)
    sig = _sigmoid(q_ref[0, :, D_MODEL:D_MODEL + LANES])
    kc = kc_ref[0].astype(BF16)
    vc = vc_ref[0].astype(BF16)
    eye = (rc == 0).astype(BF16)
    qs = [jnp.where(head_mask, q_ref[0, :, g * HEADS_LANES:(g + 1) * HEADS_LANES] * (HEAD_DIM ** -0.5), 0.0)
          .astype(BF16) for g in range(GROUP)]

    qpos_t = i * tq + lax.broadcasted_iota(jnp.int32, (1, tq), 1)
    blk_t = lax.broadcasted_iota(jnp.int32, (nc, 1), 0)
    cmp_end_t = blk_t * BLOCK + (BLOCK - 1)
    mask_c = cmp_end_t <= qpos_t
    bias_c = _rel_bias_heads(qpos_t - cmp_end_t, rb_ref, [k * GROUP + g for g in range(GROUP)])
    imp = jnp.zeros((nc, tq), F32)
    for g in range(GROUP):
        s = jnp.where(mask_c, _dot_nt(kc, qs[g]) + bias_c[g], NEG)
        m = jnp.max(s, axis=0, keepdims=True)
        p = jnp.where(mask_c, jnp.exp(s - m), 0.0)
        p = p / jnp.maximum(jnp.sum(p, axis=0, keepdims=True), TINY)
        imp = imp + p
        p_rows = _dot_nt(eye, p.astype(BF16)).astype(BF16)
        sl = slice(g * HEADS_LANES, (g + 1) * HEADS_LANES)
        o_ref[0, :, sl] = o_ref[0, :, sl] + jnp.where(
            head_mask, _head_gate(sig, glane, g, 0, k) * _dot(p_rows, vc), 0.0)

    cur_t = qpos_t // BLOCK
    forced = (blk_t == cur_t) | (blk_t == cur_t - 1) | (blk_t == 0)
    score = jnp.where(blk_t > cur_t, NEG, jnp.where(forced, FORCE_SCORE, imp))
    sel_t = jnp.zeros((nc, tq), F32)
    for j in range(nc):
        row = score[j:j + 1, :]
        beats = (score > row) | ((score == row) & (blk_t < j))
        rank = jnp.sum(beats.astype(F32), axis=0, keepdims=True)
        sel_t = jnp.where(blk_t == j, (rank < TOP_N).astype(F32), sel_t)
    sel = _dot_nt(eye, sel_t.astype(BF16)).astype(BF16)

    kx = (k + 1) % N_KV_HEADS
    lsum_lane = lane == kx * HEAD_DIM
    place = (lax.broadcasted_iota(jnp.int32, (nc, HEADS_LANES), 1)
             == kx * HEAD_DIM + lax.broadcasted_iota(jnp.int32, (nc, HEADS_LANES), 0)).astype(BF16)
    unsel_lanes = _dot(1.0 - sel, place).astype(BF16)
    row = lax.broadcasted_iota(jnp.int32, (HEADS_LANES, 1), 0)
    slot_blk = row - kx * HEAD_DIM
    in_slot = (slot_blk >= 0) & (slot_blk < nc)
    key_blk = lax.broadcasted_iota(jnp.int32, (1, tq), 1) // BLOCK

    def tile(ref, j):
        return ref[0, :, pl.ds(pl.multiple_of(j * tq, tq), tq)]

    def k_sel(j):
        mask_rows = jnp.where(slot_blk == j * (tq // BLOCK) + key_blk, -MASK_BIG, 0.0)
        return jnp.where(in_slot, mask_rows, tile(ks_ref, j)).astype(BF16)

    def k_win(j):
        return tile(kw_ref, j).astype(BF16)

    def scores(g, kt, sidx):
        return _dot(qx_ref[g][...], kt) + strips_ref[(g * N_KV_HEADS + k) * 4 + sidx]

    def run_branch(queries, n_visits, k_tile, v_ref, strip_index, branch):
        for g in range(GROUP):
            qx_ref[g][...] = queries[g]
            mv_ref[g][...] = jnp.full((tq, tq // 2), NEG, F32)
            acc_ref[g][...] = jnp.zeros((tq, HEADS_LANES), F32)

        def max_visit(jj, carry):
            kt, sidx = k_tile(i - jj), strip_index(jj)
            for g in range(GROUP):
                s = scores(g, kt, sidx)
                s_ref[g][jj] = s
                mv_ref[g][...] = jnp.maximum(mv_ref[g][...], jnp.maximum(s[:, :tq // 2], s[:, tq // 2:]))
            return carry

        lax.fori_loop(0, n_visits, max_visit, 0)
        for g in range(GROUP):
            mb_ref[g][...] = jnp.broadcast_to(jnp.max(mv_ref[g][...], axis=-1, keepdims=True), (tq, tq))

        def acc_visit(jj, carry):
            vt = jnp.where(row == kx * HEAD_DIM, 1.0, tile(v_ref, i - jj)).astype(BF16)
            for g in range(GROUP):
                p_ref[g][...] = jnp.exp(s_ref[g][jj] - mb_ref[g][...]).astype(BF16)
            for g in range(GROUP):
                acc_ref[g][...] += _dot_nt(p_ref[g][...], vt)
            return carry

        lax.fori_loop(0, n_visits, acc_visit, 0)
        for g in range(GROUP):
            acc = acc_ref[g][...]
            l = jnp.sum(jnp.where(lsum_lane, acc, 0.0), axis=-1, keepdims=True)
            scale = _head_gate(sig, glane, g, branch, k) / jnp.maximum(l, TINY)
            sl = slice(g * HEADS_LANES, (g + 1) * HEADS_LANES)
            o_ref[0, :, sl] = o_ref[0, :, sl] + jnp.where(head_mask, scale * acc, 0.0)

    run_branch([qs[g] + unsel_lanes for g in range(GROUP)], i + 1, k_sel, vs_ref,
               lambda jj: jnp.minimum(jj, 2), 1)
    run_branch(qs, jnp.minimum(i, n_back) + 1, k_win, vw_ref,
               lambda jj: jnp.where(jj == n_back, 3, jnp.minimum(jj, 2)), 2)


def _prompt_attention(rel_bias, proj, cmp_kv, paged_t, win_t, *, tq):
    bsz, t, _ = proj.shape
    nc = cmp_kv.shape[1]
    assert t % tq == 0 and tq % BLOCK == 0 and WINDOW == 2 * tq and nc * BLOCK == t
    assert tq + 1 >= FAR_DIST

    def rows(idx, n):
        return pl.BlockSpec((1, n, HEADS_LANES), functools.partial(lambda b, i, k, c: (b, 0, c), c=idx))

    def cols(idx):
        return pl.BlockSpec((1, HEADS_LANES, t), functools.partial(lambda b, i, k, c: (b, c, 0), c=idx))

    return pl.pallas_call(
        functools.partial(_pattn_kernel, tq=tq, nc=nc),
        grid=(bsz, t // tq, N_KV_HEADS),
        in_specs=[pl.BlockSpec(memory_space=pltpu.SMEM),
                  pl.BlockSpec((1, tq, proj.shape[2]), lambda b, i, k: (b, i, 0)),
                  rows(0, nc), rows(1, nc), cols(2), cols(3), cols(0), cols(1)],
        out_specs=pl.BlockSpec((1, tq, D_MODEL), lambda b, i, k: (b, i, 0)),
        out_shape=jax.ShapeDtypeStruct((bsz, t, D_MODEL), F32),
        scratch_shapes=[pltpu.VMEM((4 * N_HEADS, tq, tq), F32)]
        + [pltpu.VMEM(shape, dtype) for shape, dtype in (
            ((tq, HEADS_LANES), BF16), ((tq, tq), BF16), ((tq, tq // 2), F32), ((tq, HEADS_LANES), F32),
            ((t // tq, tq, tq), F32), ((tq, tq), F32)) for _ in range(GROUP)],
        compiler_params=_cparams(("arbitrary", "arbitrary", "arbitrary")),
        name="prompt_attention",
    )(rel_bias, proj, cmp_kv, cmp_kv, paged_t, paged_t, win_t, win_t)


SROWS = 8
ALL_ROWS = N_HEADS * SROWS


def _all_head_queries(q_ref):
    lane = lax.broadcasted_iota(jnp.int32, (1, HEADS_LANES), 1)
    parts = []
    for k in range(N_KV_HEADS):
        for g in range(GROUP):
            qg = q_ref[0, :, g * HEADS_LANES:(g + 1) * HEADS_LANES] * (HEAD_DIM ** -0.5)
            parts.append(jnp.where((lane // HEAD_DIM) == k, qg, 0.0))
    return jnp.concatenate(parts, axis=0).astype(BF16)


def _all_head_bias(dist, rb_ref):
    return jnp.concatenate(_rel_bias_heads(dist, rb_ref, list(range(N_HEADS))), axis=0)


def _sattn_a_kernel(rb_ref, q_ref, cmp_ref, win_ref, part_ref, sel_ref, *, pos0, ts, wp, nc):
    lane = lax.broadcasted_iota(jnp.int32, (1, HEADS_LANES), 1)
    sig = _sigmoid(q_ref[0, :, D_MODEL:D_MODEL + LANES])
    q = _all_head_queries(q_ref)
    qpos = pos0 + lax.broadcasted_iota(jnp.int32, (SROWS, 1), 0)
    tile_rows = lambda a: jnp.concatenate([a] * N_HEADS, axis=0)

    blk = lax.broadcasted_iota(jnp.int32, (1, nc), 1)
    cmp_end = blk * BLOCK + (BLOCK - 1)
    s = _dot_nt(q, cmp_ref[0, :, 0:HEADS_LANES].astype(BF16)) + _all_head_bias(qpos - cmp_end, rb_ref)
    p, l = _softmax_parts(s, tile_rows(cmp_end <= qpos))
    p = p / l
    o_c = _dot(p.astype(BF16), cmp_ref[0, :, HEADS_LANES:2 * HEADS_LANES].astype(BF16))

    imp = jnp.concatenate(
        [sum(p[(k * GROUP + g) * SROWS:(k * GROUP + g + 1) * SROWS] for g in range(GROUP))
         for k in range(N_KV_HEADS)], axis=0)
    cur = jnp.concatenate([qpos // BLOCK] * N_KV_HEADS, axis=0)
    forced = (blk == cur) | (blk == cur - 1) | (blk == 0)
    score = jnp.where(blk > cur, NEG, jnp.where(forced, FORCE_SCORE, imp))
    sel = jnp.zeros(score.shape, F32)
    for _ in range(TOP_N - 1):
        m = jnp.max(score, axis=-1, keepdims=True)
        idx = jnp.min(jnp.where(score == m, blk, nc), axis=-1, keepdims=True)
        hit = blk == idx
        sel = jnp.where(hit, 1.0, sel)
        score = jnp.where(hit, -jnp.inf, score)
    for k in range(N_KV_HEADS):
        sel_ref[0, :, k * nc:(k + 1) * nc] = sel[k * SROWS:(k + 1) * SROWS]

    wk = win_ref.shape[2]
    j = lax.broadcasted_iota(jnp.int32, (1, wk), 1)
    kw_pos = pos0 - wp + j
    dist = qpos - kw_pos
    mask = (dist >= 0) & (dist <= WINDOW) & (kw_pos >= 0) & (j < wp + ts)
    s = _dot(q, win_ref[0, 0:HEADS_LANES, :].astype(BF16)) + _all_head_bias(dist, rb_ref)
    p, l = _softmax_parts(s, tile_rows(mask))
    o_w = _dot_nt((p / l).astype(BF16), win_ref[0, HEADS_LANES:2 * HEADS_LANES, :].astype(BF16))

    for g in range(GROUP):
        chunk = jnp.zeros((SROWS, HEADS_LANES), F32)
        for k in range(N_KV_HEADS):
            r = slice((k * GROUP + g) * SROWS, (k * GROUP + g + 1) * SROWS)
            c_cmp, c_win = (g * 3 + 0) * N_KV_HEADS + k, (g * 3 + 2) * N_KV_HEADS + k
            comb = sig[:, c_cmp:c_cmp + 1] * o_c[r] + sig[:, c_win:c_win + 1] * o_w[r]
            chunk = chunk + jnp.where((lane // HEAD_DIM) == k, comb, 0.0)
        part_ref[0, :, g * HEADS_LANES:(g + 1) * HEADS_LANES] = chunk


def _sample_attention_a(rel_bias, q, cmp_kv, win_t, *, pos0, ts, wp):
    bsz = q.shape[0]
    nc = cmp_kv.shape[1]
    assert pos0 % BLOCK == 0 and ts < BLOCK and pos0 // BLOCK == nc and ts <= SROWS and TOP_N >= 3
    whole = lambda arr: pl.BlockSpec((1,) + arr.shape[1:], lambda b: (b, 0, 0))
    return pl.pallas_call(
        functools.partial(_sattn_a_kernel, pos0=pos0, ts=ts, wp=wp, nc=nc),
        grid=(bsz,),
        in_specs=[pl.BlockSpec(memory_space=pltpu.SMEM), whole(q), whole(cmp_kv), whole(win_t)],
        out_specs=[pl.BlockSpec((1, SROWS, D_MODEL), lambda b: (b, 0, 0)),
                   pl.BlockSpec((1, SROWS, N_KV_HEADS * nc), lambda b: (b, 0, 0))],
        out_shape=[jax.ShapeDtypeStruct((bsz, SROWS, D_MODEL), F32),
                   jax.ShapeDtypeStruct((bsz, SROWS, N_KV_HEADS * nc), F32)],
        compiler_params=_cparams(("parallel",)),
        name="sample_attention_a",
    )(rel_bias, q, cmp_kv, win_t)


def _sattn_b_kernel(pt_ref, rb_ref, q_ref, sel_ref, part_ref, knew_ref, *refs, pos0, ts, nc, n_pg, n_seq):
    k_refs = [refs[s * n_pg:(s + 1) * n_pg] for s in range(n_seq)]
    v_refs = [refs[(n_seq + s) * n_pg:(n_seq + s + 1) * n_pg] for s in range(n_seq)]
    o_ref, expand_ref = refs[2 * n_seq * n_pg:2 * n_seq * n_pg + 2]
    state = refs[2 * n_seq * n_pg + 2:]
    p_idx = pl.program_id(1)
    keys = n_pg * PAGE
    blocks = keys // BLOCK
    n_pat = LANES // blocks
    qpos = pos0 + lax.broadcasted_iota(jnp.int32, (SROWS, 1), 0)
    col = lax.broadcasted_iota(jnp.int32, (1, keys), 1)

    @pl.when((pl.program_id(0) == 0) & (p_idx == 0))
    def _():
        b_row = lax.broadcasted_iota(jnp.int32, (LANES, 1), 0)
        for o in range(n_pat):
            expand_ref[o] = jnp.where(b_row == o * blocks + col // BLOCK, -MASK_BIG, 0.0).astype(BF16)

    def near_bias(dist):
        return jnp.concatenate(
            [b - rb_ref[N_BUCKETS - 1, h] for h, b in enumerate(_rel_bias_heads(dist, rb_ref, list(range(N_HEADS))))],
            axis=0)

    page0 = p_idx * n_pg
    last = p_idx == pl.num_programs(1) - 1
    window = pl.multiple_of((2 * page0) // LANES * LANES, LANES)
    expand = expand_ref[((2 * page0) % LANES) // blocks]
    seq_state = [state[5 * s:5 * s + 5] for s in range(n_seq)]

    def update(s, scores, vt):
        _, _, m_ref, l_ref, acc_ref = seq_state[s]
        m, l, acc = _online_update((m_ref[...], l_ref[...], acc_ref[...]), scores, vt)
        m_ref[...], l_ref[...], acc_ref[...] = m, l, acc

    def page_scores(s):
        qall_ref, unsel_ref = seq_state[s][:2]
        kt = jnp.concatenate([r[0] for r in k_refs[s]], axis=1).astype(BF16)
        vt = jnp.concatenate([r[0] for r in v_refs[s]], axis=1).astype(BF16)
        return _dot(qall_ref[...], kt) + _dot(unsel_ref[:, pl.ds(window, LANES)].astype(BF16), expand), vt

    @pl.when(p_idx == 0)
    def _():
        for s in range(n_seq):
            qall_ref, unsel_ref, m_ref, l_ref, acc_ref = seq_state[s]
            qall_ref[...] = _all_head_queries(q_ref.at[pl.ds(s, 1)])
            for k in range(N_KV_HEADS):
                for g in range(GROUP):
                    r = slice((k * GROUP + g) * SROWS, (k * GROUP + g + 1) * SROWS)
                    unsel_ref[r, :] = 1.0 - sel_ref[s, :, k * nc:(k + 1) * nc]
            m_ref[...] = jnp.full(m_ref.shape, NEG, F32)
            l_ref[...] = jnp.zeros(l_ref.shape, F32)
            acc_ref[...] = jnp.zeros(acc_ref.shape, F32)

    for s in range(n_seq):
        base, vt = page_scores(s)
        update(s, base + jnp.where(last, -MASK_BIG, 0.0), vt)

    @pl.when(last)
    def _():
        for s in range(n_seq):
            qall_ref, _, _, l_ref, acc_ref = seq_state[s]
            base, vt = page_scores(s)
            update(s, base + near_bias(qpos - (page0 * PAGE + col)), vt)
            ncol = lax.broadcasted_iota(jnp.int32, (1, PAGE), 1)
            dist = qpos - (pos0 + ncol)
            add = near_bias(dist) + jnp.concatenate(
                [jnp.where((dist >= 0) & (ncol < ts), 0.0, -MASK_BIG)] * N_HEADS, axis=0)
            update(s, _dot(qall_ref[...], knew_ref[s, 0:HEADS_LANES, :].astype(BF16)) + add,
                   knew_ref[s, HEADS_LANES:2 * HEADS_LANES, :].astype(BF16))
            o_sel = acc_ref[...] / jnp.maximum(l_ref[...], TINY)
            sig = _sigmoid(q_ref[s, :, D_MODEL:D_MODEL + LANES])
            lane = lax.broadcasted_iota(jnp.int32, (1, HEADS_LANES), 1)
            for g in range(GROUP):
                sl = slice(g * HEADS_LANES, (g + 1) * HEADS_LANES)
                chunk = part_ref[s, :, sl]
                for k in range(N_KV_HEADS):
                    r = slice((k * GROUP + g) * SROWS, (k * GROUP + g + 1) * SROWS)
                    c = (g * 3 + 1) * N_KV_HEADS + k
                    chunk = chunk + jnp.where((lane // HEAD_DIM) == k, sig[:, c:c + 1] * o_sel[r], 0.0)
                o_ref[s, :, sl] = chunk


def _sample_attention_b(rel_bias, page_ids, q, sel, part, knew_t, cache_t, *, pos0, ts, n_pg):
    bsz = q.shape[0]
    n_pages = page_ids.shape[0] // bsz
    nc = sel.shape[2] // N_KV_HEADS
    n_seq = 2 if bsz % 2 == 0 else 1
    keys = n_pg * PAGE
    assert n_pages % n_pg == 0 and n_pages * 2 == nc and pos0 == n_pages * PAGE and TOP_N >= 3
    assert nc % LANES == 0 and LANES % (keys // BLOCK) == 0 and keys >= FAR_DIST

    def page_spec(slot, s, j):
        return pl.BlockSpec(
            (1, HEADS_LANES, PAGE),
            functools.partial(lambda b, p, pt, s, j, c: (pt[(b * n_seq + s) * n_pages + p * n_pg + j], c, 0),
                              s=s, j=j, c=slot))

    whole = lambda arr: pl.BlockSpec((n_seq,) + arr.shape[1:], lambda b, p, pt: (b, 0, 0))
    pages = [page_spec(slot, s, j) for slot in (2, 3) for s in range(n_seq) for j in range(n_pg)]
    per_seq = [pltpu.VMEM((ALL_ROWS, HEADS_LANES), BF16), pltpu.VMEM((ALL_ROWS, nc), F32),
               pltpu.VMEM((ALL_ROWS, 1), F32), pltpu.VMEM((ALL_ROWS, 1), F32),
               pltpu.VMEM((ALL_ROWS, HEADS_LANES), F32)]
    grid_spec = pltpu.PrefetchScalarGridSpec(
        num_scalar_prefetch=1,
        grid=(bsz // n_seq, n_pages // n_pg),
        in_specs=[pl.BlockSpec(memory_space=pltpu.SMEM), whole(q), whole(sel), whole(part), whole(knew_t)] + pages,
        out_specs=pl.BlockSpec((n_seq, SROWS, D_MODEL), lambda b, p, pt: (b, 0, 0)),
        scratch_shapes=[pltpu.VMEM((LANES // (keys // BLOCK), LANES, keys), BF16)] + per_seq * n_seq,
    )
    return pl.pallas_call(
        functools.partial(_sattn_b_kernel, pos0=pos0, ts=ts, nc=nc, n_pg=n_pg, n_seq=n_seq),
        grid_spec=grid_spec,
        out_shape=jax.ShapeDtypeStruct((bsz, SROWS, D_MODEL), F32),
        compiler_params=_cparams(("arbitrary", "arbitrary")),
        name="sample_attention_b",
    )(page_ids, rel_bias, q, sel, part, knew_t, *([cache_t] * (2 * n_seq * n_pg)))


def _router_kernel(h_ref, g_ref, wr_ref, xn_ref, comb_ref):
    xn = _rmsnorm(h_ref[...], g_ref[...]).astype(BF16)
    xn_ref[...] = xn
    logits = _dot(xn, wr_ref[...].astype(BF16))
    lane = lax.broadcasted_iota(jnp.int32, logits.shape, 1)
    lg = jnp.where(lane < N_EXPERTS, logits, -jnp.inf)
    m1 = jnp.max(lg, axis=-1, keepdims=True)
    i1 = jnp.min(jnp.where(lg == m1, lane, LANES), axis=-1, keepdims=True)
    lg2 = jnp.where(lane == i1, -jnp.inf, lg)
    m2 = jnp.max(lg2, axis=-1, keepdims=True)
    i2 = jnp.min(jnp.where(lg2 == m2, lane, LANES), axis=-1, keepdims=True)
    e = jnp.exp(m2 - m1)
    comb_ref[...] = jnp.where(lane == i1, 1.0 / (1.0 + e), 0.0) + jnp.where(lane == i2, e / (1.0 + e), 0.0)


def _router(h, g, w_router_pad, tm, name):
    m = h.shape[0]
    return pl.pallas_call(
        _router_kernel,
        grid=(m // tm,),
        in_specs=[pl.BlockSpec((tm, D_MODEL), lambda i: (i, 0)),
                  pl.BlockSpec((1, D_MODEL), lambda i: (0, 0)),
                  pl.BlockSpec((D_MODEL, LANES), lambda i: (0, 0))],
        out_specs=[pl.BlockSpec((tm, D_MODEL), lambda i: (i, 0)), pl.BlockSpec((tm, LANES), lambda i: (i, 0))],
        out_shape=[jax.ShapeDtypeStruct((m, D_MODEL), BF16), jax.ShapeDtypeStruct((m, LANES), F32)],
        compiler_params=_cparams(("parallel",)),
        name=name,
    )(h, g.reshape(1, -1), w_router_pad)


def _moe_kernel(xn_ref, comb_ref, h_ref, wg_ref, wu_ref, wd_ref, gf_ref, y_ref, acc_ref):
    e, f = pl.program_id(1), pl.program_id(2)
    last_e, last_f = pl.num_programs(1) - 1, pl.num_programs(2) - 1

    @pl.when((e == 0) & (f == 0))
    def _():
        y_ref[...] = h_ref[...]

    @pl.when(f == 0)
    def _():
        acc_ref[...] = jnp.zeros(acc_ref.shape, F32)

    x = xn_ref[...]
    hid = _silu(_dot(x, wg_ref[0].astype(BF16))) * _dot(x, wu_ref[0].astype(BF16))
    acc_ref[...] += _dot(hid.astype(BF16), wd_ref[0].astype(BF16))

    @pl.when(f == last_f)
    def _():
        lane = lax.broadcasted_iota(jnp.int32, comb_ref.shape, 1)
        ce = jnp.sum(jnp.where(lane == e, comb_ref[...], 0.0), axis=-1, keepdims=True)
        y_ref[...] += ce * acc_ref[...]

    @pl.when((e == last_e) & (f == last_f))
    def _():
        y_ref[...] = _rmsnorm(y_ref[...], gf_ref[...])


def _moe(xn, comb, h, wg, wu, wd, g_final, tm, tf, name):
    m = h.shape[0]
    n_e, _, d_ff = wg.shape
    assert m % tm == 0 and d_ff % tf == 0
    row = lambda i, e, f: (i, 0)
    return pl.pallas_call(
        _moe_kernel,
        grid=(m // tm, n_e, d_ff // tf),
        in_specs=[pl.BlockSpec((tm, D_MODEL), row), pl.BlockSpec((tm, LANES), row), pl.BlockSpec((tm, D_MODEL), row),
                  pl.BlockSpec((1, D_MODEL, tf), lambda i, e, f: (e, 0, f)),
                  pl.BlockSpec((1, D_MODEL, tf), lambda i, e, f: (e, 0, f)),
                  pl.BlockSpec((1, tf, D_MODEL), lambda i, e, f: (e, f, 0)),
                  pl.BlockSpec((1, D_MODEL), lambda i, e, f: (0, 0))],
        out_specs=pl.BlockSpec((tm, D_MODEL), row),
        out_shape=jax.ShapeDtypeStruct((m, D_MODEL), F32),
        scratch_shapes=[pltpu.VMEM((tm, D_MODEL), F32)],
        compiler_params=_cparams(("parallel", "arbitrary", "arbitrary")),
        name=name,
    )(xn, comb, h, wg, wu, wd, g_final.reshape(1, -1))


def _route_kernel(h_ref, g_ref, wr_ref, xn_ref, idx_ref, w_ref):
    xn = _rmsnorm(h_ref[...], g_ref[...])
    xn_ref[...] = xn
    logits = _dot(xn.astype(BF16), wr_ref[...].astype(BF16))
    lane = lax.broadcasted_iota(jnp.int32, logits.shape, 1)
    lg = jnp.where(lane < N_EXPERTS, logits, -jnp.inf)
    m1 = jnp.max(lg, axis=-1, keepdims=True)
    i1 = jnp.min(jnp.where(lg == m1, lane, LANES), axis=-1, keepdims=True)
    lg2 = jnp.where(lane == i1, -jnp.inf, lg)
    m2 = jnp.max(lg2, axis=-1, keepdims=True)
    i2 = jnp.min(jnp.where(lg2 == m2, lane, LANES), axis=-1, keepdims=True)
    e = jnp.exp(m2 - m1)
    idx_ref[...] = jnp.where(lane == 0, i1, jnp.where(lane == 1, i2, 0))
    w_ref[...] = jnp.where(lane == 0, 1.0 / (1.0 + e), jnp.where(lane == 1, e / (1.0 + e), 0.0))


def _route(h, g, w_router_pad, tm, name):
    m = h.shape[0]
    row = lambda n: pl.BlockSpec((tm, n), lambda i: (i, 0))
    return pl.pallas_call(
        _route_kernel,
        grid=(m // tm,),
        in_specs=[row(D_MODEL), pl.BlockSpec((1, D_MODEL), lambda i: (0, 0)),
                  pl.BlockSpec((D_MODEL, LANES), lambda i: (0, 0))],
        out_specs=[row(D_MODEL), row(LANES), row(LANES)],
        out_shape=[jax.ShapeDtypeStruct((m, D_MODEL), F32), jax.ShapeDtypeStruct((m, LANES), jnp.int32),
                   jax.ShapeDtypeStruct((m, LANES), F32)],
        compiler_params=_cparams(("parallel",)),
        name=name,
    )(h, g.reshape(1, -1), w_router_pad)


ROW_DMA_CHUNK = 1024


def _row_moves_step(items, copies):
    step, last = pl.program_id(0), pl.num_programs(0) - 1
    per_item = len(copies(0))

    def start(r, carry):
        for c in copies(step * items + r):
            c.start()
        return carry

    def drain():
        def wait(r, carry):
            for _ in range(per_item):
                copies(0)[0].wait()
            return carry
        lax.fori_loop(0, items, wait, 0, unroll=8)

    lax.fori_loop(0, items, start, 0, unroll=8)
    pl.when(step > 0)(drain)
    pl.when(step == last)(drain)


def _spread_rows_kernel(row_of_ref, src_ref, dst_ref, zeros_ref, sem):
    @pl.when(pl.program_id(0) == 0)
    def _():
        zeros_ref[...] = jnp.zeros(zeros_ref.shape, zeros_ref.dtype)
        zr = zeros_ref.shape[0]
        n_fill = dst_ref.shape[0] // zr

        def fill(i):
            return pltpu.make_async_copy(zeros_ref, dst_ref.at[pl.ds(i * zr, zr)], sem)

        def fill_start(i, carry):
            fill(i).start()
            return carry

        def fill_wait(i, carry):
            fill(0).wait()
            return carry

        lax.fori_loop(0, n_fill, fill_start, 0)
        lax.fori_loop(0, n_fill, fill_wait, 0)

    _row_moves_step(ROW_DMA_CHUNK, lambda a: [pltpu.make_async_copy(
        src_ref.at[pl.ds(lax.shift_right_logical(a, 1), 1)], dst_ref.at[pl.ds(row_of_ref[a], 1)], sem)])


def _spread_rows(src, row_of, n_dst, fill_rows, name):
    n = row_of.shape[0]
    assert n_dst % fill_rows == 0 and n % ROW_DMA_CHUNK == 0
    any_spec = pl.BlockSpec(memory_space=pl.ANY)
    grid_spec = pltpu.PrefetchScalarGridSpec(
        num_scalar_prefetch=1, grid=(n // ROW_DMA_CHUNK,), in_specs=[any_spec], out_specs=any_spec,
        scratch_shapes=[pltpu.VMEM((fill_rows, src.shape[1]), src.dtype), pltpu.SemaphoreType.DMA(())])
    return pl.pallas_call(
        _spread_rows_kernel,
        grid_spec=grid_spec,
        out_shape=jax.ShapeDtypeStruct((n_dst, src.shape[1]), src.dtype),
        compiler_params=_cparams(("arbitrary",)),
        name=name,
    )(row_of, src)


def _collect_pairs_kernel(row_of_ref, src_ref, dst_ref, sem, *, width):
    _row_moves_step(ROW_DMA_CHUNK // 2, lambda t: [pltpu.make_async_copy(
        src_ref.at[pl.ds(row_of_ref[2 * t + half], 1)],
        dst_ref.at[pl.ds(t, 1), pl.ds(half * width, width)], sem) for half in range(2)])


def _collect_pairs(src, row_of, name):
    n = row_of.shape[0]
    width = src.shape[1]
    any_spec = pl.BlockSpec(memory_space=pl.ANY)
    assert n % ROW_DMA_CHUNK == 0
    grid_spec = pltpu.PrefetchScalarGridSpec(
        num_scalar_prefetch=1, grid=(n // ROW_DMA_CHUNK,), in_specs=[any_spec], out_specs=any_spec,
        scratch_shapes=[pltpu.SemaphoreType.DMA(())])
    return pl.pallas_call(
        functools.partial(_collect_pairs_kernel, width=width),
        grid_spec=grid_spec,
        out_shape=jax.ShapeDtypeStruct((n // 2, 2 * width), src.dtype),
        compiler_params=_cparams(("arbitrary",)),
        name=name,
    )(row_of, src)


def _experts_kernel(te_ref, tv_ref, x_ref, wg_ref, wu_ref, wd_ref, o_ref, *, tf):
    i = pl.program_id(0)

    @pl.when(tv_ref[i] == 0)
    def _():
        o_ref[...] = jnp.zeros(o_ref.shape, F32)

    @pl.when(tv_ref[i] != 0)
    def _():
        x = x_ref[...].astype(BF16)
        acc = jnp.zeros(o_ref.shape, F32)
        for f in range(wg_ref.shape[2] // tf):
            sl = slice(f * tf, (f + 1) * tf)
            hid = _silu(_dot(x, wg_ref[0, :, sl].astype(BF16))) * _dot(x, wu_ref[0, :, sl].astype(BF16))
            acc = acc + _dot(hid.astype(BF16), wd_ref[0, sl, :].astype(BF16))
        o_ref[...] = acc


def _experts(xg, tile_expert, tile_valid, wg, wu, wd, rows, tf, name):
    n = xg.shape[0]
    n_e, _, d_ff = wg.shape
    assert n % rows == 0 and d_ff % tf == 0
    once = pl.Buffered(1)
    grid_spec = pltpu.PrefetchScalarGridSpec(
        num_scalar_prefetch=2,
        grid=(n // rows,),
        in_specs=[pl.BlockSpec((rows, D_MODEL), lambda i, te, tv: (i, 0)),
                  pl.BlockSpec((1, D_MODEL, d_ff), lambda i, te, tv: (te[i], 0, 0), pipeline_mode=once),
                  pl.BlockSpec((1, D_MODEL, d_ff), lambda i, te, tv: (te[i], 0, 0), pipeline_mode=once),
                  pl.BlockSpec((1, d_ff, D_MODEL), lambda i, te, tv: (te[i], 0, 0), pipeline_mode=once)],
        out_specs=pl.BlockSpec((rows, D_MODEL), lambda i, te, tv: (i, 0)),
    )
    return pl.pallas_call(
        functools.partial(_experts_kernel, tf=tf),
        grid_spec=grid_spec,
        out_shape=jax.ShapeDtypeStruct((n, D_MODEL), F32),
        compiler_params=_cparams(("arbitrary",)),
        name=name,
    )(tile_expert, tile_valid, xg, wg, wu, wd)


def _moe_combine_kernel(h_ref, y2_ref, w_ref, gf_ref, o_ref):
    w = w_ref[...]
    y = h_ref[...] + w[:, 0:1] * y2_ref[:, 0:D_MODEL] + w[:, 1:2] * y2_ref[:, D_MODEL:2 * D_MODEL]
    o_ref[...] = _rmsnorm(y, gf_ref[...])


def _moe_combine(h, y2, w, g_final, tm, name):
    m = h.shape[0]
    row = lambda n: pl.BlockSpec((tm, n), lambda i: (i, 0))
    return pl.pallas_call(
        _moe_combine_kernel,
        grid=(m // tm,),
        in_specs=[row(D_MODEL), row(2 * D_MODEL), row(LANES), pl.BlockSpec((1, D_MODEL), lambda i: (0, 0))],
        out_specs=row(D_MODEL),
        out_shape=jax.ShapeDtypeStruct((m, D_MODEL), F32),
        compiler_params=_cparams(("parallel",)),
        name=name,
    )(h, y2, w, g_final.reshape(1, -1))


def _moe_grouped(h, p, tag, rows=512):
    m = h.shape[0]
    tm = _row_tile(m, 1024)
    xn, idx, w = _route(h, p["norm_ffn"][1], p["w_router_pad"], tm, "route_" + tag)
    n_e = p["moe_w_gate"].shape[1]
    expert = idx[:, :2].reshape(-1)
    onehot = (expert[:, None] == jnp.arange(n_e, dtype=jnp.int32)[None, :]).astype(jnp.int32)
    pos = jnp.sum((jnp.cumsum(onehot, axis=0) - 1) * onehot, axis=1)
    counts = jnp.sum(onehot, axis=0)
    padded = (counts + rows - 1) // rows * rows
    ends = jnp.cumsum(padded)
    row_of = (ends - padded)[expert] + pos
    n_tiles = (2 * m + n_e * (rows - 1)) // rows + 1
    tile_start = jnp.arange(n_tiles, dtype=jnp.int32) * rows
    tile_expert = jnp.minimum(jnp.sum((tile_start[:, None] >= ends[None, :]).astype(jnp.int32), axis=1), n_e - 1)
    tile_valid = (tile_start < ends[-1]).astype(jnp.int32)
    xg = _spread_rows(xn, row_of, n_tiles * rows, rows, "moe_spread_" + tag)
    yg = _experts(xg, tile_expert, tile_valid, p["moe_w_gate"][0], p["moe_w_up"][0], p["moe_w_down"][0], rows, 256,
                  "moe_experts_" + tag)
    y2 = _collect_pairs(yg, row_of, "moe_collect_" + tag)
    return _moe_combine(h, y2, w, p["norm_final"], tm, "moe_combine_" + tag)


def _compress_params(cmp_pos, cmp_w1, cmp_w2):
    pe = jnp.transpose(cmp_pos, (0, 2, 1))[:, None, :, :]
    pe = jnp.broadcast_to(pe, (2, N_KV_HEADS, HEAD_DIM, BLOCK))
    pe_t = jnp.concatenate([pe, pe], axis=-1).reshape(2 * HEADS_LANES, PAGE)
    hidden = cmp_w1.shape[2]
    w1t = cmp_w1.reshape(2, BLOCK, HEAD_DIM, hidden).transpose(0, 2, 1, 3)
    w2pad = jnp.zeros((2, N_KV_HEADS, hidden, HEADS_LANES), F32)
    for h in range(N_KV_HEADS):
        w2pad = w2pad.at[:, h, :, h * HEAD_DIM:(h + 1) * HEAD_DIM].set(cmp_w2)
    return pe_t, w1t, w2pad


def _attn_weights(w_in, w_out):
    hd = N_HEADS * HEAD_DIM
    wq = w_in[:, :hd].reshape(D_MODEL, N_KV_HEADS, GROUP, HEAD_DIM).transpose(0, 2, 1, 3).reshape(D_MODEL, hd)
    wg = w_in[:, hd:].reshape(D_MODEL, N_KV_HEADS, GROUP, 3).transpose(0, 2, 3, 1).reshape(D_MODEL, 3 * N_HEADS)
    w_in_pad = jnp.concatenate([wq, wg, jnp.zeros((D_MODEL, LANES - 3 * N_HEADS), F32)], axis=1)
    w_out_perm = w_out.reshape(N_KV_HEADS, GROUP, HEAD_DIM, D_MODEL).transpose(1, 0, 2, 3).reshape(hd, D_MODEL)
    return w_in_pad, w_out_perm


def _row_tile(m, cap):
    t = cap
    while m % t:
        t //= 2
    return t


def _from_rows_t(a_t, n_slots):
    bsz, _, t = a_t.shape
    return a_t.reshape(bsz, n_slots, N_KV_HEADS, HEAD_DIM, t).transpose(0, 4, 1, 2, 3)


def _to_rows_t(a):
    bsz, t, n_slots = a.shape[:3]
    return a.transpose(0, 2, 3, 4, 1).reshape(bsz, n_slots * HEADS_LANES, t)


def _layer0_front(x, p, tag):
    m = x.shape[0]
    w1, b1 = p["conv_w_pw1"][0], p["conv_b_pw1"][0]
    return _mm(x, [(w1, 0), (w1, D_MODEL)], D_MODEL, tm=_row_tile(m, 1024), tn=512, norm_g=p["norm_mix"][0],
               biases=[(b1, 0), (b1, D_MODEL)], act="glu", name="pw1_glu_" + tag)


def _layer0_back(x, c_act, p, seq, tag):
    m = x.shape[0]
    tm = _row_tile(m, 1024)
    h1 = _mm(c_act, [(p["conv_w_pw2"][0], 0)], D_MODEL, tm=tm, tn=512, biases=[(p["conv_b_pw2"][0], 0)], residual=x,
             name="pw2_" + tag)
    d_ff = p["ffn_w_gate"].shape[2]
    a = _mm(h1, [(p["ffn_w_gate"][0], 0), (p["ffn_w_up"][0], 0)], d_ff, tm=tm, tn=256, norm_g=p["norm_ffn"][0],
            act="swiglu", out_dtype=BF16, name="ffn_up_" + tag)
    h2 = _mm(a, [(p["ffn_w_down"][0], 0)], D_MODEL, tm=_row_tile(m, 512), tn=512, residual=h1, name="ffn_down_" + tag)
    tmt = _row_tile(seq, 1024)
    paged_t = _mm_t(h2, p["w_kv_t"][:PAGED_ROWS], p["norm_kv"], seq=seq, tm=tmt, tn=512, name="kv_paged_" + tag)
    win_t = _mm_t(h2, p["w_kv_t"][PAGED_ROWS:], p["norm_kv"], seq=seq, tm=tmt, tn=512, name="kv_win_" + tag)
    proj = _mm(h2, [(p["w_in_pad"], 0)], p["w_in_pad"].shape[1], tm=tm, tn=384, norm_g=p["norm_mix"][1],
               name="q_proj_" + tag)
    return h2, paged_t, win_t, proj


def _layer1_back(o, h2, p, tag):
    m = h2.shape[0]
    tm = _row_tile(m, 1024)
    h3 = _mm(o, [(p["w_out_perm"], 0)], D_MODEL, tm=tm, tn=512, residual=h2, name="attn_out_" + tag)
    if m >= GROUPED_MOE_MIN_TOKENS:
        return _moe_grouped(h3, p, tag)
    xn, comb = _router(h3, p["norm_ffn"][1], p["w_router_pad"], tm, "router_" + tag)
    return _moe(xn, comb, h3, p["moe_w_gate"][0], p["moe_w_up"][0], p["moe_w_down"][0], p["norm_final"], tm, 256,
                "moe_" + tag)


def kernel(x_prompt, x_sample, cache_kv, state_win_kv, state_conv, page_table, norm_mix, norm_ffn, norm_kv, norm_final, conv_w_pw1, conv_b_pw1, conv_w_dw, conv_b_dw, conv_ln_g, conv_ln_b, conv_w_pw2, conv_b_pw2, w_kv, cmp_pos, cmp_w1, cmp_w2, nsa_w_in, nsa_w_out, rel_bias, ffn_w_gate, ffn_w_up, ffn_w_down, moe_w_router, moe_w_gate, moe_w_up, moe_w_down):
    p = dict(norm_mix=norm_mix, norm_ffn=norm_ffn, norm_kv=norm_kv, norm_final=norm_final,
             conv_w_pw1=conv_w_pw1, conv_b_pw1=conv_b_pw1, conv_w_pw2=conv_w_pw2, conv_b_pw2=conv_b_pw2,
             ffn_w_gate=ffn_w_gate, ffn_w_up=ffn_w_up, ffn_w_down=ffn_w_down,
             moe_w_gate=moe_w_gate, moe_w_up=moe_w_up, moe_w_down=moe_w_down)
    p["w_kv_t"] = w_kv.T
    p["w_in_pad"], p["w_out_perm"] = _attn_weights(nsa_w_in[0], nsa_w_out[0])
    p["w_router_pad"] = jnp.pad(moe_w_router[0], ((0, 0), (0, LANES - N_EXPERTS)))
    pe_t, w1t, w2pad = _compress_params(cmp_pos, cmp_w1, cmp_w2)
    conv_args = (conv_w_dw[0], conv_b_dw[0], conv_ln_g[0], conv_ln_b[0])
    n_state = CONV_WIDTH - 1

    bp, tp, _ = x_prompt.shape
    mp = bp * tp
    xp = x_prompt.reshape(mp, D_MODEL)
    glu = _layer0_front(xp, p, "p").reshape(bp, tp, D_MODEL)
    gpad = jnp.concatenate([jnp.zeros((bp, CONV_HALO, D_MODEL), F32), glu], axis=1)
    tt = 256
    c_act = _conv_ln_swish(gpad, gpad, lambda b, i: (b, i, 0), lambda b, i: (b, (i + 1) * (tt // CONV_HALO), 0),
                           tp // tt, tt, 32, *conv_args, "conv_p")
    h2, paged_t, win_t, proj = _layer0_back(xp, c_act.reshape(mp, D_MODEL), p, tp, "p")
    ppb = tp // PAGE
    cmp_p = _compress(paged_t, jnp.arange(bp * ppb, dtype=jnp.int32), lambda pid: (pid // ppb, 0, pid % ppb),
                      pe_t, w1t, w2pad, _row_tile(bp * ppb, 32), "compress_p").reshape(bp, tp // BLOCK, -1)
    o = _prompt_attention(rel_bias, proj.reshape(bp, tp, -1), cmp_p, paged_t, win_t, tq=256)
    y_p = _layer1_back(o.reshape(mp, D_MODEL), h2, p, "p").reshape(bp, tp, D_MODEL)
    kv_p = _from_rows_t(paged_t, 4)
    keep_p = min(WINDOW, tp)
    win_p = _from_rows_t(win_t[:, :, tp - keep_p:], 2)
    conv_p = glu[:, tp - n_state:][None]

    bs, ts, _ = x_sample.shape
    ms = bs * ts
    n_pages = page_table.shape[1]
    assert cache_kv.shape[1] == PAGE
    pos0 = n_pages * PAGE
    xs = x_sample.reshape(ms, D_MODEL)
    glu_s = _layer0_front(xs, p, "s").reshape(bs, ts, D_MODEL)
    full_s = jnp.concatenate([state_conv[0], glu_s], axis=1)
    conv_s = full_s[:, full_s.shape[1] - n_state:][None]
    gpad_s = jnp.concatenate([jnp.zeros((bs, CONV_HALO - n_state, D_MODEL), F32), full_s,
                              jnp.zeros((bs, SROWS - ts, D_MODEL), F32)], axis=1)
    once = lambda b, i: (b, 0, 0)
    c_act_s = _conv_ln_swish(gpad_s[:, :SROWS], gpad_s[:, SROWS:], once, once, 1, SROWS, SROWS, *conv_args, "conv_s")
    h2_s, paged_ts, win_ts, proj_s = _layer0_back(xs, c_act_s[:, :ts].reshape(ms, D_MODEL), p, ms, "s")
    new_paged_t = paged_ts[0].reshape(PAGED_ROWS, bs, ts).transpose(1, 0, 2)
    new_win_t = win_ts[0].reshape(2 * HEADS_LANES, bs, ts).transpose(1, 0, 2)
    cache_t = _to_rows_t(cache_kv)
    page_ids = page_table.reshape(-1)
    cmp_s = _compress(cache_t, page_ids, lambda pid: (pid, 0, 0), pe_t, w1t, w2pad,
                      _row_tile(page_ids.shape[0], 32), "compress_s").reshape(bs, pos0 // BLOCK, -1)
    wp = state_win_kv.shape[1]
    win_all_t = jnp.concatenate([_to_rows_t(state_win_kv), new_win_t], axis=2)
    keep_s = min(WINDOW, pos0 + ts)
    win_s = _from_rows_t(win_all_t[:, :, wp + ts - keep_s:], 2)
    pad_last = lambda a, n: jnp.pad(a, ((0, 0), (0, 0), (0, n - a.shape[2])))
    q_s = jnp.pad(proj_s.reshape(bs, ts, -1), ((0, 0), (0, SROWS - ts), (0, 0)))
    part, sel = _sample_attention_a(rel_bias, q_s, cmp_s, pad_last(win_all_t, -(-(wp + ts) // LANES) * LANES),
                                    pos0=pos0, ts=ts, wp=wp)
    knew_t = pad_last(new_paged_t[:, 2 * HEADS_LANES:], PAGE)
    o_s = _sample_attention_b(rel_bias, page_ids, q_s, sel, part, knew_t, cache_t, pos0=pos0, ts=ts, n_pg=8)
    y_s = _layer1_back(o_s[:, :ts].reshape(ms, D_MODEL), h2_s, p, "s").reshape(bs, ts, D_MODEL)
    kv_s_out = _from_rows_t(new_paged_t, 4)
    return (y_p, y_s, kv_p, win_p, conv_p, kv_s_out, win_s, conv_s)
```

```python
import functools
import math

import numpy as np
import jax
import jax.numpy as jnp
from jax import lax
from jax.experimental import pallas as pl
from jax.experimental.pallas import tpu as pltpu

F32 = jnp.float32
BF16 = jnp.bfloat16

D_MODEL = 1024
N_HEADS = 16
HEAD_DIM = 64
N_KV_HEADS = 4
GROUP = N_HEADS // N_KV_HEADS
HEADS_LANES = N_KV_HEADS * HEAD_DIM
PAGED_ROWS = 4 * HEADS_LANES
BLOCK = 64
TOP_N = 16
WINDOW = 512
CONV_WIDTH = 31
CONV_HALO = 32
CMP_PITCH = BLOCK + 8
N_BUCKETS = 32
MAX_EXACT = 16
MAX_DISTANCE = 128
N_EXPERTS = 8
GROUPED_MOE_MIN_TOKENS = 1024
FORCE_SCORE = 1e4
EPS = 1e-6
NEG = -1e30
MASK_BIG = 2.0 ** 100
TINY = 1e-30
LANES = 128
PAGE = 128
VMEM_LIMIT_BYTES = 56 * 1024 * 1024


def _bucket_thresholds():
    d = np.arange(0, 4 * MAX_DISTANCE)
    nf = np.maximum(d, 1).astype(np.float32)
    large = MAX_EXACT + (np.log(nf / np.float32(MAX_EXACT)) / np.float32(math.log(MAX_DISTANCE / MAX_EXACT))
                         * np.float32(N_BUCKETS - MAX_EXACT)).astype(np.int32)
    b = np.where(d < MAX_EXACT, d, np.minimum(large, N_BUCKETS - 1))
    assert np.all(np.diff(b) >= 0)
    return [int(np.min(d[b >= k])) for k in range(N_BUCKETS)]


BUCKET_THR = _bucket_thresholds()
FAR_DIST = BUCKET_THR[-1]


def _cparams(sem):
    return pltpu.CompilerParams(dimension_semantics=sem, vmem_limit_bytes=VMEM_LIMIT_BYTES)


def _sigmoid(x):
    return 1.0 / (1.0 + jnp.exp(-x))


def _silu(x):
    return x * _sigmoid(x)


def _rmsnorm(x, g):
    return x * lax.rsqrt(jnp.mean(x * x, axis=-1, keepdims=True) + EPS) * g


def _dot(a, b):
    return jnp.dot(a, b, preferred_element_type=F32)


def _dot_nt(a, b):
    return lax.dot_general(a, b, (((1,), (1,)), ((), ())), preferred_element_type=F32)


def _rel_bias(dist, rb_ref, head):
    val = jnp.full(dist.shape, rb_ref[0, head], F32)
    for b in range(1, N_BUCKETS):
        val = jnp.where(dist >= BUCKET_THR[b], rb_ref[b, head], val)
    return val


def _rel_bias_heads(dist, rb_ref, heads):
    ge = [dist >= BUCKET_THR[b] for b in range(1, N_BUCKETS)]
    out = []
    for hd in heads:
        val = jnp.full(dist.shape, rb_ref[0, hd], F32)
        for b in range(1, N_BUCKETS):
            val = jnp.where(ge[b - 1], rb_ref[b, hd], val)
        out.append(val)
    return out


def _softmax_parts(s, mask):
    s = jnp.where(mask, s, NEG)
    m = jnp.max(s, axis=-1, keepdims=True)
    p = jnp.where(mask, jnp.exp(s - m), 0.0)
    return p, jnp.maximum(jnp.sum(p, axis=-1, keepdims=True), TINY)


def _online_update(carry, s, vt):
    m, l, acc = carry
    m_new = jnp.maximum(m, jnp.max(s, axis=-1, keepdims=True))
    alpha = jnp.exp(m - m_new)
    p = jnp.exp(s - m_new)
    return m_new, alpha * l + jnp.sum(p, axis=-1, keepdims=True), alpha * acc + _dot_nt(p.astype(BF16), vt)


def _mm_kernel(*refs, n_w, has_norm, has_bias, act, has_res, use_scratch):
    it = iter(refs)
    x_ref = next(it)
    g_ref = next(it) if has_norm else None
    w_refs = [next(it) for _ in range(n_w)]
    b_refs = [next(it) for _ in range(n_w)] if has_bias else []
    r_ref = next(it) if has_res else None
    o_ref = next(it)
    if use_scratch:
        xb_ref = next(it)

        @pl.when(pl.program_id(1) == 0)
        def _():
            x = x_ref[...].astype(F32)
            if has_norm:
                x = _rmsnorm(x, g_ref[...])
            xb_ref[...] = x.astype(BF16)

        xb = xb_ref[...]
    else:
        xb = x_ref[...]
    ys = [_dot(xb, w[...].astype(BF16)) for w in w_refs]
    if has_bias:
        ys = [y + b[...] for y, b in zip(ys, b_refs)]
    if act == "glu":
        y = ys[0] * _sigmoid(ys[1])
    elif act == "swiglu":
        y = _silu(ys[0]) * ys[1]
    else:
        y = ys[0]
    if has_res:
        y = y + r_ref[...]
    o_ref[...] = y.astype(o_ref.dtype)


def _mm(x, ws, n_out, *, tm, tn, name, norm_g=None, biases=None, act=None, residual=None, out_dtype=F32):
    m, k = x.shape
    assert m % tm == 0 and n_out % tn == 0
    use_scratch = norm_g is not None or x.dtype != BF16
    args, specs = [x], [pl.BlockSpec((tm, k), lambda i, j: (i, 0))]
    if norm_g is not None:
        args.append(norm_g.reshape(1, k))
        specs.append(pl.BlockSpec((1, k), lambda i, j: (0, 0)))
    for w, off in ws:
        assert off % tn == 0
        args.append(w)
        specs.append(pl.BlockSpec((k, tn), functools.partial(lambda i, j, o: (0, j + o), o=off // tn)))
    if biases is not None:
        for (bias, off) in biases:
            args.append(bias.reshape(1, -1))
            specs.append(pl.BlockSpec((1, tn), functools.partial(lambda i, j, o: (0, j + o), o=off // tn)))
    if residual is not None:
        args.append(residual)
        specs.append(pl.BlockSpec((tm, tn), lambda i, j: (i, j)))
    kern = functools.partial(_mm_kernel, n_w=len(ws), has_norm=norm_g is not None, has_bias=biases is not None,
                             act=act, has_res=residual is not None, use_scratch=use_scratch)
    return pl.pallas_call(
        kern,
        grid=(m // tm, n_out // tn),
        in_specs=specs,
        out_specs=pl.BlockSpec((tm, tn), lambda i, j: (i, j)),
        out_shape=jax.ShapeDtypeStruct((m, n_out), out_dtype),
        scratch_shapes=[pltpu.VMEM((tm, k), BF16)] if use_scratch else [],
        compiler_params=_cparams(("parallel", "arbitrary")),
        name=name,
    )(*args)


def _mm_t_kernel(x_ref, g_ref, wt_ref, o_ref, xb_ref):
    @pl.when(pl.program_id(1) == 0)
    def _():
        xb_ref[...] = _rmsnorm(x_ref[...], g_ref[...]).astype(BF16)

    o_ref[0] = _dot_nt(wt_ref[...].astype(BF16), xb_ref[...])


def _mm_t(x, wt, norm_g, *, seq, tm, tn, name):
    m, k = x.shape
    n = wt.shape[0]
    assert m % seq == 0 and seq % tm == 0 and n % tn == 0
    per = seq // tm
    return pl.pallas_call(
        _mm_t_kernel,
        grid=(m // tm, n // tn),
        in_specs=[pl.BlockSpec((tm, k), lambda i, j: (i, 0)),
                  pl.BlockSpec((1, k), lambda i, j: (0, 0)),
                  pl.BlockSpec((tn, k), lambda i, j: (j, 0))],
        out_specs=pl.BlockSpec((1, tn, tm), lambda i, j: (i // per, j, i % per)),
        out_shape=jax.ShapeDtypeStruct((m // seq, n, seq), F32),
        scratch_shapes=[pltpu.VMEM((tm, k), BF16)],
        compiler_params=_cparams(("parallel", "arbitrary")),
        name=name,
    )(x, norm_g.reshape(1, k), wt)


def _conv_kernel(main_ref, tail_ref, wdw_ref, bdw_ref, lng_ref, lnb_ref, o_ref, win_ref, c_ref, *, tt, rc):
    for c in range(D_MODEL // LANES):
        sl = slice(c * LANES, (c + 1) * LANES)
        win_ref[c, 0:tt, :] = main_ref[0, :, sl]
        win_ref[c, tt:tt + CONV_HALO, :] = tail_ref[0, :, sl]

    def row_chunk(r, carry):
        r0 = pl.multiple_of(r * rc, rc)
        for c in range(D_MODEL // LANES):
            sl = slice(c * LANES, (c + 1) * LANES)
            acc = jnp.zeros((rc, LANES), F32)
            for k in range(CONV_WIDTH):
                acc = acc + win_ref[c, pl.ds(r0 + (CONV_HALO - CONV_WIDTH + 1) + k, rc), :] * wdw_ref[k:k + 1, sl]
            c_ref[pl.ds(r0, rc), sl] = acc + bdw_ref[:, sl]
        return carry

    lax.fori_loop(0, tt // rc, row_chunk, 0)
    c = c_ref[...]
    mu = jnp.mean(c, axis=-1, keepdims=True)
    xc = c - mu
    y = xc * lax.rsqrt(jnp.mean(xc * xc, axis=-1, keepdims=True) + EPS)
    y = y * lng_ref[...] + lnb_ref[...]
    o_ref[0] = _silu(y).astype(o_ref.dtype)


def _conv_ln_swish(main, tail, main_map, tail_map, n_t, tt, rc, wdw, bdw, lng, lnb, name):
    bsz = main.shape[0]
    vec = pl.BlockSpec((1, D_MODEL), lambda b, i: (0, 0))
    return pl.pallas_call(
        functools.partial(_conv_kernel, tt=tt, rc=rc),
        grid=(bsz, n_t),
        in_specs=[pl.BlockSpec((1, tt, D_MODEL), main_map),
                  pl.BlockSpec((1, CONV_HALO, D_MODEL), tail_map),
                  pl.BlockSpec((CONV_WIDTH, D_MODEL), lambda b, i: (0, 0)), vec, vec, vec],
        out_specs=pl.BlockSpec((1, tt, D_MODEL), lambda b, i: (b, i, 0)),
        out_shape=jax.ShapeDtypeStruct((bsz, n_t * tt, D_MODEL), BF16),
        scratch_shapes=[pltpu.VMEM((D_MODEL // LANES, tt + CONV_HALO, LANES), F32), pltpu.VMEM((tt, D_MODEL), F32)],
        compiler_params=_cparams(("parallel", "parallel")),
        name=name,
    )(main, tail, wdw, bdw.reshape(1, -1), lng.reshape(1, -1), lnb.reshape(1, -1))


def _compress_kernel(pt_ref, *refs, n_pages):
    x_refs = refs[:n_pages]
    pe_ref, w1_ref, w2_ref, o_ref, xs_ref = refs[n_pages:]
    hidden = w1_ref.shape[2] // 2
    rows = N_KV_HEADS * n_pages
    for j in range(n_pages):
        x = x_refs[j][0] + pe_ref[...]
        for s in range(2):
            for h in range(N_KV_HEADS):
                r0 = (s * N_KV_HEADS + h) * HEAD_DIM
                xs_ref[s, pl.ds((h * n_pages + j) * CMP_PITCH, HEAD_DIM), :] = x[r0:r0 + HEAD_DIM, :]
    for s in range(2):
        lhs = jnp.concatenate([xs_ref[s, pl.ds(d, rows, stride=CMP_PITCH), :].astype(BF16) for d in range(HEAD_DIM)],
                              axis=1)
        hid = _silu(_dot(lhs, w1_ref[s])).astype(BF16)
        for blk in range(2):
            out = jnp.zeros((n_pages, HEADS_LANES), F32)
            for h in range(N_KV_HEADS):
                out = out + _dot(hid[h * n_pages:(h + 1) * n_pages, blk * hidden:(blk + 1) * hidden],
                                 w2_ref[s, h].astype(BF16))
            c0 = blk * 2 * HEADS_LANES + s * HEADS_LANES
            o_ref[:, c0:c0 + HEADS_LANES] = out


def _compress(src, page_ids, page_index, pe_t, w1t, w2pad, n_pages, name):
    n = page_ids.shape[0]
    assert n % n_pages == 0 and n_pages % 8 == 0
    x_specs = [pl.BlockSpec((1, 2 * HEADS_LANES, PAGE),
                            functools.partial(lambda i, pt, j: page_index(pt[i * n_pages + j]), j=j))
               for j in range(n_pages)]
    grid_spec = pltpu.PrefetchScalarGridSpec(
        num_scalar_prefetch=1,
        grid=(n // n_pages,),
        in_specs=x_specs + [
            pl.BlockSpec((2 * HEADS_LANES, PAGE), lambda i, pt: (0, 0)),
            pl.BlockSpec(w1t.shape, lambda i, pt: (0, 0, 0), pipeline_mode=pl.Buffered(1)),
            pl.BlockSpec(w2pad.shape, lambda i, pt: (0, 0, 0, 0)),
        ],
        out_specs=pl.BlockSpec((n_pages, 4 * HEADS_LANES), lambda i, pt: (i, 0)),
        scratch_shapes=[pltpu.VMEM((2, N_KV_HEADS * n_pages * CMP_PITCH, LANES), F32)],
    )
    out = pl.pallas_call(
        functools.partial(_compress_kernel, n_pages=n_pages),
        grid_spec=grid_spec,
        out_shape=jax.ShapeDtypeStruct((n, 4 * HEADS_LANES), F32),
        compiler_params=_cparams(("parallel",)),
        name=name,
    )(page_ids, *([src] * n_pages), pe_t, w1t, w2pad)
    return out.reshape(2 * n, 2 * HEADS_LANES)


def _head_gate(sig, lane, g, branch, k):
    col = (g * 3 + branch) * N_KV_HEADS + k
    return jnp.sum(jnp.where(lane == col, sig, 0.0), axis=-1, keepdims=True)


def _pattn_kernel(rb_ref, q_ref, kc_ref, vc_ref, ks_ref, vs_ref, kw_ref, vw_ref, o_ref, strips_ref, *scratch,
                  tq, nc):
    qx_ref, p_ref, mv_ref, acc_ref, s_ref, mb_ref = (scratch[n * GROUP:(n + 1) * GROUP] for n in range(6))
    b, i, k = pl.program_id(0), pl.program_id(1), pl.program_id(2)
    rc = lax.broadcasted_iota(jnp.int32, (tq, tq), 0) - lax.broadcasted_iota(jnp.int32, (tq, tq), 1)
    n_back = WINDOW // tq

    @pl.when((b == 0) & (i == 0) & (k == 0))
    def _():
        def head_body(hh, carry):
            head = (hh & 3) * GROUP + (hh >> 2)
            far = rb_ref[N_BUCKETS - 1, head]
            strips_ref[hh * 4] = jnp.where(rc >= 0, _rel_bias(rc, rb_ref, head), -MASK_BIG)
            strips_ref[hh * 4 + 1] = _rel_bias(rc + tq, rb_ref, head)
            strips_ref[hh * 4 + 2] = jnp.full((tq, tq), far, F32)
            strips_ref[hh * 4 + 3] = jnp.where(rc + n_back * tq <= WINDOW, far, -MASK_BIG)
            return carry

        lax.fori_loop(0, N_HEADS, head_body, 0)

    @pl.when(k == 0)
    def _():
        o_ref[...] = jnp.zeros(o_ref.shape, o_ref.dtype)

    lane = lax.broadcasted_iota(jnp.int32, (1, HEADS_LANES), 1)
    head_mask = (lane // HEAD_DIM) == k
    glane = lax.broadcasted_iota(jnp.int32, (1, LANES), 1)
    sig = _sigmoid(q_ref[0, :, D_MODEL:D_MODEL + LANES])
    kc = kc_ref[0].astype(BF16)
    vc = vc_ref[0].astype(BF16)
    eye = (rc == 0).astype(BF16)
    qs = [jnp.where(head_mask, q_ref[0, :, g * HEADS_LANES:(g + 1) * HEADS_LANES] * (HEAD_DIM ** -0.5), 0.0)
          .astype(BF16) for g in range(GROUP)]

    qpos_t = i * tq + lax.broadcasted_iota(jnp.int32, (1, tq), 1)
    blk_t = lax.broadcasted_iota(jnp.int32, (nc, 1), 0)
    cmp_end_t = blk_t * BLOCK + (BLOCK - 1)
    mask_c = cmp_end_t <= qpos_t
    bias_c = _rel_bias_heads(qpos_t - cmp_end_t, rb_ref, [k * GROUP + g for g in range(GROUP)])
    imp = jnp.zeros((nc, tq), F32)
    for g in range(GROUP):
        s = jnp.where(mask_c, _dot_nt(kc, qs[g]) + bias_c[g], NEG)
        m = jnp.max(s, axis=0, keepdims=True)
        p = jnp.where(mask_c, jnp.exp(s - m), 0.0)
        p = p / jnp.maximum(jnp.sum(p, axis=0, keepdims=True), TINY)
        imp = imp + p
        p_rows = _dot_nt(eye, p.astype(BF16)).astype(BF16)
        sl = slice(g * HEADS_LANES, (g + 1) * HEADS_LANES)
        o_ref[0, :, sl] = o_ref[0, :, sl] + jnp.where(
            head_mask, _head_gate(sig, glane, g, 0, k) * _dot(p_rows, vc), 0.0)

    cur_t = qpos_t // BLOCK
    forced = (blk_t == cur_t) | (blk_t == cur_t - 1) | (blk_t == 0)
    score = jnp.where(blk_t > cur_t, NEG, jnp.where(forced, FORCE_SCORE, imp))
    sel_t = jnp.zeros((nc, tq), F32)
    for j in range(nc):
        row = score[j:j + 1, :]
        beats = (score > row) | ((score == row) & (blk_t < j))
        rank = jnp.sum(beats.astype(F32), axis=0, keepdims=True)
        sel_t = jnp.where(blk_t == j, (rank < TOP_N).astype(F32), sel_t)
    sel = _dot_nt(eye, sel_t.astype(BF16)).astype(BF16)

    kx = (k + 1) % N_KV_HEADS
    lsum_lane = lane == kx * HEAD_DIM
    place = (lax.broadcasted_iota(jnp.int32, (nc, HEADS_LANES), 1)
             == kx * HEAD_DIM + lax.broadcasted_iota(jnp.int32, (nc, HEADS_LANES), 0)).astype(BF16)
    unsel_lanes = _dot(1.0 - sel, place).astype(BF16)
    row = lax.broadcasted_iota(jnp.int32, (HEADS_LANES, 1), 0)
    slot_blk = row - kx * HEAD_DIM
    in_slot = (slot_blk >= 0) & (slot_blk < nc)
    key_blk = lax.broadcasted_iota(jnp.int32, (1, tq), 1) // BLOCK

    def tile(ref, j):
        return ref[0, :, pl.ds(pl.multiple_of(j * tq, tq), tq)]

    def k_sel(j):
        mask_rows = jnp.where(slot_blk == j * (tq // BLOCK) + key_blk, -MASK_BIG, 0.0)
        return jnp.where(in_slot, mask_rows, tile(ks_ref, j)).astype(BF16)

    def k_win(j):
        return tile(kw_ref, j).astype(BF16)

    def scores(g, kt, sidx):
        return _dot(qx_ref[g][...], kt) + strips_ref[(g * N_KV_HEADS + k) * 4 + sidx]

    def run_branch(queries, n_visits, k_tile, v_ref, strip_index, branch):
        for g in range(GROUP):
            qx_ref[g][...] = queries[g]
            mv_ref[g][...] = jnp.full((tq, tq // 2), NEG, F32)
            acc_ref[g][...] = jnp.zeros((tq, HEADS_LANES), F32)

        def max_visit(jj, carry):
            kt, sidx = k_tile(i - jj), strip_index(jj)
            for g in range(GROUP):
                s = scores(g, kt, sidx)
                s_ref[g][jj] = s
                mv_ref[g][...] = jnp.maximum(mv_ref[g][...], jnp.maximum(s[:, :tq // 2], s[:, tq // 2:]))
            return carry

        lax.fori_loop(0, n_visits, max_visit, 0)
        for g in range(GROUP):
            mb_ref[g][...] = jnp.broadcast_to(jnp.max(mv_ref[g][...], axis=-1, keepdims=True), (tq, tq))

        def acc_visit(jj, carry):
            vt = jnp.where(row == kx * HEAD_DIM, 1.0, tile(v_ref, i - jj)).astype(BF16)
            for g in range(GROUP):
                p_ref[g][...] = jnp.exp(s_ref[g][jj] - mb_ref[g][...]).astype(BF16)
            for g in range(GROUP):
                acc_ref[g][...] += _dot_nt(p_ref[g][...], vt)
            return carry

        lax.fori_loop(0, n_visits, acc_visit, 0)
        for g in range(GROUP):
            acc = acc_ref[g][...]
            l = jnp.sum(jnp.where(lsum_lane, acc, 0.0), axis=-1, keepdims=True)
            scale = _head_gate(sig, glane, g, branch, k) / jnp.maximum(l, TINY)
            sl = slice(g * HEADS_LANES, (g + 1) * HEADS_LANES)
            o_ref[0, :, sl] = o_ref[0, :, sl] + jnp.where(head_mask, scale * acc, 0.0)

    run_branch([qs[g] + unsel_lanes for g in range(GROUP)], i + 1, k_sel, vs_ref,
               lambda jj: jnp.minimum(jj, 2), 1)
    run_branch(qs, jnp.minimum(i, n_back) + 1, k_win, vw_ref,
               lambda jj: jnp.where(jj == n_back, 3, jnp.minimum(jj, 2)), 2)


def _prompt_attention(rel_bias, proj, cmp_kv, paged_t, win_t, *, tq):
    bsz, t, _ = proj.shape
    nc = cmp_kv.shape[1]
    assert t % tq == 0 and tq % BLOCK == 0 and WINDOW == 2 * tq and nc * BLOCK == t
    assert tq + 1 >= FAR_DIST

    def rows(idx, n):
        return pl.BlockSpec((1, n, HEADS_LANES), functools.partial(lambda b, i, k, c: (b, 0, c), c=idx))

    def cols(idx):
        return pl.BlockSpec((1, HEADS_LANES, t), functools.partial(lambda b, i, k, c: (b, c, 0), c=idx))

    return pl.pallas_call(
        functools.partial(_pattn_kernel, tq=tq, nc=nc),
        grid=(bsz, t // tq, N_KV_HEADS),
        in_specs=[pl.BlockSpec(memory_space=pltpu.SMEM),
                  pl.BlockSpec((1, tq, proj.shape[2]), lambda b, i, k: (b, i, 0)),
                  rows(0, nc), rows(1, nc), cols(2), cols(3), cols(0), cols(1)],
        out_specs=pl.BlockSpec((1, tq, D_MODEL), lambda b, i, k: (b, i, 0)),
        out_shape=jax.ShapeDtypeStruct((bsz, t, D_MODEL), F32),
        scratch_shapes=[pltpu.VMEM((4 * N_HEADS, tq, tq), F32)]
        + [pltpu.VMEM(shape, dtype) for shape, dtype in (
            ((tq, HEADS_LANES), BF16), ((tq, tq), BF16), ((tq, tq // 2), F32), ((tq, HEADS_LANES), F32),
            ((t // tq, tq, tq), F32), ((tq, tq), F32)) for _ in range(GROUP)],
        compiler_params=_cparams(("arbitrary", "arbitrary", "arbitrary")),
        name="prompt_attention",
    )(rel_bias, proj, cmp_kv, cmp_kv, paged_t, paged_t, win_t, win_t)


SROWS = 8
ALL_ROWS = N_HEADS * SROWS


def _all_head_queries(q_ref):
    lane = lax.broadcasted_iota(jnp.int32, (1, HEADS_LANES), 1)
    parts = []
    for k in range(N_KV_HEADS):
        for g in range(GROUP):
            qg = q_ref[0, :, g * HEADS_LANES:(g + 1) * HEADS_LANES] * (HEAD_DIM ** -0.5)
            parts.append(jnp.where((lane // HEAD_DIM) == k, qg, 0.0))
    return jnp.concatenate(parts, axis=0).astype(BF16)


def _all_head_bias(dist, rb_ref):
    return jnp.concatenate(_rel_bias_heads(dist, rb_ref, list(range(N_HEADS))), axis=0)


def _sattn_a_kernel(rb_ref, q_ref, cmp_ref, win_ref, part_ref, sel_ref, *, pos0, ts, wp, nc):
    lane = lax.broadcasted_iota(jnp.int32, (1, HEADS_LANES), 1)
    sig = _sigmoid(q_ref[0, :, D_MODEL:D_MODEL + LANES])
    q = _all_head_queries(q_ref)
    qpos = pos0 + lax.broadcasted_iota(jnp.int32, (SROWS, 1), 0)
    tile_rows = lambda a: jnp.concatenate([a] * N_HEADS, axis=0)

    blk = lax.broadcasted_iota(jnp.int32, (1, nc), 1)
    cmp_end = blk * BLOCK + (BLOCK - 1)
    s = _dot_nt(q, cmp_ref[0, :, 0:HEADS_LANES].astype(BF16)) + _all_head_bias(qpos - cmp_end, rb_ref)
    p, l = _softmax_parts(s, tile_rows(cmp_end <= qpos))
    p = p / l
    o_c = _dot(p.astype(BF16), cmp_ref[0, :, HEADS_LANES:2 * HEADS_LANES].astype(BF16))

    imp = jnp.concatenate(
        [sum(p[(k * GROUP + g) * SROWS:(k * GROUP + g + 1) * SROWS] for g in range(GROUP))
         for k in range(N_KV_HEADS)], axis=0)
    cur = jnp.concatenate([qpos // BLOCK] * N_KV_HEADS, axis=0)
    forced = (blk == cur) | (blk == cur - 1) | (blk == 0)
    score = jnp.where(blk > cur, NEG, jnp.where(forced, FORCE_SCORE, imp))
    sel = jnp.zeros(score.shape, F32)
    for _ in range(TOP_N - 1):
        m = jnp.max(score, axis=-1, keepdims=True)
        idx = jnp.min(jnp.where(score == m, blk, nc), axis=-1, keepdims=True)
        hit = blk == idx
        sel = jnp.where(hit, 1.0, sel)
        score = jnp.where(hit, -jnp.inf, score)
    for k in range(N_KV_HEADS):
        sel_ref[0, :, k * nc:(k + 1) * nc] = sel[k * SROWS:(k + 1) * SROWS]

    wk = win_ref.shape[2]
    j = lax.broadcasted_iota(jnp.int32, (1, wk), 1)
    kw_pos = pos0 - wp + j
    dist = qpos - kw_pos
    mask = (dist >= 0) & (dist <= WINDOW) & (kw_pos >= 0) & (j < wp + ts)
    s = _dot(q, win_ref[0, 0:HEADS_LANES, :].astype(BF16)) + _all_head_bias(dist, rb_ref)
    p, l = _softmax_parts(s, tile_rows(mask))
    o_w = _dot_nt((p / l).astype(BF16), win_ref[0, HEADS_LANES:2 * HEADS_LANES, :].astype(BF16))

    for g in range(GROUP):
        chunk = jnp.zeros((SROWS, HEADS_LANES), F32)
        for k in range(N_KV_HEADS):
            r = slice((k * GROUP + g) * SROWS, (k * GROUP + g + 1) * SROWS)
            c_cmp, c_win = (g * 3 + 0) * N_KV_HEADS + k, (g * 3 + 2) * N_KV_HEADS + k
            comb = sig[:, c_cmp:c_cmp + 1] * o_c[r] + sig[:, c_win:c_win + 1] * o_w[r]
            chunk = chunk + jnp.where((lane // HEAD_DIM) == k, comb, 0.0)
        part_ref[0, :, g * HEADS_LANES:(g + 1) * HEADS_LANES] = chunk


def _sample_attention_a(rel_bias, q, cmp_kv, win_t, *, pos0, ts, wp):
    bsz = q.shape[0]
    nc = cmp_kv.shape[1]
    assert pos0 % BLOCK == 0 and ts < BLOCK and pos0 // BLOCK == nc and ts <= SROWS and TOP_N >= 3
    whole = lambda arr: pl.BlockSpec((1,) + arr.shape[1:], lambda b: (b, 0, 0))
    return pl.pallas_call(
        functools.partial(_sattn_a_kernel, pos0=pos0, ts=ts, wp=wp, nc=nc),
        grid=(bsz,),
        in_specs=[pl.BlockSpec(memory_space=pltpu.SMEM), whole(q), whole(cmp_kv), whole(win_t)],
        out_specs=[pl.BlockSpec((1, SROWS, D_MODEL), lambda b: (b, 0, 0)),
                   pl.BlockSpec((1, SROWS, N_KV_HEADS * nc), lambda b: (b, 0, 0))],
        out_shape=[jax.ShapeDtypeStruct((bsz, SROWS, D_MODEL), F32),
                   jax.ShapeDtypeStruct((bsz, SROWS, N_KV_HEADS * nc), F32)],
        compiler_params=_cparams(("parallel",)),
        name="sample_attention_a",
    )(rel_bias, q, cmp_kv, win_t)


def _sattn_b_kernel(pt_ref, rb_ref, q_ref, sel_ref, part_ref, knew_ref, *refs, pos0, ts, nc, n_pg, n_seq):
    k_refs = [refs[s * n_pg:(s + 1) * n_pg] for s in range(n_seq)]
    v_refs = [refs[(n_seq + s) * n_pg:(n_seq + s + 1) * n_pg] for s in range(n_seq)]
    o_ref, expand_ref = refs[2 * n_seq * n_pg:2 * n_seq * n_pg + 2]
    state = refs[2 * n_seq * n_pg + 2:]
    p_idx = pl.program_id(1)
    keys = n_pg * PAGE
    blocks = keys // BLOCK
    n_pat = LANES // blocks
    qpos = pos0 + lax.broadcasted_iota(jnp.int32, (SROWS, 1), 0)
    col = lax.broadcasted_iota(jnp.int32, (1, keys), 1)

    @pl.when((pl.program_id(0) == 0) & (p_idx == 0))
    def _():
        b_row = lax.broadcasted_iota(jnp.int32, (LANES, 1), 0)
        for o in range(n_pat):
            expand_ref[o] = jnp.where(b_row == o * blocks + col // BLOCK, -MASK_BIG, 0.0).astype(BF16)

    def near_bias(dist):
        return jnp.concatenate(
            [b - rb_ref[N_BUCKETS - 1, h] for h, b in enumerate(_rel_bias_heads(dist, rb_ref, list(range(N_HEADS))))],
            axis=0)

    page0 = p_idx * n_pg
    last = p_idx == pl.num_programs(1) - 1
    window = pl.multiple_of((2 * page0) // LANES * LANES, LANES)
    expand = expand_ref[((2 * page0) % LANES) // blocks]
    seq_state = [state[5 * s:5 * s + 5] for s in range(n_seq)]

    def update(s, scores, vt):
        _, _, m_ref, l_ref, acc_ref = seq_state[s]
        m, l, acc = _online_update((m_ref[...], l_ref[...], acc_ref[...]), scores, vt)
        m_ref[...], l_ref[...], acc_ref[...] = m, l, acc

    def page_scores(s):
        qall_ref, unsel_ref = seq_state[s][:2]
        kt = jnp.concatenate([r[0] for r in k_refs[s]], axis=1).astype(BF16)
        vt = jnp.concatenate([r[0] for r in v_refs[s]], axis=1).astype(BF16)
        return _dot(qall_ref[...], kt) + _dot(unsel_ref[:, pl.ds(window, LANES)].astype(BF16), expand), vt

    @pl.when(p_idx == 0)
    def _():
        for s in range(n_seq):
            qall_ref, unsel_ref, m_ref, l_ref, acc_ref = seq_state[s]
            qall_ref[...] = _all_head_queries(q_ref.at[pl.ds(s, 1)])
            for k in range(N_KV_HEADS):
                for g in range(GROUP):
                    r = slice((k * GROUP + g) * SROWS, (k * GROUP + g + 1) * SROWS)
                    unsel_ref[r, :] = 1.0 - sel_ref[s, :, k * nc:(k + 1) * nc]
            m_ref[...] = jnp.full(m_ref.shape, NEG, F32)
            l_ref[...] = jnp.zeros(l_ref.shape, F32)
            acc_ref[...] = jnp.zeros(acc_ref.shape, F32)

    for s in range(n_seq):
        base, vt = page_scores(s)
        update(s, base + jnp.where(last, -MASK_BIG, 0.0), vt)

    @pl.when(last)
    def _():
        for s in range(n_seq):
            qall_ref, _, _, l_ref, acc_ref = seq_state[s]
            base, vt = page_scores(s)
            update(s, base + near_bias(qpos - (page0 * PAGE + col)), vt)
            ncol = lax.broadcasted_iota(jnp.int32, (1, PAGE), 1)
            dist = qpos - (pos0 + ncol)
            add = near_bias(dist) + jnp.concatenate(
                [jnp.where((dist >= 0) & (ncol < ts), 0.0, -MASK_BIG)] * N_HEADS, axis=0)
            update(s, _dot(qall_ref[...], knew_ref[s, 0:HEADS_LANES, :].astype(BF16)) + add,
                   knew_ref[s, HEADS_LANES:2 * HEADS_LANES, :].astype(BF16))
            o_sel = acc_ref[...] / jnp.maximum(l_ref[...], TINY)
            sig = _sigmoid(q_ref[s, :, D_MODEL:D_MODEL + LANES])
            lane = lax.broadcasted_iota(jnp.int32, (1, HEADS_LANES), 1)
            for g in range(GROUP):
                sl = slice(g * HEADS_LANES, (g + 1) * HEADS_LANES)
                chunk = part_ref[s, :, sl]
                for k in range(N_KV_HEADS):
                    r = slice((k * GROUP + g) * SROWS, (k * GROUP + g + 1) * SROWS)
                    c = (g * 3 + 1) * N_KV_HEADS + k
                    chunk = chunk + jnp.where((lane // HEAD_DIM) == k, sig[:, c:c + 1] * o_sel[r], 0.0)
                o_ref[s, :, sl] = chunk


def _sample_attention_b(rel_bias, page_ids, q, sel, part, knew_t, cache_t, *, pos0, ts, n_pg):
    bsz = q.shape[0]
    n_pages = page_ids.shape[0] // bsz
    nc = sel.shape[2] // N_KV_HEADS
    n_seq = 2 if bsz % 2 == 0 else 1
    keys = n_pg * PAGE
    assert n_pages % n_pg == 0 and n_pages * 2 == nc and pos0 == n_pages * PAGE and TOP_N >= 3
    assert nc % LANES == 0 and LANES % (keys // BLOCK) == 0 and keys >= FAR_DIST

    def page_spec(slot, s, j):
        return pl.BlockSpec(
            (1, HEADS_LANES, PAGE),
            functools.partial(lambda b, p, pt, s, j, c: (pt[(b * n_seq + s) * n_pages + p * n_pg + j], c, 0),
                              s=s, j=j, c=slot))

    whole = lambda arr: pl.BlockSpec((n_seq,) + arr.shape[1:], lambda b, p, pt: (b, 0, 0))
    pages = [page_spec(slot, s, j) for slot in (2, 3) for s in range(n_seq) for j in range(n_pg)]
    per_seq = [pltpu.VMEM((ALL_ROWS, HEADS_LANES), BF16), pltpu.VMEM((ALL_ROWS, nc), F32),
               pltpu.VMEM((ALL_ROWS, 1), F32), pltpu.VMEM((ALL_ROWS, 1), F32),
               pltpu.VMEM((ALL_ROWS, HEADS_LANES), F32)]
    grid_spec = pltpu.PrefetchScalarGridSpec(
        num_scalar_prefetch=1,
        grid=(bsz // n_seq, n_pages // n_pg),
        in_specs=[pl.BlockSpec(memory_space=pltpu.SMEM), whole(q), whole(sel), whole(part), whole(knew_t)] + pages,
        out_specs=pl.BlockSpec((n_seq, SROWS, D_MODEL), lambda b, p, pt: (b, 0, 0)),
        scratch_shapes=[pltpu.VMEM((LANES // (keys // BLOCK), LANES, keys), BF16)] + per_seq * n_seq,
    )
    return pl.pallas_call(
        functools.partial(_sattn_b_kernel, pos0=pos0, ts=ts, nc=nc, n_pg=n_pg, n_seq=n_seq),
        grid_spec=grid_spec,
        out_shape=jax.ShapeDtypeStruct((bsz, SROWS, D_MODEL), F32),
        compiler_params=_cparams(("arbitrary", "arbitrary")),
        name="sample_attention_b",
    )(page_ids, rel_bias, q, sel, part, knew_t, *([cache_t] * (2 * n_seq * n_pg)))


def _router_kernel(h_ref, g_ref, wr_ref, xn_ref, comb_ref):
    xn = _rmsnorm(h_ref[...], g_ref[...]).astype(BF16)
    xn_ref[...] = xn
    logits = _dot(xn, wr_ref[...].astype(BF16))
    lane = lax.broadcasted_iota(jnp.int32, logits.shape, 1)
    lg = jnp.where(lane < N_EXPERTS, logits, -jnp.inf)
    m1 = jnp.max(lg, axis=-1, keepdims=True)
    i1 = jnp.min(jnp.where(lg == m1, lane, LANES), axis=-1, keepdims=True)
    lg2 = jnp.where(lane == i1, -jnp.inf, lg)
    m2 = jnp.max(lg2, axis=-1, keepdims=True)
    i2 = jnp.min(jnp.where(lg2 == m2, lane, LANES), axis=-1, keepdims=True)
    e = jnp.exp(m2 - m1)
    comb_ref[...] = jnp.where(lane == i1, 1.0 / (1.0 + e), 0.0) + jnp.where(lane == i2, e / (1.0 + e), 0.0)


def _router(h, g, w_router_pad, tm, name):
    m = h.shape[0]
    return pl.pallas_call(
        _router_kernel,
        grid=(m // tm,),
        in_specs=[pl.BlockSpec((tm, D_MODEL), lambda i: (i, 0)),
                  pl.BlockSpec((1, D_MODEL), lambda i: (0, 0)),
                  pl.BlockSpec((D_MODEL, LANES), lambda i: (0, 0))],
        out_specs=[pl.BlockSpec((tm, D_MODEL), lambda i: (i, 0)), pl.BlockSpec((tm, LANES), lambda i: (i, 0))],
        out_shape=[jax.ShapeDtypeStruct((m, D_MODEL), BF16), jax.ShapeDtypeStruct((m, LANES), F32)],
        compiler_params=_cparams(("parallel",)),
        name=name,
    )(h, g.reshape(1, -1), w_router_pad)


def _moe_kernel(xn_ref, comb_ref, h_ref, wg_ref, wu_ref, wd_ref, gf_ref, y_ref, acc_ref):
    e, f = pl.program_id(1), pl.program_id(2)
    last_e, last_f = pl.num_programs(1) - 1, pl.num_programs(2) - 1

    @pl.when((e == 0) & (f == 0))
    def _():
        y_ref[...] = h_ref[...]

    @pl.when(f == 0)
    def _():
        acc_ref[...] = jnp.zeros(acc_ref.shape, F32)

    x = xn_ref[...]
    hid = _silu(_dot(x, wg_ref[0].astype(BF16))) * _dot(x, wu_ref[0].astype(BF16))
    acc_ref[...] += _dot(hid.astype(BF16), wd_ref[0].astype(BF16))

    @pl.when(f == last_f)
    def _():
        lane = lax.broadcasted_iota(jnp.int32, comb_ref.shape, 1)
        ce = jnp.sum(jnp.where(lane == e, comb_ref[...], 0.0), axis=-1, keepdims=True)
        y_ref[...] += ce * acc_ref[...]

    @pl.when((e == last_e) & (f == last_f))
    def _():
        y_ref[...] = _rmsnorm(y_ref[...], gf_ref[...])


def _moe(xn, comb, h, wg, wu, wd, g_final, tm, tf, name):
    m = h.shape[0]
    n_e, _, d_ff = wg.shape
    assert m % tm == 0 and d_ff % tf == 0
    row = lambda i, e, f: (i, 0)
    return pl.pallas_call(
        _moe_kernel,
        grid=(m // tm, n_e, d_ff // tf),
        in_specs=[pl.BlockSpec((tm, D_MODEL), row), pl.BlockSpec((tm, LANES), row), pl.BlockSpec((tm, D_MODEL), row),
                  pl.BlockSpec((1, D_MODEL, tf), lambda i, e, f: (e, 0, f)),
                  pl.BlockSpec((1, D_MODEL, tf), lambda i, e, f: (e, 0, f)),
                  pl.BlockSpec((1, tf, D_MODEL), lambda i, e, f: (e, f, 0)),
                  pl.BlockSpec((1, D_MODEL), lambda i, e, f: (0, 0))],
        out_specs=pl.BlockSpec((tm, D_MODEL), row),
        out_shape=jax.ShapeDtypeStruct((m, D_MODEL), F32),
        scratch_shapes=[pltpu.VMEM((tm, D_MODEL), F32)],
        compiler_params=_cparams(("parallel", "arbitrary", "arbitrary")),
        name=name,
    )(xn, comb, h, wg, wu, wd, g_final.reshape(1, -1))


def _route_kernel(h_ref, g_ref, wr_ref, xn_ref, idx_ref, w_ref):
    xn = _rmsnorm(h_ref[...], g_ref[...])
    xn_ref[...] = xn
    logits = _dot(xn.astype(BF16), wr_ref[...].astype(BF16))
    lane = lax.broadcasted_iota(jnp.int32, logits.shape, 1)
    lg = jnp.where(lane < N_EXPERTS, logits, -jnp.inf)
    m1 = jnp.max(lg, axis=-1, keepdims=True)
    i1 = jnp.min(jnp.where(lg == m1, lane, LANES), axis=-1, keepdims=True)
    lg2 = jnp.where(lane == i1, -jnp.inf, lg)
    m2 = jnp.max(lg2, axis=-1, keepdims=True)
    i2 = jnp.min(jnp.where(lg2 == m2, lane, LANES), axis=-1, keepdims=True)
    e = jnp.exp(m2 - m1)
    idx_ref[...] = jnp.where(lane == 0, i1, jnp.where(lane == 1, i2, 0))
    w_ref[...] = jnp.where(lane == 0, 1.0 / (1.0 + e), jnp.where(lane == 1, e / (1.0 + e), 0.0))


def _route(h, g, w_router_pad, tm, name):
    m = h.shape[0]
    row = lambda n: pl.BlockSpec((tm, n), lambda i: (i, 0))
    return pl.pallas_call(
        _route_kernel,
        grid=(m // tm,),
        in_specs=[row(D_MODEL), pl.BlockSpec((1, D_MODEL), lambda i: (0, 0)),
                  pl.BlockSpec((D_MODEL, LANES), lambda i: (0, 0))],
        out_specs=[row(D_MODEL), row(LANES), row(LANES)],
        out_shape=[jax.ShapeDtypeStruct((m, D_MODEL), F32), jax.ShapeDtypeStruct((m, LANES), jnp.int32),
                   jax.ShapeDtypeStruct((m, LANES), F32)],
        compiler_params=_cparams(("parallel",)),
        name=name,
    )(h, g.reshape(1, -1), w_router_pad)


ROW_GATHER_STEP = 2048


def _gather_rows_kernel(idx_ref, src_ref, o_ref, sem, *, rows, parts):
    base = pl.program_id(0) * rows
    width = src_ref.shape[1]

    def copies(t, src_rows):
        return [pltpu.make_async_copy(src_ref.at[pl.ds(src_rows[h], 1)],
                                      o_ref.at[pl.ds(t, 1), pl.ds(h * width, width)], sem) for h in range(parts)]

    def issue(t, carry):
        for c in copies(t, [idx_ref[base + parts * t + h] for h in range(parts)]):
            c.start()
        return carry

    def drain(t, carry):
        for c in copies(t, [0] * parts):
            c.wait()
        return carry

    lax.fori_loop(0, rows // parts, issue, 0, unroll=8)
    lax.fori_loop(0, rows // parts, drain, 0, unroll=8)


def _gather_rows(src, idx, parts, name):
    n = idx.shape[0]
    rows = _row_tile(n, ROW_GATHER_STEP)
    width = src.shape[1]
    grid_spec = pltpu.PrefetchScalarGridSpec(
        num_scalar_prefetch=1,
        grid=(n // rows,),
        in_specs=[pl.BlockSpec(memory_space=pl.ANY)],
        out_specs=pl.BlockSpec((rows // parts, parts * width), lambda i, idx_ref: (i, 0)),
        scratch_shapes=[pltpu.SemaphoreType.DMA(())],
    )
    return pl.pallas_call(
        functools.partial(_gather_rows_kernel, rows=rows, parts=parts),
        grid_spec=grid_spec,
        out_shape=jax.ShapeDtypeStruct((n // parts, parts * width), src.dtype),
        compiler_params=_cparams(("arbitrary",)),
        name=name,
    )(idx, src)


def _experts_kernel(te_ref, tv_ref, x_ref, wg_ref, wu_ref, wd_ref, o_ref, *, tf):
    i = pl.program_id(0)

    @pl.when(tv_ref[i] == 0)
    def _():
        o_ref[...] = jnp.zeros(o_ref.shape, F32)

    @pl.when(tv_ref[i] != 0)
    def _():
        x = x_ref[...].astype(BF16)
        acc = jnp.zeros(o_ref.shape, F32)
        for f in range(wg_ref.shape[2] // tf):
            sl = slice(f * tf, (f + 1) * tf)
            hid = _silu(_dot(x, wg_ref[0, :, sl].astype(BF16))) * _dot(x, wu_ref[0, :, sl].astype(BF16))
            acc = acc + _dot(hid.astype(BF16), wd_ref[0, sl, :].astype(BF16))
        o_ref[...] = acc


def _experts(xg, tile_expert, tile_valid, wg, wu, wd, rows, tf, name):
    n = xg.shape[0]
    n_e, _, d_ff = wg.shape
    assert n % rows == 0 and d_ff % tf == 0
    once = pl.Buffered(1)
    grid_spec = pltpu.PrefetchScalarGridSpec(
        num_scalar_prefetch=2,
        grid=(n // rows,),
        in_specs=[pl.BlockSpec((rows, D_MODEL), lambda i, te, tv: (i, 0)),
                  pl.BlockSpec((1, D_MODEL, d_ff), lambda i, te, tv: (te[i], 0, 0), pipeline_mode=once),
                  pl.BlockSpec((1, D_MODEL, d_ff), lambda i, te, tv: (te[i], 0, 0), pipeline_mode=once),
                  pl.BlockSpec((1, d_ff, D_MODEL), lambda i, te, tv: (te[i], 0, 0), pipeline_mode=once)],
        out_specs=pl.BlockSpec((rows, D_MODEL), lambda i, te, tv: (i, 0)),
    )
    return pl.pallas_call(
        functools.partial(_experts_kernel, tf=tf),
        grid_spec=grid_spec,
        out_shape=jax.ShapeDtypeStruct((n, D_MODEL), F32),
        compiler_params=_cparams(("arbitrary",)),
        name=name,
    )(tile_expert, tile_valid, xg, wg, wu, wd)


def _moe_combine_kernel(h_ref, y2_ref, w_ref, gf_ref, o_ref):
    w = w_ref[...]
    y = h_ref[...] + w[:, 0:1] * y2_ref[:, 0:D_MODEL] + w[:, 1:2] * y2_ref[:, D_MODEL:2 * D_MODEL]
    o_ref[...] = _rmsnorm(y, gf_ref[...])


def _moe_combine(h, y2, w, g_final, tm, name):
    m = h.shape[0]
    row = lambda n: pl.BlockSpec((tm, n), lambda i: (i, 0))
    return pl.pallas_call(
        _moe_combine_kernel,
        grid=(m // tm,),
        in_specs=[row(D_MODEL), row(2 * D_MODEL), row(LANES), pl.BlockSpec((1, D_MODEL), lambda i: (0, 0))],
        out_specs=row(D_MODEL),
        out_shape=jax.ShapeDtypeStruct((m, D_MODEL), F32),
        compiler_params=_cparams(("parallel",)),
        name=name,
    )(h, y2, w, g_final.reshape(1, -1))


def _moe_grouped(h, p, tag, rows=512):
    m = h.shape[0]
    tm = _row_tile(m, 1024)
    xn, idx, w = _route(h, p["norm_ffn"][1], p["w_router_pad"], tm, "route_" + tag)
    n_e = p["moe_w_gate"].shape[1]
    expert = idx[:, :2].reshape(-1)
    onehot = (expert[:, None] == jnp.arange(n_e, dtype=jnp.int32)[None, :]).astype(jnp.int32)
    pos = jnp.sum((jnp.cumsum(onehot, axis=0) - 1) * onehot, axis=1)
    counts = jnp.sum(onehot, axis=0)
    padded = (counts + rows - 1) // rows * rows
    ends = jnp.cumsum(padded)
    row_of = (ends - padded)[expert] + pos
    n_tiles = (2 * m + n_e * (rows - 1)) // rows + 1
    src = jnp.zeros((n_tiles * rows,), jnp.int32).at[row_of].set(jnp.arange(2 * m, dtype=jnp.int32) // 2)
    tile_start = jnp.arange(n_tiles, dtype=jnp.int32) * rows
    tile_expert = jnp.minimum(jnp.sum((tile_start[:, None] >= ends[None, :]).astype(jnp.int32), axis=1), n_e - 1)
    tile_valid = (tile_start < ends[-1]).astype(jnp.int32)
    xg = _gather_rows(xn, src, 1, "moe_gather_" + tag)
    yg = _experts(xg, tile_expert, tile_valid, p["moe_w_gate"][0], p["moe_w_up"][0], p["moe_w_down"][0], rows, 256,
                  "moe_experts_" + tag)
    y2 = _gather_rows(yg, row_of, 2, "moe_ungather_" + tag)
    return _moe_combine(h, y2, w, p["norm_final"], tm, "moe_combine_" + tag)


def _compress_params(cmp_pos, cmp_w1, cmp_w2):
    pe = jnp.transpose(cmp_pos, (0, 2, 1))[:, None, :, :]
    pe = jnp.broadcast_to(pe, (2, N_KV_HEADS, HEAD_DIM, BLOCK))
    pe_t = jnp.concatenate([pe, pe], axis=-1).reshape(2 * HEADS_LANES, PAGE)
    hidden = cmp_w1.shape[2]
    w1_dp = cmp_w1.reshape(2, BLOCK, HEAD_DIM, hidden).transpose(0, 2, 1, 3)
    w1t = jnp.zeros((2, HEAD_DIM, 2, BLOCK, 2, hidden), F32)
    for blk in range(2):
        w1t = w1t.at[:, :, blk, :, blk, :].set(w1_dp)
    w1t = w1t.reshape(2, HEAD_DIM * PAGE, 2 * hidden).astype(BF16)
    w2pad = jnp.zeros((2, N_KV_HEADS, hidden, HEADS_LANES), F32)
    for h in range(N_KV_HEADS):
        w2pad = w2pad.at[:, h, :, h * HEAD_DIM:(h + 1) * HEAD_DIM].set(cmp_w2)
    return pe_t, w1t, w2pad


def _attn_weights(w_in, w_out):
    hd = N_HEADS * HEAD_DIM
    wq = w_in[:, :hd].reshape(D_MODEL, N_KV_HEADS, GROUP, HEAD_DIM).transpose(0, 2, 1, 3).reshape(D_MODEL, hd)
    wg = w_in[:, hd:].reshape(D_MODEL, N_KV_HEADS, GROUP, 3).transpose(0, 2, 3, 1).reshape(D_MODEL, 3 * N_HEADS)
    w_in_pad = jnp.concatenate([wq, wg, jnp.zeros((D_MODEL, LANES - 3 * N_HEADS), F32)], axis=1)
    w_out_perm = w_out.reshape(N_KV_HEADS, GROUP, HEAD_DIM, D_MODEL).transpose(1, 0, 2, 3).reshape(hd, D_MODEL)
    return w_in_pad, w_out_perm


def _row_tile(m, cap):
    t = cap
    while m % t:
        t //= 2
    return t


def _from_rows_t(a_t, n_slots):
    bsz, _, t = a_t.shape
    return a_t.reshape(bsz, n_slots, N_KV_HEADS, HEAD_DIM, t).transpose(0, 4, 1, 2, 3)


def _to_rows_t(a):
    bsz, t, n_slots = a.shape[:3]
    return a.transpose(0, 2, 3, 4, 1).reshape(bsz, n_slots * HEADS_LANES, t)


def _layer0_front(x, p, tag):
    m = x.shape[0]
    w1, b1 = p["conv_w_pw1"][0], p["conv_b_pw1"][0]
    return _mm(x, [(w1, 0), (w1, D_MODEL)], D_MODEL, tm=_row_tile(m, 1024), tn=512, norm_g=p["norm_mix"][0],
               biases=[(b1, 0), (b1, D_MODEL)], act="glu", name="pw1_glu_" + tag)


def _layer0_back(x, c_act, p, seq, tag):
    m = x.shape[0]
    tm = _row_tile(m, 1024)
    h1 = _mm(c_act, [(p["conv_w_pw2"][0], 0)], D_MODEL, tm=tm, tn=512, biases=[(p["conv_b_pw2"][0], 0)], residual=x,
             name="pw2_" + tag)
    d_ff = p["ffn_w_gate"].shape[2]
    a = _mm(h1, [(p["ffn_w_gate"][0], 0), (p["ffn_w_up"][0], 0)], d_ff, tm=tm, tn=256, norm_g=p["norm_ffn"][0],
            act="swiglu", out_dtype=BF16, name="ffn_up_" + tag)
    h2 = _mm(a, [(p["ffn_w_down"][0], 0)], D_MODEL, tm=_row_tile(m, 512), tn=512, residual=h1, name="ffn_down_" + tag)
    tmt = _row_tile(seq, 1024)
    paged_t = _mm_t(h2, p["w_kv_t"][:PAGED_ROWS], p["norm_kv"], seq=seq, tm=tmt, tn=512, name="kv_paged_" + tag)
    win_t = _mm_t(h2, p["w_kv_t"][PAGED_ROWS:], p["norm_kv"], seq=seq, tm=tmt, tn=512, name="kv_win_" + tag)
    proj = _mm(h2, [(p["w_in_pad"], 0)], p["w_in_pad"].shape[1], tm=tm, tn=384, norm_g=p["norm_mix"][1],
               name="q_proj_" + tag)
    return h2, paged_t, win_t, proj


def _layer1_back(o, h2, p, tag):
    m = h2.shape[0]
    tm = _row_tile(m, 1024)
    h3 = _mm(o, [(p["w_out_perm"], 0)], D_MODEL, tm=tm, tn=512, residual=h2, name="attn_out_" + tag)
    if m >= GROUPED_MOE_MIN_TOKENS:
        return _moe_grouped(h3, p, tag)
    xn, comb = _router(h3, p["norm_ffn"][1], p["w_router_pad"], tm, "router_" + tag)
    return _moe(xn, comb, h3, p["moe_w_gate"][0], p["moe_w_up"][0], p["moe_w_down"][0], p["norm_final"], tm, 256,
                "moe_" + tag)


def kernel(x_prompt, x_sample, cache_kv, state_win_kv, state_conv, page_table, norm_mix, norm_ffn, norm_kv, norm_final, conv_w_pw1, conv_b_pw1, conv_w_dw, conv_b_dw, conv_ln_g, conv_ln_b, conv_w_pw2, conv_b_pw2, w_kv, cmp_pos, cmp_w1, cmp_w2, nsa_w_in, nsa_w_out, rel_bias, ffn_w_gate, ffn_w_up, ffn_w_down, moe_w_router, moe_w_gate, moe_w_up, moe_w_down):
    p = dict(norm_mix=norm_mix, norm_ffn=norm_ffn, norm_kv=norm_kv, norm_final=norm_final,
             conv_w_pw1=conv_w_pw1, conv_b_pw1=conv_b_pw1, conv_w_pw2=conv_w_pw2, conv_b_pw2=conv_b_pw2,
             ffn_w_gate=ffn_w_gate, ffn_w_up=ffn_w_up, ffn_w_down=ffn_w_down,
             moe_w_gate=moe_w_gate, moe_w_up=moe_w_up, moe_w_down=moe_w_down)
    p["w_kv_t"] = w_kv.T
    p["w_in_pad"], p["w_out_perm"] = _attn_weights(nsa_w_in[0], nsa_w_out[0])
    p["w_router_pad"] = jnp.pad(moe_w_router[0], ((0, 0), (0, LANES - N_EXPERTS)))
    pe_t, w1t, w2pad = _compress_params(cmp_pos, cmp_w1, cmp_w2)
    conv_args = (conv_w_dw[0], conv_b_dw[0], conv_ln_g[0], conv_ln_b[0])
    n_state = CONV_WIDTH - 1

    bp, tp, _ = x_prompt.shape
    mp = bp * tp
    xp = x_prompt.reshape(mp, D_MODEL)
    glu = _layer0_front(xp, p, "p").reshape(bp, tp, D_MODEL)
    gpad = jnp.concatenate([jnp.zeros((bp, CONV_HALO, D_MODEL), F32), glu], axis=1)
    tt = 256
    c_act = _conv_ln_swish(gpad, gpad, lambda b, i: (b, i, 0), lambda b, i: (b, (i + 1) * (tt // CONV_HALO), 0),
                           tp // tt, tt, 32, *conv_args, "conv_p")
    h2, paged_t, win_t, proj = _layer0_back(xp, c_act.reshape(mp, D_MODEL), p, tp, "p")
    ppb = tp // PAGE
    cmp_p = _compress(paged_t, jnp.arange(bp * ppb, dtype=jnp.int32), lambda pid: (pid // ppb, 0, pid % ppb),
                      pe_t, w1t, w2pad, _row_tile(bp * ppb, 32), "compress_p").reshape(bp, tp // BLOCK, -1)
    o = _prompt_attention(rel_bias, proj.reshape(bp, tp, -1), cmp_p, paged_t, win_t, tq=256)
    y_p = _layer1_back(o.reshape(mp, D_MODEL), h2, p, "p").reshape(bp, tp, D_MODEL)
    kv_p = _from_rows_t(paged_t, 4)
    keep_p = min(WINDOW, tp)
    win_p = _from_rows_t(win_t[:, :, tp - keep_p:], 2)
    conv_p = glu[:, tp - n_state:][None]

    bs, ts, _ = x_sample.shape
    ms = bs * ts
    n_pages = page_table.shape[1]
    assert cache_kv.shape[1] == PAGE
    pos0 = n_pages * PAGE
    xs = x_sample.reshape(ms, D_MODEL)
    glu_s = _layer0_front(xs, p, "s").reshape(bs, ts, D_MODEL)
    full_s = jnp.concatenate([state_conv[0], glu_s], axis=1)
    conv_s = full_s[:, full_s.shape[1] - n_state:][None]
    gpad_s = jnp.concatenate([jnp.zeros((bs, CONV_HALO - n_state, D_MODEL), F32), full_s,
                              jnp.zeros((bs, SROWS - ts, D_MODEL), F32)], axis=1)
    once = lambda b, i: (b, 0, 0)
    c_act_s = _conv_ln_swish(gpad_s[:, :SROWS], gpad_s[:, SROWS:], once, once, 1, SROWS, SROWS, *conv_args, "conv_s")
    h2_s, paged_ts, win_ts, proj_s = _layer0_back(xs, c_act_s[:, :ts].reshape(ms, D_MODEL), p, ms, "s")
    new_paged_t = paged_ts[0].reshape(PAGED_ROWS, bs, ts).transpose(1, 0, 2)
    new_win_t = win_ts[0].reshape(2 * HEADS_LANES, bs, ts).transpose(1, 0, 2)
    cache_t = _to_rows_t(cache_kv)
    page_ids = page_table.reshape(-1)
    cmp_s = _compress(cache_t, page_ids, lambda pid: (pid, 0, 0), pe_t, w1t, w2pad,
                      _row_tile(page_ids.shape[0], 32), "compress_s").reshape(bs, pos0 // BLOCK, -1)
    wp = state_win_kv.shape[1]
    win_all_t = jnp.concatenate([_to_rows_t(state_win_kv), new_win_t], axis=2)
    keep_s = min(WINDOW, pos0 + ts)
    win_s = _from_rows_t(win_all_t[:, :, wp + ts - keep_s:], 2)
    pad_last = lambda a, n: jnp.pad(a, ((0, 0), (0, 0), (0, n - a.shape[2])))
    q_s = jnp.pad(proj_s.reshape(bs, ts, -1), ((0, 0), (0, SROWS - ts), (0, 0)))
    part, sel = _sample_attention_a(rel_bias, q_s, cmp_s, pad_last(win_all_t, -(-(wp + ts) // LANES) * LANES),
                                    pos0=pos0, ts=ts, wp=wp)
    knew_t = pad_last(new_paged_t[:, 2 * HEADS_LANES:], PAGE)
    o_s = _sample_attention_b(rel_bias, page_ids, q_s, sel, part, knew_t, cache_t, pos0=pos0, ts=ts, n_pg=8)
    y_s = _layer1_back(o_s[:, :ts].reshape(ms, D_MODEL), h2_s, p, "s").reshape(bs, ts, D_MODEL)
    kv_s_out = _from_rows_t(new_paged_t, 4)
    return (y_p, y_s, kv_p, win_p, conv_p, kv_s_out, win_s, conv_s)
```

```python
import functools
import math

import numpy as np
import jax
import jax.numpy as jnp
from jax import lax
from jax.experimental import pallas as pl
from jax.experimental.pallas import tpu as pltpu

F32 = jnp.float32
BF16 = jnp.bfloat16

D_MODEL = 1024
N_HEADS = 16
HEAD_DIM = 64
N_KV_HEADS = 4
GROUP = N_HEADS // N_KV_HEADS
HEADS_LANES = N_KV_HEADS * HEAD_DIM
PAGED_ROWS = 4 * HEADS_LANES
BLOCK = 64
TOP_N = 16
WINDOW = 512
CONV_WIDTH = 31
CONV_HALO = 32
CMP_PITCH = BLOCK + 8
N_BUCKETS = 32
MAX_EXACT = 16
MAX_DISTANCE = 128
N_EXPERTS = 8
GROUPED_MOE_MIN_TOKENS = 1024
FORCE_SCORE = 1e4
EPS = 1e-6
NEG = -1e30
MASK_BIG = 2.0 ** 100
TINY = 1e-30
LANES = 128
PAGE = 128
VMEM_LIMIT_BYTES = 56 * 1024 * 1024


def _bucket_thresholds():
    d = np.arange(0, 4 * MAX_DISTANCE)
    nf = np.maximum(d, 1).astype(np.float32)
    large = MAX_EXACT + (np.log(nf / np.float32(MAX_EXACT)) / np.float32(math.log(MAX_DISTANCE / MAX_EXACT))
                         * np.float32(N_BUCKETS - MAX_EXACT)).astype(np.int32)
    b = np.where(d < MAX_EXACT, d, np.minimum(large, N_BUCKETS - 1))
    assert np.all(np.diff(b) >= 0)
    return [int(np.min(d[b >= k])) for k in range(N_BUCKETS)]


BUCKET_THR = _bucket_thresholds()
FAR_DIST = BUCKET_THR[-1]


def _cparams(sem):
    return pltpu.CompilerParams(dimension_semantics=sem, vmem_limit_bytes=VMEM_LIMIT_BYTES)


def _sigmoid(x):
    return 1.0 / (1.0 + jnp.exp(-x))


def _silu(x):
    return x * _sigmoid(x)


def _rmsnorm(x, g):
    return x * lax.rsqrt(jnp.mean(x * x, axis=-1, keepdims=True) + EPS) * g


def _dot(a, b):
    return jnp.dot(a, b, preferred_element_type=F32)


def _dot_nt(a, b):
    return lax.dot_general(a, b, (((1,), (1,)), ((), ())), preferred_element_type=F32)


def _rel_bias(dist, rb_ref, head):
    val = jnp.full(dist.shape, rb_ref[0, head], F32)
    for b in range(1, N_BUCKETS):
        val = jnp.where(dist >= BUCKET_THR[b], rb_ref[b, head], val)
    return val


def _rel_bias_heads(dist, rb_ref, heads):
    ge = [dist >= BUCKET_THR[b] for b in range(1, N_BUCKETS)]
    out = []
    for hd in heads:
        val = jnp.full(dist.shape, rb_ref[0, hd], F32)
        for b in range(1, N_BUCKETS):
            val = jnp.where(ge[b - 1], rb_ref[b, hd], val)
        out.append(val)
    return out


def _softmax_parts(s, mask):
    s = jnp.where(mask, s, NEG)
    m = jnp.max(s, axis=-1, keepdims=True)
    p = jnp.where(mask, jnp.exp(s - m), 0.0)
    return p, jnp.maximum(jnp.sum(p, axis=-1, keepdims=True), TINY)


def _online_update(carry, s, vt):
    m, l, acc = carry
    m_new = jnp.maximum(m, jnp.max(s, axis=-1, keepdims=True))
    alpha = jnp.exp(m - m_new)
    p = jnp.exp(s - m_new)
    return m_new, alpha * l + jnp.sum(p, axis=-1, keepdims=True), alpha * acc + _dot_nt(p.astype(BF16), vt)


def _mm_kernel(*refs, n_w, has_norm, has_bias, act, has_res, use_scratch):
    it = iter(refs)
    x_ref = next(it)
    g_ref = next(it) if has_norm else None
    w_refs = [next(it) for _ in range(n_w)]
    b_refs = [next(it) for _ in range(n_w)] if has_bias else []
    r_ref = next(it) if has_res else None
    o_ref = next(it)
    if use_scratch:
        xb_ref = next(it)

        @pl.when(pl.program_id(1) == 0)
        def _():
            x = x_ref[...].astype(F32)
            if has_norm:
                x = _rmsnorm(x, g_ref[...])
            xb_ref[...] = x.astype(BF16)

        xb = xb_ref[...]
    else:
        xb = x_ref[...]
    ys = [_dot(xb, w[...].astype(BF16)) for w in w_refs]
    if has_bias:
        ys = [y + b[...] for y, b in zip(ys, b_refs)]
    if act == "glu":
        y = ys[0] * _sigmoid(ys[1])
    elif act == "swiglu":
        y = _silu(ys[0]) * ys[1]
    else:
        y = ys[0]
    if has_res:
        y = y + r_ref[...]
    o_ref[...] = y.astype(o_ref.dtype)


def _mm(x, ws, n_out, *, tm, tn, name, norm_g=None, biases=None, act=None, residual=None, out_dtype=F32):
    m, k = x.shape
    assert m % tm == 0 and n_out % tn == 0
    use_scratch = norm_g is not None or x.dtype != BF16
    args, specs = [x], [pl.BlockSpec((tm, k), lambda i, j: (i, 0))]
    if norm_g is not None:
        args.append(norm_g.reshape(1, k))
        specs.append(pl.BlockSpec((1, k), lambda i, j: (0, 0)))
    for w, off in ws:
        assert off % tn == 0
        args.append(w)
        specs.append(pl.BlockSpec((k, tn), functools.partial(lambda i, j, o: (0, j + o), o=off // tn)))
    if biases is not None:
        for (bias, off) in biases:
            args.append(bias.reshape(1, -1))
            specs.append(pl.BlockSpec((1, tn), functools.partial(lambda i, j, o: (0, j + o), o=off // tn)))
    if residual is not None:
        args.append(residual)
        specs.append(pl.BlockSpec((tm, tn), lambda i, j: (i, j)))
    kern = functools.partial(_mm_kernel, n_w=len(ws), has_norm=norm_g is not None, has_bias=biases is not None,
                             act=act, has_res=residual is not None, use_scratch=use_scratch)
    return pl.pallas_call(
        kern,
        grid=(m // tm, n_out // tn),
        in_specs=specs,
        out_specs=pl.BlockSpec((tm, tn), lambda i, j: (i, j)),
        out_shape=jax.ShapeDtypeStruct((m, n_out), out_dtype),
        scratch_shapes=[pltpu.VMEM((tm, k), BF16)] if use_scratch else [],
        compiler_params=_cparams(("parallel", "arbitrary")),
        name=name,
    )(*args)


def _mm_t_kernel(x_ref, g_ref, wt_ref, o_ref, xb_ref):
    @pl.when(pl.program_id(1) == 0)
    def _():
        xb_ref[...] = _rmsnorm(x_ref[...], g_ref[...]).astype(BF16)

    o_ref[0] = _dot_nt(wt_ref[...].astype(BF16), xb_ref[...])


def _mm_t(x, wt, norm_g, *, seq, tm, tn, name):
    m, k = x.shape
    n = wt.shape[0]
    assert m % seq == 0 and seq % tm == 0 and n % tn == 0
    per = seq // tm
    return pl.pallas_call(
        _mm_t_kernel,
        grid=(m // tm, n // tn),
        in_specs=[pl.BlockSpec((tm, k), lambda i, j: (i, 0)),
                  pl.BlockSpec((1, k), lambda i, j: (0, 0)),
                  pl.BlockSpec((tn, k), lambda i, j: (j, 0))],
        out_specs=pl.BlockSpec((1, tn, tm), lambda i, j: (i // per, j, i % per)),
        out_shape=jax.ShapeDtypeStruct((m // seq, n, seq), F32),
        scratch_shapes=[pltpu.VMEM((tm, k), BF16)],
        compiler_params=_cparams(("parallel", "arbitrary")),
        name=name,
    )(x, norm_g.reshape(1, k), wt)


def _conv_kernel(main_ref, tail_ref, wdw_ref, bdw_ref, lng_ref, lnb_ref, o_ref, win_ref, c_ref, *, tt, rc):
    for c in range(D_MODEL // LANES):
        sl = slice(c * LANES, (c + 1) * LANES)
        win_ref[c, 0:tt, :] = main_ref[0, :, sl]
        win_ref[c, tt:tt + CONV_HALO, :] = tail_ref[0, :, sl]

    def row_chunk(r, carry):
        r0 = pl.multiple_of(r * rc, rc)
        for c in range(D_MODEL // LANES):
            sl = slice(c * LANES, (c + 1) * LANES)
            acc = jnp.zeros((rc, LANES), F32)
            for k in range(CONV_WIDTH):
                acc = acc + win_ref[c, pl.ds(r0 + (CONV_HALO - CONV_WIDTH + 1) + k, rc), :] * wdw_ref[k:k + 1, sl]
            c_ref[pl.ds(r0, rc), sl] = acc + bdw_ref[:, sl]
        return carry

    lax.fori_loop(0, tt // rc, row_chunk, 0)
    c = c_ref[...]
    mu = jnp.mean(c, axis=-1, keepdims=True)
    xc = c - mu
    y = xc * lax.rsqrt(jnp.mean(xc * xc, axis=-1, keepdims=True) + EPS)
    y = y * lng_ref[...] + lnb_ref[...]
    o_ref[0] = _silu(y).astype(o_ref.dtype)


def _conv_ln_swish(main, tail, main_map, tail_map, n_t, tt, rc, wdw, bdw, lng, lnb, name):
    bsz = main.shape[0]
    vec = pl.BlockSpec((1, D_MODEL), lambda b, i: (0, 0))
    return pl.pallas_call(
        functools.partial(_conv_kernel, tt=tt, rc=rc),
        grid=(bsz, n_t),
        in_specs=[pl.BlockSpec((1, tt, D_MODEL), main_map),
                  pl.BlockSpec((1, CONV_HALO, D_MODEL), tail_map),
                  pl.BlockSpec((CONV_WIDTH, D_MODEL), lambda b, i: (0, 0)), vec, vec, vec],
        out_specs=pl.BlockSpec((1, tt, D_MODEL), lambda b, i: (b, i, 0)),
        out_shape=jax.ShapeDtypeStruct((bsz, n_t * tt, D_MODEL), BF16),
        scratch_shapes=[pltpu.VMEM((D_MODEL // LANES, tt + CONV_HALO, LANES), F32), pltpu.VMEM((tt, D_MODEL), F32)],
        compiler_params=_cparams(("parallel", "parallel")),
        name=name,
    )(main, tail, wdw, bdw.reshape(1, -1), lng.reshape(1, -1), lnb.reshape(1, -1))


def _compress_kernel(pt_ref, *refs, n_pages):
    x_refs = refs[:n_pages]
    pe_ref, w1_ref, w2_ref, o_ref, xs_ref = refs[n_pages:]
    hidden = w1_ref.shape[2] // 2
    rows = N_KV_HEADS * n_pages
    for j in range(n_pages):
        x = x_refs[j][0] + pe_ref[...]
        for s in range(2):
            for h in range(N_KV_HEADS):
                r0 = (s * N_KV_HEADS + h) * HEAD_DIM
                xs_ref[s, pl.ds((h * n_pages + j) * CMP_PITCH, HEAD_DIM), :] = x[r0:r0 + HEAD_DIM, :]
    for s in range(2):
        lhs = jnp.concatenate([xs_ref[s, pl.ds(d, rows, stride=CMP_PITCH), :].astype(BF16) for d in range(HEAD_DIM)],
                              axis=1)
        hid = _silu(_dot(lhs, w1_ref[s])).astype(BF16)
        for blk in range(2):
            out = jnp.zeros((n_pages, HEADS_LANES), F32)
            for h in range(N_KV_HEADS):
                out = out + _dot(hid[h * n_pages:(h + 1) * n_pages, blk * hidden:(blk + 1) * hidden],
                                 w2_ref[s, h].astype(BF16))
            c0 = blk * 2 * HEADS_LANES + s * HEADS_LANES
            o_ref[:, c0:c0 + HEADS_LANES] = out


def _compress(src, page_ids, page_index, pe_t, w1t, w2pad, n_pages, name):
    n = page_ids.shape[0]
    assert n % n_pages == 0 and n_pages % 8 == 0
    x_specs = [pl.BlockSpec((1, 2 * HEADS_LANES, PAGE),
                            functools.partial(lambda i, pt, j: page_index(pt[i * n_pages + j]), j=j))
               for j in range(n_pages)]
    grid_spec = pltpu.PrefetchScalarGridSpec(
        num_scalar_prefetch=1,
        grid=(n // n_pages,),
        in_specs=x_specs + [
            pl.BlockSpec((2 * HEADS_LANES, PAGE), lambda i, pt: (0, 0)),
            pl.BlockSpec(w1t.shape, lambda i, pt: (0, 0, 0), pipeline_mode=pl.Buffered(1)),
            pl.BlockSpec(w2pad.shape, lambda i, pt: (0, 0, 0, 0)),
        ],
        out_specs=pl.BlockSpec((n_pages, 4 * HEADS_LANES), lambda i, pt: (i, 0)),
        scratch_shapes=[pltpu.VMEM((2, N_KV_HEADS * n_pages * CMP_PITCH, LANES), F32)],
    )
    out = pl.pallas_call(
        functools.partial(_compress_kernel, n_pages=n_pages),
        grid_spec=grid_spec,
        out_shape=jax.ShapeDtypeStruct((n, 4 * HEADS_LANES), F32),
        compiler_params=_cparams(("parallel",)),
        name=name,
    )(page_ids, *([src] * n_pages), pe_t, w1t, w2pad)
    return out.reshape(2 * n, 2 * HEADS_LANES)


def _head_gate(sig, lane, g, branch, k):
    col = (g * 3 + branch) * N_KV_HEADS + k
    return jnp.sum(jnp.where(lane == col, sig, 0.0), axis=-1, keepdims=True)


def _pattn_kernel(rb_ref, q_ref, kc_ref, vc_ref, ks_ref, vs_ref, kw_ref, vw_ref, o_ref, strips_ref, *scratch,
                  tq, nc):
    qx_ref, p_ref, mv_ref, acc_ref, s_ref, mb_ref, comb_ref = (scratch[n * GROUP:(n + 1) * GROUP] for n in range(7))
    b, i, k = pl.program_id(0), pl.program_id(1), pl.program_id(2)
    rc = lax.broadcasted_iota(jnp.int32, (tq, tq), 0) - lax.broadcasted_iota(jnp.int32, (tq, tq), 1)
    n_back = WINDOW // tq

    @pl.when((b == 0) & (i == 0) & (k == 0))
    def _():
        def head_body(hh, carry):
            head = (hh & 3) * GROUP + (hh >> 2)
            far = rb_ref[N_BUCKETS - 1, head]
            strips_ref[hh * 4] = jnp.where(rc >= 0, _rel_bias(rc, rb_ref, head), -MASK_BIG)
            strips_ref[hh * 4 + 1] = _rel_bias(rc + tq, rb_ref, head)
            strips_ref[hh * 4 + 2] = jnp.full((tq, tq), far, F32)
            strips_ref[hh * 4 + 3] = jnp.where(rc + n_back * tq <= WINDOW, far, -MASK_BIG)
            return carry

        lax.fori_loop(0, N_HEADS, head_body, 0)

    lane = lax.broadcasted_iota(jnp.int32, (1, HEADS_LANES), 1)
    head_mask = (lane // HEAD_DIM) == k
    glane = lax.broadcasted_iota(jnp.int32, (1, LANES), 1)
    sig = _sigmoid(q_ref[0, :, D_MODEL:D_MODEL + LANES])
    kc = kc_ref[0].astype(BF16)
    pick = (lax.broadcasted_iota(jnp.int32, (HEADS_LANES, LANES), 0)
            == k * HEAD_DIM + lax.broadcasted_iota(jnp.int32, (HEADS_LANES, LANES), 1))
    pick = (pick & (lax.broadcasted_iota(jnp.int32, (HEADS_LANES, LANES), 1) < HEAD_DIM)).astype(BF16)
    vc = _dot(vc_ref[0].astype(BF16), pick).astype(BF16)
    eye = (rc == 0).astype(BF16)
    qs = [jnp.where(head_mask, q_ref[0, :, g * HEADS_LANES:(g + 1) * HEADS_LANES] * (HEAD_DIM ** -0.5), 0.0)
          .astype(BF16) for g in range(GROUP)]

    qpos_t = i * tq + lax.broadcasted_iota(jnp.int32, (1, tq), 1)
    blk_t = lax.broadcasted_iota(jnp.int32, (nc, 1), 0)
    cmp_end_t = blk_t * BLOCK + (BLOCK - 1)
    mask_c = cmp_end_t <= qpos_t
    bias_c = _rel_bias_heads(qpos_t - cmp_end_t, rb_ref, [k * GROUP + g for g in range(GROUP)])
    imp = jnp.zeros((nc, tq), F32)
    for g in range(GROUP):
        s = jnp.where(mask_c, _dot_nt(kc, qs[g]) + bias_c[g], NEG)
        m = jnp.max(s, axis=0, keepdims=True)
        p = jnp.where(mask_c, jnp.exp(s - m), 0.0)
        p = p / jnp.maximum(jnp.sum(p, axis=0, keepdims=True), TINY)
        imp = imp + p
        p_rows = _dot_nt(eye, p.astype(BF16)).astype(BF16)
        comb_ref[g][...] = _head_gate(sig, glane, g, 0, k) * _dot(p_rows, vc)

    cur_t = qpos_t // BLOCK
    forced = (blk_t == cur_t) | (blk_t == cur_t - 1) | (blk_t == 0)
    score = jnp.where(blk_t > cur_t, NEG, jnp.where(forced, FORCE_SCORE, imp))
    sel_t = jnp.zeros((nc, tq), F32)
    for j in range(nc):
        row = score[j:j + 1, :]
        beats = (score > row) | ((score == row) & (blk_t < j))
        rank = jnp.sum(beats.astype(F32), axis=0, keepdims=True)
        sel_t = jnp.where(blk_t == j, (rank < TOP_N).astype(F32), sel_t)
    sel = _dot_nt(eye, sel_t.astype(BF16)).astype(BF16)

    kx = (k + 1) % N_KV_HEADS
    ones_row = (lax.broadcasted_iota(jnp.int32, (LANES - HEAD_DIM, tq), 0) == 0).astype(F32)
    place = (lax.broadcasted_iota(jnp.int32, (nc, HEADS_LANES), 1)
             == kx * HEAD_DIM + lax.broadcasted_iota(jnp.int32, (nc, HEADS_LANES), 0)).astype(BF16)
    unsel_lanes = _dot(1.0 - sel, place).astype(BF16)
    row = lax.broadcasted_iota(jnp.int32, (HEADS_LANES, 1), 0)
    slot_blk = row - kx * HEAD_DIM
    in_slot = (slot_blk >= 0) & (slot_blk < nc)
    key_blk = lax.broadcasted_iota(jnp.int32, (1, tq), 1) // BLOCK

    def tile(ref, j):
        return ref[0, :, pl.ds(pl.multiple_of(j * tq, tq), tq)]

    def k_sel(j):
        mask_rows = jnp.where(slot_blk == j * (tq // BLOCK) + key_blk, -MASK_BIG, 0.0)
        return jnp.where(in_slot, mask_rows, tile(ks_ref, j)).astype(BF16)

    def k_win(j):
        return tile(kw_ref, j).astype(BF16)

    def scores(g, kt, sidx):
        return _dot(qx_ref[g][...], kt) + strips_ref[(g * N_KV_HEADS + k) * 4 + sidx]

    def run_branch(queries, n_visits, k_tile, v_ref, strip_index, branch):
        for g in range(GROUP):
            qx_ref[g][...] = queries[g]
            mv_ref[g][...] = jnp.full((tq, tq // 2), NEG, F32)
            acc_ref[g][...] = jnp.zeros((tq, LANES), F32)

        def max_visit(jj, carry):
            kt, sidx = k_tile(i - jj), strip_index(jj)
            for g in range(GROUP):
                s = scores(g, kt, sidx)
                s_ref[g][jj] = s
                mv_ref[g][...] = jnp.maximum(mv_ref[g][...], jnp.maximum(s[:, :tq // 2], s[:, tq // 2:]))
            return carry

        lax.fori_loop(0, n_visits, max_visit, 0)
        for g in range(GROUP):
            mb_ref[g][...] = jnp.broadcast_to(jnp.max(mv_ref[g][...], axis=-1, keepdims=True), (tq, tq))

        def acc_visit(jj, carry):
            v_head = v_ref[0, pl.ds(pl.multiple_of(k * HEAD_DIM, HEAD_DIM), HEAD_DIM),
                           pl.ds(pl.multiple_of((i - jj) * tq, tq), tq)]
            vt = jnp.concatenate([v_head, ones_row], axis=0).astype(BF16)
            for g in range(GROUP):
                p_ref[g][...] = jnp.exp(s_ref[g][jj] - mb_ref[g][...]).astype(BF16)
            for g in range(GROUP):
                acc_ref[g][...] += _dot_nt(p_ref[g][...], vt)
            return carry

        lax.fori_loop(0, n_visits, acc_visit, 0)
        for g in range(GROUP):
            acc = acc_ref[g][...]
            scale = _head_gate(sig, glane, g, branch, k) / jnp.maximum(acc[:, HEAD_DIM:HEAD_DIM + 1], TINY)
            comb_ref[g][...] += scale * acc

    run_branch([qs[g] + unsel_lanes for g in range(GROUP)], i + 1, k_sel, vs_ref,
               lambda jj: jnp.minimum(jj, 2), 1)
    run_branch(qs, jnp.minimum(i, n_back) + 1, k_win, vw_ref,
               lambda jj: jnp.where(jj == n_back, 3, jnp.minimum(jj, 2)), 2)
    o_ref[0] = jnp.concatenate([comb_ref[g][:, 0:HEAD_DIM] for g in range(GROUP)], axis=1)


def _prompt_attention(rel_bias, proj, cmp_kv, paged_t, win_t, *, tq):
    bsz, t, _ = proj.shape
    nc = cmp_kv.shape[1]
    assert t % tq == 0 and tq % BLOCK == 0 and WINDOW == 2 * tq and nc * BLOCK == t
    assert tq + 1 >= FAR_DIST

    def rows(idx, n):
        return pl.BlockSpec((1, n, HEADS_LANES), functools.partial(lambda b, i, k, c: (b, 0, c), c=idx))

    def cols(idx):
        return pl.BlockSpec((1, HEADS_LANES, t), functools.partial(lambda b, i, k, c: (b, c, 0), c=idx))

    return pl.pallas_call(
        functools.partial(_pattn_kernel, tq=tq, nc=nc),
        grid=(bsz, t // tq, N_KV_HEADS),
        in_specs=[pl.BlockSpec(memory_space=pltpu.SMEM),
                  pl.BlockSpec((1, tq, proj.shape[2]), lambda b, i, k: (b, i, 0)),
                  rows(0, nc), rows(1, nc), cols(2), cols(3), cols(0), cols(1)],
        out_specs=pl.BlockSpec((1, tq, HEADS_LANES), lambda b, i, k: (b, i, k)),
        out_shape=jax.ShapeDtypeStruct((bsz, t, D_MODEL), F32),
        scratch_shapes=[pltpu.VMEM((4 * N_HEADS, tq, tq), F32)]
        + [pltpu.VMEM(shape, dtype) for shape, dtype in (
            ((tq, HEADS_LANES), BF16), ((tq, tq), BF16), ((tq, tq // 2), F32), ((tq, LANES), F32),
            ((t // tq, tq, tq), F32), ((tq, tq), F32), ((tq, LANES), F32)) for _ in range(GROUP)],
        compiler_params=_cparams(("arbitrary", "arbitrary", "arbitrary")),
        name="prompt_attention",
    )(rel_bias, proj, cmp_kv, cmp_kv, paged_t, paged_t, win_t, win_t)


SROWS = 8
ALL_ROWS = N_HEADS * SROWS


def _all_head_queries(q_ref):
    lane = lax.broadcasted_iota(jnp.int32, (1, HEADS_LANES), 1)
    parts = []
    for k in range(N_KV_HEADS):
        for g in range(GROUP):
            qg = q_ref[0, :, g * HEADS_LANES:(g + 1) * HEADS_LANES] * (HEAD_DIM ** -0.5)
            parts.append(jnp.where((lane // HEAD_DIM) == k, qg, 0.0))
    return jnp.concatenate(parts, axis=0).astype(BF16)


def _all_head_bias(dist, rb_ref):
    return jnp.concatenate(_rel_bias_heads(dist, rb_ref, list(range(N_HEADS))), axis=0)


def _sattn_a_kernel(rb_ref, q_ref, cmp_ref, win_ref, part_ref, sel_ref, *, pos0, ts, wp, nc):
    lane = lax.broadcasted_iota(jnp.int32, (1, HEADS_LANES), 1)
    sig = _sigmoid(q_ref[0, :, D_MODEL:D_MODEL + LANES])
    q = _all_head_queries(q_ref)
    qpos = pos0 + lax.broadcasted_iota(jnp.int32, (SROWS, 1), 0)
    tile_rows = lambda a: jnp.concatenate([a] * N_HEADS, axis=0)

    blk = lax.broadcasted_iota(jnp.int32, (1, nc), 1)
    cmp_end = blk * BLOCK + (BLOCK - 1)
    s = _dot_nt(q, cmp_ref[0, :, 0:HEADS_LANES].astype(BF16)) + _all_head_bias(qpos - cmp_end, rb_ref)
    p, l = _softmax_parts(s, tile_rows(cmp_end <= qpos))
    p = p / l
    o_c = _dot(p.astype(BF16), cmp_ref[0, :, HEADS_LANES:2 * HEADS_LANES].astype(BF16))

    imp = jnp.concatenate(
        [sum(p[(k * GROUP + g) * SROWS:(k * GROUP + g + 1) * SROWS] for g in range(GROUP))
         for k in range(N_KV_HEADS)], axis=0)
    cur = jnp.concatenate([qpos // BLOCK] * N_KV_HEADS, axis=0)
    forced = (blk == cur) | (blk == cur - 1) | (blk == 0)
    score = jnp.where(blk > cur, NEG, jnp.where(forced, FORCE_SCORE, imp))
    sel = jnp.zeros(score.shape, F32)
    for _ in range(TOP_N - 1):
        m = jnp.max(score, axis=-1, keepdims=True)
        idx = jnp.min(jnp.where(score == m, blk, nc), axis=-1, keepdims=True)
        hit = blk == idx
        sel = jnp.where(hit, 1.0, sel)
        score = jnp.where(hit, -jnp.inf, score)
    for k in range(N_KV_HEADS):
        sel_ref[0, :, k * nc:(k + 1) * nc] = sel[k * SROWS:(k + 1) * SROWS]

    wk = win_ref.shape[2]
    j = lax.broadcasted_iota(jnp.int32, (1, wk), 1)
    kw_pos = pos0 - wp + j
    dist = qpos - kw_pos
    mask = (dist >= 0) & (dist <= WINDOW) & (kw_pos >= 0) & (j < wp + ts)
    s = _dot(q, win_ref[0, 0:HEADS_LANES, :].astype(BF16)) + _all_head_bias(dist, rb_ref)
    p, l = _softmax_parts(s, tile_rows(mask))
    o_w = _dot_nt((p / l).astype(BF16), win_ref[0, HEADS_LANES:2 * HEADS_LANES, :].astype(BF16))

    for g in range(GROUP):
        chunk = jnp.zeros((SROWS, HEADS_LANES), F32)
        for k in range(N_KV_HEADS):
            r = slice((k * GROUP + g) * SROWS, (k * GROUP + g + 1) * SROWS)
            c_cmp, c_win = (g * 3 + 0) * N_KV_HEADS + k, (g * 3 + 2) * N_KV_HEADS + k
            comb = sig[:, c_cmp:c_cmp + 1] * o_c[r] + sig[:, c_win:c_win + 1] * o_w[r]
            chunk = chunk + jnp.where((lane // HEAD_DIM) == k, comb, 0.0)
        part_ref[0, :, g * HEADS_LANES:(g + 1) * HEADS_LANES] = chunk


def _sample_attention_a(rel_bias, q, cmp_kv, win_t, *, pos0, ts, wp):
    bsz = q.shape[0]
    nc = cmp_kv.shape[1]
    assert pos0 % BLOCK == 0 and ts < BLOCK and pos0 // BLOCK == nc and ts <= SROWS and TOP_N >= 3
    whole = lambda arr: pl.BlockSpec((1,) + arr.shape[1:], lambda b: (b, 0, 0))
    return pl.pallas_call(
        functools.partial(_sattn_a_kernel, pos0=pos0, ts=ts, wp=wp, nc=nc),
        grid=(bsz,),
        in_specs=[pl.BlockSpec(memory_space=pltpu.SMEM), whole(q), whole(cmp_kv), whole(win_t)],
        out_specs=[pl.BlockSpec((1, SROWS, D_MODEL), lambda b: (b, 0, 0)),
                   pl.BlockSpec((1, SROWS, N_KV_HEADS * nc), lambda b: (b, 0, 0))],
        out_shape=[jax.ShapeDtypeStruct((bsz, SROWS, D_MODEL), F32),
                   jax.ShapeDtypeStruct((bsz, SROWS, N_KV_HEADS * nc), F32)],
        compiler_params=_cparams(("parallel",)),
        name="sample_attention_a",
    )(rel_bias, q, cmp_kv, win_t)


def _sattn_b_kernel(pt_ref, rb_ref, q_ref, sel_ref, part_ref, knew_ref, *refs, pos0, ts, nc, n_pg, n_seq):
    k_refs = [refs[s * n_pg:(s + 1) * n_pg] for s in range(n_seq)]
    v_refs = [refs[(n_seq + s) * n_pg:(n_seq + s + 1) * n_pg] for s in range(n_seq)]
    o_ref, expand_ref = refs[2 * n_seq * n_pg:2 * n_seq * n_pg + 2]
    state = refs[2 * n_seq * n_pg + 2:]
    p_idx = pl.program_id(1)
    keys = n_pg * PAGE
    blocks = keys // BLOCK
    n_pat = LANES // blocks
    qpos = pos0 + lax.broadcasted_iota(jnp.int32, (SROWS, 1), 0)
    col = lax.broadcasted_iota(jnp.int32, (1, keys), 1)

    @pl.when((pl.program_id(0) == 0) & (p_idx == 0))
    def _():
        b_row = lax.broadcasted_iota(jnp.int32, (LANES, 1), 0)
        for o in range(n_pat):
            expand_ref[o] = jnp.where(b_row == o * blocks + col // BLOCK, -MASK_BIG, 0.0).astype(BF16)

    def near_bias(dist):
        return jnp.concatenate(
            [b - rb_ref[N_BUCKETS - 1, h] for h, b in enumerate(_rel_bias_heads(dist, rb_ref, list(range(N_HEADS))))],
            axis=0)

    page0 = p_idx * n_pg
    last = p_idx == pl.num_programs(1) - 1
    window = pl.multiple_of((2 * page0) // LANES * LANES, LANES)
    expand = expand_ref[((2 * page0) % LANES) // blocks]
    seq_state = [state[5 * s:5 * s + 5] for s in range(n_seq)]

    def update(s, scores, vt):
        _, _, m_ref, l_ref, acc_ref = seq_state[s]
        m, l, acc = _online_update((m_ref[...], l_ref[...], acc_ref[...]), scores, vt)
        m_ref[...], l_ref[...], acc_ref[...] = m, l, acc

    def page_scores(s):
        qall_ref, unsel_ref = seq_state[s][:2]
        kt = jnp.concatenate([r[0] for r in k_refs[s]], axis=1).astype(BF16)
        vt = jnp.concatenate([r[0] for r in v_refs[s]], axis=1).astype(BF16)
        return _dot(qall_ref[...], kt) + _dot(unsel_ref[:, pl.ds(window, LANES)].astype(BF16), expand), vt

    @pl.when(p_idx == 0)
    def _():
        for s in range(n_seq):
            qall_ref, unsel_ref, m_ref, l_ref, acc_ref = seq_state[s]
            qall_ref[...] = _all_head_queries(q_ref.at[pl.ds(s, 1)])
            for k in range(N_KV_HEADS):
                for g in range(GROUP):
                    r = slice((k * GROUP + g) * SROWS, (k * GROUP + g + 1) * SROWS)
                    unsel_ref[r, :] = 1.0 - sel_ref[s, :, k * nc:(k + 1) * nc]
            m_ref[...] = jnp.full(m_ref.shape, NEG, F32)
            l_ref[...] = jnp.zeros(l_ref.shape, F32)
            acc_ref[...] = jnp.zeros(acc_ref.shape, F32)

    for s in range(n_seq):
        base, vt = page_scores(s)
        update(s, base + jnp.where(last, -MASK_BIG, 0.0), vt)

    @pl.when(last)
    def _():
        for s in range(n_seq):
            qall_ref, _, _, l_ref, acc_ref = seq_state[s]
            base, vt = page_scores(s)
            update(s, base + near_bias(qpos - (page0 * PAGE + col)), vt)
            ncol = lax.broadcasted_iota(jnp.int32, (1, PAGE), 1)
            dist = qpos - (pos0 + ncol)
            add = near_bias(dist) + jnp.concatenate(
                [jnp.where((dist >= 0) & (ncol < ts), 0.0, -MASK_BIG)] * N_HEADS, axis=0)
            update(s, _dot(qall_ref[...], knew_ref[s, 0:HEADS_LANES, :].astype(BF16)) + add,
                   knew_ref[s, HEADS_LANES:2 * HEADS_LANES, :].astype(BF16))
            o_sel = acc_ref[...] / jnp.maximum(l_ref[...], TINY)
            sig = _sigmoid(q_ref[s, :, D_MODEL:D_MODEL + LANES])
            lane = lax.broadcasted_iota(jnp.int32, (1, HEADS_LANES), 1)
            for g in range(GROUP):
                sl = slice(g * HEADS_LANES, (g + 1) * HEADS_LANES)
                chunk = part_ref[s, :, sl]
                for k in range(N_KV_HEADS):
                    r = slice((k * GROUP + g) * SROWS, (k * GROUP + g + 1) * SROWS)
                    c = (g * 3 + 1) * N_KV_HEADS + k
                    chunk = chunk + jnp.where((lane // HEAD_DIM) == k, sig[:, c:c + 1] * o_sel[r], 0.0)
                o_ref[s, :, sl] = chunk


def _sample_attention_b(rel_bias, page_ids, q, sel, part, knew_t, cache_t, *, pos0, ts, n_pg):
    bsz = q.shape[0]
    n_pages = page_ids.shape[0] // bsz
    nc = sel.shape[2] // N_KV_HEADS
    n_seq = 2 if bsz % 2 == 0 else 1
    keys = n_pg * PAGE
    assert n_pages % n_pg == 0 and n_pages * 2 == nc and pos0 == n_pages * PAGE and TOP_N >= 3
    assert nc % LANES == 0 and LANES % (keys // BLOCK) == 0 and keys >= FAR_DIST

    def page_spec(slot, s, j):
        return pl.BlockSpec(
            (1, HEADS_LANES, PAGE),
            functools.partial(lambda b, p, pt, s, j, c: (pt[(b * n_seq + s) * n_pages + p * n_pg + j], c, 0),
                              s=s, j=j, c=slot))

    whole = lambda arr: pl.BlockSpec((n_seq,) + arr.shape[1:], lambda b, p, pt: (b, 0, 0))
    pages = [page_spec(slot, s, j) for slot in (2, 3) for s in range(n_seq) for j in range(n_pg)]
    per_seq = [pltpu.VMEM((ALL_ROWS, HEADS_LANES), BF16), pltpu.VMEM((ALL_ROWS, nc), F32),
               pltpu.VMEM((ALL_ROWS, 1), F32), pltpu.VMEM((ALL_ROWS, 1), F32),
               pltpu.VMEM((ALL_ROWS, HEADS_LANES), F32)]
    grid_spec = pltpu.PrefetchScalarGridSpec(
        num_scalar_prefetch=1,
        grid=(bsz // n_seq, n_pages // n_pg),
        in_specs=[pl.BlockSpec(memory_space=pltpu.SMEM), whole(q), whole(sel), whole(part), whole(knew_t)] + pages,
        out_specs=pl.BlockSpec((n_seq, SROWS, D_MODEL), lambda b, p, pt: (b, 0, 0)),
        scratch_shapes=[pltpu.VMEM((LANES // (keys // BLOCK), LANES, keys), BF16)] + per_seq * n_seq,
    )
    return pl.pallas_call(
        functools.partial(_sattn_b_kernel, pos0=pos0, ts=ts, nc=nc, n_pg=n_pg, n_seq=n_seq),
        grid_spec=grid_spec,
        out_shape=jax.ShapeDtypeStruct((bsz, SROWS, D_MODEL), F32),
        compiler_params=_cparams(("arbitrary", "arbitrary")),
        name="sample_attention_b",
    )(page_ids, rel_bias, q, sel, part, knew_t, *([cache_t] * (2 * n_seq * n_pg)))


def _router_kernel(h_ref, g_ref, wr_ref, xn_ref, comb_ref):
    xn = _rmsnorm(h_ref[...], g_ref[...]).astype(BF16)
    xn_ref[...] = xn
    logits = _dot(xn, wr_ref[...].astype(BF16))
    lane = lax.broadcasted_iota(jnp.int32, logits.shape, 1)
    lg = jnp.where(lane < N_EXPERTS, logits, -jnp.inf)
    m1 = jnp.max(lg, axis=-1, keepdims=True)
    i1 = jnp.min(jnp.where(lg == m1, lane, LANES), axis=-1, keepdims=True)
    lg2 = jnp.where(lane == i1, -jnp.inf, lg)
    m2 = jnp.max(lg2, axis=-1, keepdims=True)
    i2 = jnp.min(jnp.where(lg2 == m2, lane, LANES), axis=-1, keepdims=True)
    e = jnp.exp(m2 - m1)
    comb_ref[...] = jnp.where(lane == i1, 1.0 / (1.0 + e), 0.0) + jnp.where(lane == i2, e / (1.0 + e), 0.0)


def _router(h, g, w_router_pad, tm, name):
    m = h.shape[0]
    return pl.pallas_call(
        _router_kernel,
        grid=(m // tm,),
        in_specs=[pl.BlockSpec((tm, D_MODEL), lambda i: (i, 0)),
                  pl.BlockSpec((1, D_MODEL), lambda i: (0, 0)),
                  pl.BlockSpec((D_MODEL, LANES), lambda i: (0, 0))],
        out_specs=[pl.BlockSpec((tm, D_MODEL), lambda i: (i, 0)), pl.BlockSpec((tm, LANES), lambda i: (i, 0))],
        out_shape=[jax.ShapeDtypeStruct((m, D_MODEL), BF16), jax.ShapeDtypeStruct((m, LANES), F32)],
        compiler_params=_cparams(("parallel",)),
        name=name,
    )(h, g.reshape(1, -1), w_router_pad)


def _moe_kernel(xn_ref, comb_ref, h_ref, wg_ref, wu_ref, wd_ref, gf_ref, y_ref, acc_ref):
    e, f = pl.program_id(1), pl.program_id(2)
    last_e, last_f = pl.num_programs(1) - 1, pl.num_programs(2) - 1

    @pl.when((e == 0) & (f == 0))
    def _():
        y_ref[...] = h_ref[...]

    @pl.when(f == 0)
    def _():
        acc_ref[...] = jnp.zeros(acc_ref.shape, F32)

    x = xn_ref[...]
    hid = _silu(_dot(x, wg_ref[0].astype(BF16))) * _dot(x, wu_ref[0].astype(BF16))
    acc_ref[...] += _dot(hid.astype(BF16), wd_ref[0].astype(BF16))

    @pl.when(f == last_f)
    def _():
        lane = lax.broadcasted_iota(jnp.int32, comb_ref.shape, 1)
        ce = jnp.sum(jnp.where(lane == e, comb_ref[...], 0.0), axis=-1, keepdims=True)
        y_ref[...] += ce * acc_ref[...]

    @pl.when((e == last_e) & (f == last_f))
    def _():
        y_ref[...] = _rmsnorm(y_ref[...], gf_ref[...])


def _moe(xn, comb, h, wg, wu, wd, g_final, tm, tf, name):
    m = h.shape[0]
    n_e, _, d_ff = wg.shape
    assert m % tm == 0 and d_ff % tf == 0
    row = lambda i, e, f: (i, 0)
    return pl.pallas_call(
        _moe_kernel,
        grid=(m // tm, n_e, d_ff // tf),
        in_specs=[pl.BlockSpec((tm, D_MODEL), row), pl.BlockSpec((tm, LANES), row), pl.BlockSpec((tm, D_MODEL), row),
                  pl.BlockSpec((1, D_MODEL, tf), lambda i, e, f: (e, 0, f)),
                  pl.BlockSpec((1, D_MODEL, tf), lambda i, e, f: (e, 0, f)),
                  pl.BlockSpec((1, tf, D_MODEL), lambda i, e, f: (e, f, 0)),
                  pl.BlockSpec((1, D_MODEL), lambda i, e, f: (0, 0))],
        out_specs=pl.BlockSpec((tm, D_MODEL), row),
        out_shape=jax.ShapeDtypeStruct((m, D_MODEL), F32),
        scratch_shapes=[pltpu.VMEM((tm, D_MODEL), F32)],
        compiler_params=_cparams(("parallel", "arbitrary", "arbitrary")),
        name=name,
    )(xn, comb, h, wg, wu, wd, g_final.reshape(1, -1))


def _route_kernel(h_ref, g_ref, wr_ref, xn_ref, idx_ref, w_ref):
    xn = _rmsnorm(h_ref[...], g_ref[...])
    xn_ref[...] = xn
    logits = _dot(xn.astype(BF16), wr_ref[...].astype(BF16))
    lane = lax.broadcasted_iota(jnp.int32, logits.shape, 1)
    lg = jnp.where(lane < N_EXPERTS, logits, -jnp.inf)
    m1 = jnp.max(lg, axis=-1, keepdims=True)
    i1 = jnp.min(jnp.where(lg == m1, lane, LANES), axis=-1, keepdims=True)
    lg2 = jnp.where(lane == i1, -jnp.inf, lg)
    m2 = jnp.max(lg2, axis=-1, keepdims=True)
    i2 = jnp.min(jnp.where(lg2 == m2, lane, LANES), axis=-1, keepdims=True)
    e = jnp.exp(m2 - m1)
    idx_ref[...] = jnp.where(lane == 0, i1, jnp.where(lane == 1, i2, 0))
    w_ref[...] = jnp.where(lane == 0, 1.0 / (1.0 + e), jnp.where(lane == 1, e / (1.0 + e), 0.0))


def _route(h, g, w_router_pad, tm, name):
    m = h.shape[0]
    row = lambda n: pl.BlockSpec((tm, n), lambda i: (i, 0))
    return pl.pallas_call(
        _route_kernel,
        grid=(m // tm,),
        in_specs=[row(D_MODEL), pl.BlockSpec((1, D_MODEL), lambda i: (0, 0)),
                  pl.BlockSpec((D_MODEL, LANES), lambda i: (0, 0))],
        out_specs=[row(D_MODEL), row(LANES), row(LANES)],
        out_shape=[jax.ShapeDtypeStruct((m, D_MODEL), F32), jax.ShapeDtypeStruct((m, LANES), jnp.int32),
                   jax.ShapeDtypeStruct((m, LANES), F32)],
        compiler_params=_cparams(("parallel",)),
        name=name,
    )(h, g.reshape(1, -1), w_router_pad)


ROW_GATHER_STEP = 2048


def _gather_rows_kernel(idx_ref, src_ref, o_ref, sem, *, rows, parts):
    base = pl.program_id(0) * rows
    width = src_ref.shape[1]

    def copies(t, src_rows):
        return [pltpu.make_async_copy(src_ref.at[pl.ds(src_rows[h], 1)],
                                      o_ref.at[pl.ds(t, 1), pl.ds(h * width, width)], sem) for h in range(parts)]

    def issue(t, carry):
        for c in copies(t, [idx_ref[base + parts * t + h] for h in range(parts)]):
            c.start()
        return carry

    def drain(t, carry):
        for c in copies(t, [0] * parts):
            c.wait()
        return carry

    lax.fori_loop(0, rows // parts, issue, 0, unroll=8)
    lax.fori_loop(0, rows // parts, drain, 0, unroll=8)


def _gather_rows(src, idx, parts, name):
    n = idx.shape[0]
    rows = _row_tile(n, ROW_GATHER_STEP)
    width = src.shape[1]
    grid_spec = pltpu.PrefetchScalarGridSpec(
        num_scalar_prefetch=1,
        grid=(n // rows,),
        in_specs=[pl.BlockSpec(memory_space=pl.ANY)],
        out_specs=pl.BlockSpec((rows // parts, parts * width), lambda i, idx_ref: (i, 0)),
        scratch_shapes=[pltpu.SemaphoreType.DMA(())],
    )
    return pl.pallas_call(
        functools.partial(_gather_rows_kernel, rows=rows, parts=parts),
        grid_spec=grid_spec,
        out_shape=jax.ShapeDtypeStruct((n // parts, parts * width), src.dtype),
        compiler_params=_cparams(("arbitrary",)),
        name=name,
    )(idx, src)


def _experts_kernel(te_ref, tv_ref, x_ref, wg_ref, wu_ref, wd_ref, o_ref, *, tf):
    i = pl.program_id(0)

    @pl.when(tv_ref[i] == 0)
    def _():
        o_ref[...] = jnp.zeros(o_ref.shape, F32)

    @pl.when(tv_ref[i] != 0)
    def _():
        x = x_ref[...].astype(BF16)
        acc = jnp.zeros(o_ref.shape, F32)
        for f in range(wg_ref.shape[2] // tf):
            sl = slice(f * tf, (f + 1) * tf)
            hid = _silu(_dot(x, wg_ref[0, :, sl].astype(BF16))) * _dot(x, wu_ref[0, :, sl].astype(BF16))
            acc = acc + _dot(hid.astype(BF16), wd_ref[0, sl, :].astype(BF16))
        o_ref[...] = acc


def _experts(xg, tile_expert, tile_valid, wg, wu, wd, rows, tf, name):
    n = xg.shape[0]
    n_e, _, d_ff = wg.shape
    assert n % rows == 0 and d_ff % tf == 0
    once = pl.Buffered(1)
    grid_spec = pltpu.PrefetchScalarGridSpec(
        num_scalar_prefetch=2,
        grid=(n // rows,),
        in_specs=[pl.BlockSpec((rows, D_MODEL), lambda i, te, tv: (i, 0)),
                  pl.BlockSpec((1, D_MODEL, d_ff), lambda i, te, tv: (te[i], 0, 0), pipeline_mode=once),
                  pl.BlockSpec((1, D_MODEL, d_ff), lambda i, te, tv: (te[i], 0, 0), pipeline_mode=once),
                  pl.BlockSpec((1, d_ff, D_MODEL), lambda i, te, tv: (te[i], 0, 0), pipeline_mode=once)],
        out_specs=pl.BlockSpec((rows, D_MODEL), lambda i, te, tv: (i, 0)),
    )
    return pl.pallas_call(
        functools.partial(_experts_kernel, tf=tf),
        grid_spec=grid_spec,
        out_shape=jax.ShapeDtypeStruct((n, D_MODEL), F32),
        compiler_params=_cparams(("arbitrary",)),
        name=name,
    )(tile_expert, tile_valid, xg, wg, wu, wd)


def _moe_combine_kernel(h_ref, y2_ref, w_ref, gf_ref, o_ref):
    w = w_ref[...]
    y = h_ref[...] + w[:, 0:1] * y2_ref[:, 0:D_MODEL] + w[:, 1:2] * y2_ref[:, D_MODEL:2 * D_MODEL]
    o_ref[...] = _rmsnorm(y, gf_ref[...])


def _moe_combine(h, y2, w, g_final, tm, name):
    m = h.shape[0]
    row = lambda n: pl.BlockSpec((tm, n), lambda i: (i, 0))
    return pl.pallas_call(
        _moe_combine_kernel,
        grid=(m // tm,),
        in_specs=[row(D_MODEL), row(2 * D_MODEL), row(LANES), pl.BlockSpec((1, D_MODEL), lambda i: (0, 0))],
        out_specs=row(D_MODEL),
        out_shape=jax.ShapeDtypeStruct((m, D_MODEL), F32),
        compiler_params=_cparams(("parallel",)),
        name=name,
    )(h, y2, w, g_final.reshape(1, -1))


def _moe_grouped(h, p, tag, rows=512):
    m = h.shape[0]
    tm = _row_tile(m, 1024)
    xn, idx, w = _route(h, p["norm_ffn"][1], p["w_router_pad"], tm, "route_" + tag)
    n_e = p["moe_w_gate"].shape[1]
    expert = idx[:, :2].reshape(-1)
    onehot = (expert[:, None] == jnp.arange(n_e, dtype=jnp.int32)[None, :]).astype(jnp.int32)
    pos = jnp.sum((jnp.cumsum(onehot, axis=0) - 1) * onehot, axis=1)
    counts = jnp.sum(onehot, axis=0)
    padded = (counts + rows - 1) // rows * rows
    ends = jnp.cumsum(padded)
    row_of = (ends - padded)[expert] + pos
    n_tiles = (2 * m + n_e * (rows - 1)) // rows + 1
    src = (jnp.arange(n_tiles * rows, dtype=jnp.int32) % m).at[row_of].set(jnp.arange(2 * m, dtype=jnp.int32) // 2)
    tile_start = jnp.arange(n_tiles, dtype=jnp.int32) * rows
    tile_expert = jnp.minimum(jnp.sum((tile_start[:, None] >= ends[None, :]).astype(jnp.int32), axis=1), n_e - 1)
    tile_valid = (tile_start < ends[-1]).astype(jnp.int32)
    xg = _gather_rows(xn, src, 1, "moe_gather_" + tag)
    yg = _experts(xg, tile_expert, tile_valid, p["moe_w_gate"][0], p["moe_w_up"][0], p["moe_w_down"][0], rows, 256,
                  "moe_experts_" + tag)
    y2 = _gather_rows(yg, row_of, 2, "moe_ungather_" + tag)
    return _moe_combine(h, y2, w, p["norm_final"], tm, "moe_combine_" + tag)


def _compress_params(cmp_pos, cmp_w1, cmp_w2):
    pe = jnp.transpose(cmp_pos, (0, 2, 1))[:, None, :, :]
    pe = jnp.broadcast_to(pe, (2, N_KV_HEADS, HEAD_DIM, BLOCK))
    pe_t = jnp.concatenate([pe, pe], axis=-1).reshape(2 * HEADS_LANES, PAGE)
    hidden = cmp_w1.shape[2]
    w1_dp = cmp_w1.reshape(2, BLOCK, HEAD_DIM, hidden).transpose(0, 2, 1, 3)
    zeros = jnp.zeros_like(w1_dp)
    w1t = jnp.stack([jnp.concatenate([w1_dp, zeros], axis=-1), jnp.concatenate([zeros, w1_dp], axis=-1)], axis=2)
    w1t = w1t.reshape(2, HEAD_DIM * PAGE, 2 * hidden).astype(BF16)
    w2pad = jnp.zeros((2, N_KV_HEADS, hidden, HEADS_LANES), F32)
    for h in range(N_KV_HEADS):
        w2pad = w2pad.at[:, h, :, h * HEAD_DIM:(h + 1) * HEAD_DIM].set(cmp_w2)
    return pe_t, w1t, w2pad


def _attn_weights(w_in, w_out):
    hd = N_HEADS * HEAD_DIM
    wq = w_in[:, :hd].reshape(D_MODEL, N_KV_HEADS, GROUP, HEAD_DIM).transpose(0, 2, 1, 3).reshape(D_MODEL, hd)
    wg = w_in[:, hd:].reshape(D_MODEL, N_KV_HEADS, GROUP, 3).transpose(0, 2, 3, 1).reshape(D_MODEL, 3 * N_HEADS)
    w_in_pad = jnp.concatenate([wq, wg, jnp.zeros((D_MODEL, LANES - 3 * N_HEADS), F32)], axis=1)
    w_out_perm = w_out.reshape(N_KV_HEADS, GROUP, HEAD_DIM, D_MODEL).transpose(1, 0, 2, 3).reshape(hd, D_MODEL)
    return w_in_pad, w_out_perm


def _row_tile(m, cap):
    t = cap
    while m % t:
        t //= 2
    return t


def _from_rows_t(a_t, n_slots):
    bsz, _, t = a_t.shape
    return a_t.reshape(bsz, n_slots, N_KV_HEADS, HEAD_DIM, t).transpose(0, 4, 1, 2, 3)


def _to_rows_t(a):
    bsz, t, n_slots = a.shape[:3]
    return a.transpose(0, 2, 3, 4, 1).reshape(bsz, n_slots * HEADS_LANES, t)


def _layer0_front(x, p, tag):
    m = x.shape[0]
    w1, b1 = p["conv_w_pw1"][0], p["conv_b_pw1"][0]
    return _mm(x, [(w1, 0), (w1, D_MODEL)], D_MODEL, tm=_row_tile(m, 1024), tn=512, norm_g=p["norm_mix"][0],
               biases=[(b1, 0), (b1, D_MODEL)], act="glu", name="pw1_glu_" + tag)


def _layer0_back(x, c_act, p, seq, tag):
    m = x.shape[0]
    tm = _row_tile(m, 1024)
    h1 = _mm(c_act, [(p["conv_w_pw2"][0], 0)], D_MODEL, tm=tm, tn=512, biases=[(p["conv_b_pw2"][0], 0)], residual=x,
             name="pw2_" + tag)
    d_ff = p["ffn_w_gate"].shape[2]
    a = _mm(h1, [(p["ffn_w_gate"][0], 0), (p["ffn_w_up"][0], 0)], d_ff, tm=tm, tn=256, norm_g=p["norm_ffn"][0],
            act="swiglu", out_dtype=BF16, name="ffn_up_" + tag)
    h2 = _mm(a, [(p["ffn_w_down"][0], 0)], D_MODEL, tm=_row_tile(m, 512), tn=512, residual=h1, name="ffn_down_" + tag)
    tmt = _row_tile(seq, 1024)
    paged_t = _mm_t(h2, p["w_kv_t"][:PAGED_ROWS], p["norm_kv"], seq=seq, tm=tmt, tn=512, name="kv_paged_" + tag)
    win_t = _mm_t(h2, p["w_kv_t"][PAGED_ROWS:], p["norm_kv"], seq=seq, tm=tmt, tn=512, name="kv_win_" + tag)
    proj = _mm(h2, [(p["w_in_pad"], 0)], p["w_in_pad"].shape[1], tm=tm, tn=384, norm_g=p["norm_mix"][1],
               name="q_proj_" + tag)
    return h2, paged_t, win_t, proj


def _layer1_back(o, w_out, h2, p, tag):
    m = h2.shape[0]
    tm = _row_tile(m, 1024)
    h3 = _mm(o, [(w_out, 0)], D_MODEL, tm=tm, tn=512, residual=h2, name="attn_out_" + tag)
    if m >= GROUPED_MOE_MIN_TOKENS:
        return _moe_grouped(h3, p, tag)
    xn, comb = _router(h3, p["norm_ffn"][1], p["w_router_pad"], tm, "router_" + tag)
    return _moe(xn, comb, h3, p["moe_w_gate"][0], p["moe_w_up"][0], p["moe_w_down"][0], p["norm_final"], tm, 256,
                "moe_" + tag)


def kernel(x_prompt, x_sample, cache_kv, state_win_kv, state_conv, page_table, norm_mix, norm_ffn, norm_kv, norm_final, conv_w_pw1, conv_b_pw1, conv_w_dw, conv_b_dw, conv_ln_g, conv_ln_b, conv_w_pw2, conv_b_pw2, w_kv, cmp_pos, cmp_w1, cmp_w2, nsa_w_in, nsa_w_out, rel_bias, ffn_w_gate, ffn_w_up, ffn_w_down, moe_w_router, moe_w_gate, moe_w_up, moe_w_down):
    p = dict(norm_mix=norm_mix, norm_ffn=norm_ffn, norm_kv=norm_kv, norm_final=norm_final,
             conv_w_pw1=conv_w_pw1, conv_b_pw1=conv_b_pw1, conv_w_pw2=conv_w_pw2, conv_b_pw2=conv_b_pw2,
             ffn_w_gate=ffn_w_gate, ffn_w_up=ffn_w_up, ffn_w_down=ffn_w_down,
             moe_w_gate=moe_w_gate, moe_w_up=moe_w_up, moe_w_down=moe_w_down)
    p["w_kv_t"] = w_kv.T
    p["w_in_pad"], p["w_out_perm"] = _attn_weights(nsa_w_in[0], nsa_w_out[0])
    p["w_router_pad"] = jnp.pad(moe_w_router[0], ((0, 0), (0, LANES - N_EXPERTS)))
    pe_t, w1t, w2pad = _compress_params(cmp_pos, cmp_w1, cmp_w2)
    conv_args = (conv_w_dw[0], conv_b_dw[0], conv_ln_g[0], conv_ln_b[0])
    n_state = CONV_WIDTH - 1

    bp, tp, _ = x_prompt.shape
    mp = bp * tp
    xp = x_prompt.reshape(mp, D_MODEL)
    glu = _layer0_front(xp, p, "p").reshape(bp, tp, D_MODEL)
    gpad = jnp.concatenate([jnp.zeros((bp, CONV_HALO, D_MODEL), F32), glu], axis=1)
    tt = 256
    c_act = _conv_ln_swish(gpad, gpad, lambda b, i: (b, i, 0), lambda b, i: (b, (i + 1) * (tt // CONV_HALO), 0),
                           tp // tt, tt, 32, *conv_args, "conv_p")
    h2, paged_t, win_t, proj = _layer0_back(xp, c_act.reshape(mp, D_MODEL), p, tp, "p")
    ppb = tp // PAGE
    cmp_p = _compress(paged_t, jnp.arange(bp * ppb, dtype=jnp.int32), lambda pid: (pid // ppb, 0, pid % ppb),
                      pe_t, w1t, w2pad, _row_tile(bp * ppb, 32), "compress_p").reshape(bp, tp // BLOCK, -1)
    o = _prompt_attention(rel_bias, proj.reshape(bp, tp, -1), cmp_p, paged_t, win_t, tq=256)
    y_p = _layer1_back(o.reshape(mp, D_MODEL), nsa_w_out[0], h2, p, "p").reshape(bp, tp, D_MODEL)
    kv_p = _from_rows_t(paged_t, 4)
    keep_p = min(WINDOW, tp)
    win_p = _from_rows_t(win_t[:, :, tp - keep_p:], 2)
    conv_p = glu[:, tp - n_state:][None]

    bs, ts, _ = x_sample.shape
    ms = bs * ts
    n_pages = page_table.shape[1]
    assert cache_kv.shape[1] == PAGE
    pos0 = n_pages * PAGE
    xs = x_sample.reshape(ms, D_MODEL)
    glu_s = _layer0_front(xs, p, "s").reshape(bs, ts, D_MODEL)
    full_s = jnp.concatenate([state_conv[0], glu_s], axis=1)
    conv_s = full_s[:, full_s.shape[1] - n_state:][None]
    gpad_s = jnp.concatenate([jnp.zeros((bs, CONV_HALO - n_state, D_MODEL), F32), full_s,
                              jnp.zeros((bs, SROWS - ts, D_MODEL), F32)], axis=1)
    once = lambda b, i: (b, 0, 0)
    c_act_s = _conv_ln_swish(gpad_s[:, :SROWS], gpad_s[:, SROWS:], once, once, 1, SROWS, SROWS, *conv_args, "conv_s")
    h2_s, paged_ts, win_ts, proj_s = _layer0_back(xs, c_act_s[:, :ts].reshape(ms, D_MODEL), p, ms, "s")
    new_paged_t = paged_ts[0].reshape(PAGED_ROWS, bs, ts).transpose(1, 0, 2)
    new_win_t = win_ts[0].reshape(2 * HEADS_LANES, bs, ts).transpose(1, 0, 2)
    cache_t = _to_rows_t(cache_kv)
    page_ids = page_table.reshape(-1)
    cmp_s = _compress(cache_t, page_ids, lambda pid: (pid, 0, 0), pe_t, w1t, w2pad,
                      _row_tile(page_ids.shape[0], 32), "compress_s").reshape(bs, pos0 // BLOCK, -1)
    wp = state_win_kv.shape[1]
    win_all_t = jnp.concatenate([_to_rows_t(state_win_kv), new_win_t], axis=2)
    keep_s = min(WINDOW, pos0 + ts)
    win_s = _from_rows_t(win_all_t[:, :, wp + ts - keep_s:], 2)
    pad_last = lambda a, n: jnp.pad(a, ((0, 0), (0, 0), (0, n - a.shape[2])))
    q_s = jnp.pad(proj_s.reshape(bs, ts, -1), ((0, 0), (0, SROWS - ts), (0, 0)))
    part, sel = _sample_attention_a(rel_bias, q_s, cmp_s, pad_last(win_all_t, -(-(wp + ts) // LANES) * LANES),
                                    pos0=pos0, ts=ts, wp=wp)
    knew_t = pad_last(new_paged_t[:, 2 * HEADS_LANES:], PAGE)
    o_s = _sample_attention_b(rel_bias, page_ids, q_s, sel, part, knew_t, cache_t, pos0=pos0, ts=ts, n_pg=8)
    y_s = _layer1_back(o_s[:, :ts].reshape(ms, D_MODEL), p["w_out_perm"], h2_s, p, "s").reshape(bs, ts, D_MODEL)
    kv_s_out = _from_rows_t(new_paged_t, 4)
    return (y_p, y_s, kv_p, win_p, conv_p, kv_s_out, win_s, conv_s)
```

```python
import functools
import math

import numpy as np
import jax
import jax.numpy as jnp
from jax import lax
from jax.experimental import pallas as pl
from jax.experimental.pallas import tpu as pltpu

F32 = jnp.float32
BF16 = jnp.bfloat16

D_MODEL = 1024
N_HEADS = 16
HEAD_DIM = 64
N_KV_HEADS = 4
GROUP = N_HEADS // N_KV_HEADS
HEADS_LANES = N_KV_HEADS * HEAD_DIM
PAGED_ROWS = 4 * HEADS_LANES
BLOCK = 64
TOP_N = 16
WINDOW = 512
CONV_WIDTH = 31
CONV_HALO = 32
CMP_PITCH = BLOCK + 8
N_BUCKETS = 32
MAX_EXACT = 16
MAX_DISTANCE = 128
N_EXPERTS = 8
GROUPED_MOE_MIN_TOKENS = 1024
FORCE_SCORE = 1e4
EPS = 1e-6
NEG = -1e30
MASK_BIG = 2.0 ** 100
TINY = 1e-30
LANES = 128
PAGE = 128
VMEM_LIMIT_BYTES = 56 * 1024 * 1024


def _bucket_thresholds():
    d = np.arange(0, 4 * MAX_DISTANCE)
    nf = np.maximum(d, 1).astype(np.float32)
    large = MAX_EXACT + (np.log(nf / np.float32(MAX_EXACT)) / np.float32(math.log(MAX_DISTANCE / MAX_EXACT))
                         * np.float32(N_BUCKETS - MAX_EXACT)).astype(np.int32)
    b = np.where(d < MAX_EXACT, d, np.minimum(large, N_BUCKETS - 1))
    assert np.all(np.diff(b) >= 0)
    return [int(np.min(d[b >= k])) for k in range(N_BUCKETS)]


BUCKET_THR = _bucket_thresholds()
FAR_DIST = BUCKET_THR[-1]


def _cparams(sem):
    return pltpu.CompilerParams(dimension_semantics=sem, vmem_limit_bytes=VMEM_LIMIT_BYTES)


def _sigmoid(x):
    return 1.0 / (1.0 + jnp.exp(-x))


def _silu(x):
    return x * _sigmoid(x)


def _rmsnorm(x, g):
    return x * lax.rsqrt(jnp.mean(x * x, axis=-1, keepdims=True) + EPS) * g


def _dot(a, b):
    return jnp.dot(a, b, preferred_element_type=F32)


def _dot_nt(a, b):
    return lax.dot_general(a, b, (((1,), (1,)), ((), ())), preferred_element_type=F32)


def _rel_bias(dist, rb_ref, head):
    val = jnp.full(dist.shape, rb_ref[0, head], F32)
    for b in range(1, N_BUCKETS):
        val = jnp.where(dist >= BUCKET_THR[b], rb_ref[b, head], val)
    return val


def _rel_bias_heads(dist, rb_ref, heads):
    ge = [dist >= BUCKET_THR[b] for b in range(1, N_BUCKETS)]
    out = []
    for hd in heads:
        val = jnp.full(dist.shape, rb_ref[0, hd], F32)
        for b in range(1, N_BUCKETS):
            val = jnp.where(ge[b - 1], rb_ref[b, hd], val)
        out.append(val)
    return out


def _softmax_parts(s, mask):
    s = jnp.where(mask, s, NEG)
    m = jnp.max(s, axis=-1, keepdims=True)
    p = jnp.where(mask, jnp.exp(s - m), 0.0)
    return p, jnp.maximum(jnp.sum(p, axis=-1, keepdims=True), TINY)


def _online_update(carry, s, vt):
    m, l, acc = carry
    m_new = jnp.maximum(m, jnp.max(s, axis=-1, keepdims=True))
    alpha = jnp.exp(m - m_new)
    p = jnp.exp(s - m_new)
    return m_new, alpha * l + jnp.sum(p, axis=-1, keepdims=True), alpha * acc + _dot_nt(p.astype(BF16), vt)


def _mm_kernel(*refs, n_w, has_norm, has_bias, act, has_res, use_scratch):
    it = iter(refs)
    x_ref = next(it)
    g_ref = next(it) if has_norm else None
    w_refs = [next(it) for _ in range(n_w)]
    b_refs = [next(it) for _ in range(n_w)] if has_bias else []
    r_ref = next(it) if has_res else None
    o_ref = next(it)
    if use_scratch:
        xb_ref = next(it)

        @pl.when(pl.program_id(1) == 0)
        def _():
            x = x_ref[...].astype(F32)
            if has_norm:
                x = _rmsnorm(x, g_ref[...])
            xb_ref[...] = x.astype(BF16)

        xb = xb_ref[...]
    else:
        xb = x_ref[...]
    ys = [_dot(xb, w[...].astype(BF16)) for w in w_refs]
    if has_bias:
        ys = [y + b[...] for y, b in zip(ys, b_refs)]
    if act == "glu":
        y = ys[0] * _sigmoid(ys[1])
    elif act == "swiglu":
        y = _silu(ys[0]) * ys[1]
    else:
        y = ys[0]
    if has_res:
        y = y + r_ref[...]
    o_ref[...] = y.astype(o_ref.dtype)


def _mm(x, ws, n_out, *, tm, tn, name, norm_g=None, biases=None, act=None, residual=None, out_dtype=F32):
    m, k = x.shape
    assert m % tm == 0 and n_out % tn == 0
    use_scratch = norm_g is not None or x.dtype != BF16
    args, specs = [x], [pl.BlockSpec((tm, k), lambda i, j: (i, 0))]
    if norm_g is not None:
        args.append(norm_g.reshape(1, k))
        specs.append(pl.BlockSpec((1, k), lambda i, j: (0, 0)))
    for w, off in ws:
        assert off % tn == 0
        args.append(w)
        specs.append(pl.BlockSpec((k, tn), functools.partial(lambda i, j, o: (0, j + o), o=off // tn)))
    if biases is not None:
        for (bias, off) in biases:
            args.append(bias.reshape(1, -1))
            specs.append(pl.BlockSpec((1, tn), functools.partial(lambda i, j, o: (0, j + o), o=off // tn)))
    if residual is not None:
        args.append(residual)
        specs.append(pl.BlockSpec((tm, tn), lambda i, j: (i, j)))
    kern = functools.partial(_mm_kernel, n_w=len(ws), has_norm=norm_g is not None, has_bias=biases is not None,
                             act=act, has_res=residual is not None, use_scratch=use_scratch)
    return pl.pallas_call(
        kern,
        grid=(m // tm, n_out // tn),
        in_specs=specs,
        out_specs=pl.BlockSpec((tm, tn), lambda i, j: (i, j)),
        out_shape=jax.ShapeDtypeStruct((m, n_out), out_dtype),
        scratch_shapes=[pltpu.VMEM((tm, k), BF16)] if use_scratch else [],
        compiler_params=_cparams(("parallel", "arbitrary")),
        name=name,
    )(*args)


def _mm_t_kernel(x_ref, g_ref, wt_ref, o_ref, xb_ref):
    @pl.when(pl.program_id(1) == 0)
    def _():
        xb_ref[...] = _rmsnorm(x_ref[...], g_ref[...]).astype(BF16)

    o_ref[0] = _dot_nt(wt_ref[...].astype(BF16), xb_ref[...])


def _mm_t(x, wt, norm_g, *, seq, tm, tn, name):
    m, k = x.shape
    n = wt.shape[0]
    assert m % seq == 0 and seq % tm == 0 and n % tn == 0
    per = seq // tm
    return pl.pallas_call(
        _mm_t_kernel,
        grid=(m // tm, n // tn),
        in_specs=[pl.BlockSpec((tm, k), lambda i, j: (i, 0)),
                  pl.BlockSpec((1, k), lambda i, j: (0, 0)),
                  pl.BlockSpec((tn, k), lambda i, j: (j, 0))],
        out_specs=pl.BlockSpec((1, tn, tm), lambda i, j: (i // per, j, i % per)),
        out_shape=jax.ShapeDtypeStruct((m // seq, n, seq), F32),
        scratch_shapes=[pltpu.VMEM((tm, k), BF16)],
        compiler_params=_cparams(("parallel", "arbitrary")),
        name=name,
    )(x, norm_g.reshape(1, k), wt)


def _conv_kernel(main_ref, prev_ref, prefix_ref, wdw_ref, bdw_ref, lng_ref, lnb_ref, o_ref, win_ref, c_ref, *, tt, rc):
    first = pl.program_id(1) == 0
    for c in range(D_MODEL // LANES):
        sl = slice(c * LANES, (c + 1) * LANES)
        win_ref[c, 0:CONV_HALO, :] = jnp.where(first, prefix_ref[0, :, sl], prev_ref[0, :, sl])
        win_ref[c, CONV_HALO:CONV_HALO + tt, :] = main_ref[0, :, sl]

    def row_chunk(r, carry):
        r0 = pl.multiple_of(r * rc, rc)
        for c in range(D_MODEL // LANES):
            sl = slice(c * LANES, (c + 1) * LANES)
            acc = jnp.zeros((rc, LANES), F32)
            for k in range(CONV_WIDTH):
                acc = acc + win_ref[c, pl.ds(r0 + (CONV_HALO - CONV_WIDTH + 1) + k, rc), :] * wdw_ref[k:k + 1, sl]
            c_ref[pl.ds(r0, rc), sl] = acc + bdw_ref[:, sl]
        return carry

    lax.fori_loop(0, tt // rc, row_chunk, 0)
    c = c_ref[...]
    mu = jnp.mean(c, axis=-1, keepdims=True)
    xc = c - mu
    y = xc * lax.rsqrt(jnp.mean(xc * xc, axis=-1, keepdims=True) + EPS)
    y = y * lng_ref[...] + lnb_ref[...]
    o_ref[0] = _silu(y).astype(o_ref.dtype)


def _conv_ln_swish(seq, prefix, tt, rc, wdw, bdw, lng, lnb, name):
    bsz, t, _ = seq.shape
    assert t % tt == 0 and (tt % CONV_HALO == 0 or t == tt)
    vec = pl.BlockSpec((1, D_MODEL), lambda b, i: (0, 0))
    prev = seq if t > tt else prefix
    prev_map = lambda b, i: ((b if prev.shape[0] > 1 else 0), jnp.maximum(i * (tt // CONV_HALO) - 1, 0), 0)
    prefix_map = lambda b, i: ((b if prefix.shape[0] > 1 else 0), 0, 0)
    return pl.pallas_call(
        functools.partial(_conv_kernel, tt=tt, rc=rc),
        grid=(bsz, t // tt),
        in_specs=[pl.BlockSpec((1, tt, D_MODEL), lambda b, i: (b, i, 0)),
                  pl.BlockSpec((1, CONV_HALO, D_MODEL), prev_map),
                  pl.BlockSpec((1, CONV_HALO, D_MODEL), prefix_map),
                  pl.BlockSpec((CONV_WIDTH, D_MODEL), lambda b, i: (0, 0)), vec, vec, vec],
        out_specs=pl.BlockSpec((1, tt, D_MODEL), lambda b, i: (b, i, 0)),
        out_shape=jax.ShapeDtypeStruct((bsz, t, D_MODEL), BF16),
        scratch_shapes=[pltpu.VMEM((D_MODEL // LANES, tt + CONV_HALO, LANES), F32), pltpu.VMEM((tt, D_MODEL), F32)],
        compiler_params=_cparams(("parallel", "parallel")),
        name=name,
    )(seq, prev, prefix, wdw, bdw.reshape(1, -1), lng.reshape(1, -1), lnb.reshape(1, -1))


def _compress_kernel(pt_ref, *refs, n_pages):
    x_refs = refs[:n_pages]
    pe_ref, w1_ref, w2_ref, o_ref, xs_ref = refs[n_pages:]
    hidden = w1_ref.shape[2] // 2
    rows = N_KV_HEADS * n_pages
    for j in range(n_pages):
        x = x_refs[j][0] + pe_ref[...]
        for s in range(2):
            for h in range(N_KV_HEADS):
                r0 = (s * N_KV_HEADS + h) * HEAD_DIM
                xs_ref[s, pl.ds((h * n_pages + j) * CMP_PITCH, HEAD_DIM), :] = x[r0:r0 + HEAD_DIM, :]
    for s in range(2):
        lhs = jnp.concatenate([xs_ref[s, pl.ds(d, rows, stride=CMP_PITCH), :].astype(BF16) for d in range(HEAD_DIM)],
                              axis=1)
        hid = _silu(_dot(lhs, w1_ref[s])).astype(BF16)
        for blk in range(2):
            out = jnp.zeros((n_pages, HEADS_LANES), F32)
            for h in range(N_KV_HEADS):
                out = out + _dot(hid[h * n_pages:(h + 1) * n_pages, blk * hidden:(blk + 1) * hidden],
                                 w2_ref[s, h].astype(BF16))
            c0 = blk * 2 * HEADS_LANES + s * HEADS_LANES
            o_ref[:, c0:c0 + HEADS_LANES] = out


def _compress(src, page_ids, page_index, pe_t, w1t, w2pad, n_pages, name):
    n = page_ids.shape[0]
    assert n % n_pages == 0 and n_pages % 8 == 0
    x_specs = [pl.BlockSpec((1, 2 * HEADS_LANES, PAGE),
                            functools.partial(lambda i, pt, j: page_index(pt[i * n_pages + j]), j=j))
               for j in range(n_pages)]
    grid_spec = pltpu.PrefetchScalarGridSpec(
        num_scalar_prefetch=1,
        grid=(n // n_pages,),
        in_specs=x_specs + [
            pl.BlockSpec((2 * HEADS_LANES, PAGE), lambda i, pt: (0, 0)),
            pl.BlockSpec(w1t.shape, lambda i, pt: (0, 0, 0), pipeline_mode=pl.Buffered(1)),
            pl.BlockSpec(w2pad.shape, lambda i, pt: (0, 0, 0, 0)),
        ],
        out_specs=pl.BlockSpec((n_pages, 4 * HEADS_LANES), lambda i, pt: (i, 0)),
        scratch_shapes=[pltpu.VMEM((2, N_KV_HEADS * n_pages * CMP_PITCH, LANES), F32)],
    )
    out = pl.pallas_call(
        functools.partial(_compress_kernel, n_pages=n_pages),
        grid_spec=grid_spec,
        out_shape=jax.ShapeDtypeStruct((n, 4 * HEADS_LANES), F32),
        compiler_params=_cparams(("parallel",)),
        name=name,
    )(page_ids, *([src] * n_pages), pe_t, w1t, w2pad)
    return out.reshape(2 * n, 2 * HEADS_LANES)


def _head_gate(sig, lane, g, branch, k):
    col = (g * 3 + branch) * N_KV_HEADS + k
    return jnp.sum(jnp.where(lane == col, sig, 0.0), axis=-1, keepdims=True)


def _pattn_kernel(rb_ref, q_ref, kc_ref, vc_ref, ks_ref, vs_ref, kw_ref, vw_ref, o_ref, strips_ref, *scratch,
                  tq, nc):
    qx_ref, p_ref, mv_ref, acc_ref, s_ref, mb_ref, comb_ref = (scratch[n * GROUP:(n + 1) * GROUP] for n in range(7))
    b, i, k = pl.program_id(0), pl.program_id(1), pl.program_id(2)
    rc = lax.broadcasted_iota(jnp.int32, (tq, tq), 0) - lax.broadcasted_iota(jnp.int32, (tq, tq), 1)
    n_back = WINDOW // tq

    @pl.when((b == 0) & (i == 0) & (k == 0))
    def _():
        def head_body(hh, carry):
            head = (hh & 3) * GROUP + (hh >> 2)
            far = rb_ref[N_BUCKETS - 1, head]
            strips_ref[hh * 4] = jnp.where(rc >= 0, _rel_bias(rc, rb_ref, head), -MASK_BIG)
            strips_ref[hh * 4 + 1] = _rel_bias(rc + tq, rb_ref, head)
            strips_ref[hh * 4 + 2] = jnp.full((tq, tq), far, F32)
            strips_ref[hh * 4 + 3] = jnp.where(rc + n_back * tq <= WINDOW, far, -MASK_BIG)
            return carry

        lax.fori_loop(0, N_HEADS, head_body, 0)

    lane = lax.broadcasted_iota(jnp.int32, (1, HEADS_LANES), 1)
    head_mask = (lane // HEAD_DIM) == k
    glane = lax.broadcasted_iota(jnp.int32, (1, LANES), 1)
    sig = _sigmoid(q_ref[0, :, D_MODEL:D_MODEL + LANES])
    kc = kc_ref[0].astype(BF16)
    pick = (lax.broadcasted_iota(jnp.int32, (HEADS_LANES, LANES), 0)
            == k * HEAD_DIM + lax.broadcasted_iota(jnp.int32, (HEADS_LANES, LANES), 1))
    pick = (pick & (lax.broadcasted_iota(jnp.int32, (HEADS_LANES, LANES), 1) < HEAD_DIM)).astype(BF16)
    vc = _dot(vc_ref[0].astype(BF16), pick).astype(BF16)
    eye = (rc == 0).astype(BF16)
    qs = [jnp.where(head_mask, q_ref[0, :, g * HEADS_LANES:(g + 1) * HEADS_LANES] * (HEAD_DIM ** -0.5), 0.0)
          .astype(BF16) for g in range(GROUP)]

    qpos_t = i * tq + lax.broadcasted_iota(jnp.int32, (1, tq), 1)
    blk_t = lax.broadcasted_iota(jnp.int32, (nc, 1), 0)
    cmp_end_t = blk_t * BLOCK + (BLOCK - 1)
    mask_c = cmp_end_t <= qpos_t
    bias_c = _rel_bias_heads(qpos_t - cmp_end_t, rb_ref, [k * GROUP + g for g in range(GROUP)])
    imp = jnp.zeros((nc, tq), F32)
    for g in range(GROUP):
        s = jnp.where(mask_c, _dot_nt(kc, qs[g]) + bias_c[g], NEG)
        m = jnp.max(s, axis=0, keepdims=True)
        p = jnp.where(mask_c, jnp.exp(s - m), 0.0)
        p = p / jnp.maximum(jnp.sum(p, axis=0, keepdims=True), TINY)
        imp = imp + p
        p_rows = _dot_nt(eye, p.astype(BF16)).astype(BF16)
        comb_ref[g][...] = _head_gate(sig, glane, g, 0, k) * _dot(p_rows, vc)

    cur_t = qpos_t // BLOCK
    forced = (blk_t == cur_t) | (blk_t == cur_t - 1) | (blk_t == 0)
    score = jnp.where(blk_t > cur_t, NEG, jnp.where(forced, FORCE_SCORE, imp))
    sel_t = jnp.zeros((nc, tq), F32)
    for j in range(nc):
        row = score[j:j + 1, :]
        beats = (score > row) | ((score == row) & (blk_t < j))
        rank = jnp.sum(beats.astype(F32), axis=0, keepdims=True)
        sel_t = jnp.where(blk_t == j, (rank < TOP_N).astype(F32), sel_t)
    sel = _dot_nt(eye, sel_t.astype(BF16)).astype(BF16)

    kx = (k + 1) % N_KV_HEADS
    ones_row = (lax.broadcasted_iota(jnp.int32, (LANES - HEAD_DIM, tq), 0) == 0).astype(F32)
    place = (lax.broadcasted_iota(jnp.int32, (nc, HEADS_LANES), 1)
             == kx * HEAD_DIM + lax.broadcasted_iota(jnp.int32, (nc, HEADS_LANES), 0)).astype(BF16)
    unsel_lanes = _dot(1.0 - sel, place).astype(BF16)
    row = lax.broadcasted_iota(jnp.int32, (HEADS_LANES, 1), 0)
    slot_blk = row - kx * HEAD_DIM
    in_slot = (slot_blk >= 0) & (slot_blk < nc)
    key_blk = lax.broadcasted_iota(jnp.int32, (1, tq), 1) // BLOCK

    def tile(ref, j):
        return ref[0, :, pl.ds(pl.multiple_of(j * tq, tq), tq)]

    def k_sel(j):
        mask_rows = jnp.where(slot_blk == j * (tq // BLOCK) + key_blk, -MASK_BIG, 0.0)
        return jnp.where(in_slot, mask_rows, tile(ks_ref, j)).astype(BF16)

    def k_win(j):
        return tile(kw_ref, j).astype(BF16)

    def scores(g, kt, sidx):
        return _dot(qx_ref[g][...], kt) + strips_ref[(g * N_KV_HEADS + k) * 4 + sidx]

    def run_branch(queries, n_visits, k_tile, v_ref, strip_index, branch):
        for g in range(GROUP):
            qx_ref[g][...] = queries[g]
            mv_ref[g][...] = jnp.full((tq, tq // 2), NEG, F32)
            acc_ref[g][...] = jnp.zeros((tq, LANES), F32)

        def max_visit(jj, carry):
            kt, sidx = k_tile(i - jj), strip_index(jj)
            for g in range(GROUP):
                s = scores(g, kt, sidx)
                s_ref[g][jj] = s
                mv_ref[g][...] = jnp.maximum(mv_ref[g][...], jnp.maximum(s[:, :tq // 2], s[:, tq // 2:]))
            return carry

        lax.fori_loop(0, n_visits, max_visit, 0)
        for g in range(GROUP):
            mb_ref[g][...] = jnp.broadcast_to(jnp.max(mv_ref[g][...], axis=-1, keepdims=True), (tq, tq))

        def acc_visit(jj, carry):
            v_head = v_ref[0, pl.ds(pl.multiple_of(k * HEAD_DIM, HEAD_DIM), HEAD_DIM),
                           pl.ds(pl.multiple_of((i - jj) * tq, tq), tq)]
            vt = jnp.concatenate([v_head, ones_row], axis=0).astype(BF16)
            for g in range(GROUP):
                p_ref[g][...] = jnp.exp(s_ref[g][jj] - mb_ref[g][...]).astype(BF16)
            for g in range(GROUP):
                acc_ref[g][...] += _dot_nt(p_ref[g][...], vt)
            return carry

        lax.fori_loop(0, n_visits, acc_visit, 0)
        for g in range(GROUP):
            acc = acc_ref[g][...]
            scale = _head_gate(sig, glane, g, branch, k) / jnp.maximum(acc[:, HEAD_DIM:HEAD_DIM + 1], TINY)
            comb_ref[g][...] += scale * acc

    run_branch([qs[g] + unsel_lanes for g in range(GROUP)], i + 1, k_sel, vs_ref,
               lambda jj: jnp.minimum(jj, 2), 1)
    run_branch(qs, jnp.minimum(i, n_back) + 1, k_win, vw_ref,
               lambda jj: jnp.where(jj == n_back, 3, jnp.minimum(jj, 2)), 2)
    o_ref[0] = jnp.concatenate([comb_ref[g][:, 0:HEAD_DIM] for g in range(GROUP)], axis=1)


def _prompt_attention(rel_bias, proj, cmp_kv, paged_t, win_t, *, tq):
    bsz, t, _ = proj.shape
    nc = cmp_kv.shape[1]
    assert t % tq == 0 and tq % BLOCK == 0 and WINDOW == 2 * tq and nc * BLOCK == t
    assert tq + 1 >= FAR_DIST

    def rows(idx, n):
        return pl.BlockSpec((1, n, HEADS_LANES), functools.partial(lambda b, i, k, c: (b, 0, c), c=idx))

    def cols(idx):
        return pl.BlockSpec((1, HEADS_LANES, t), functools.partial(lambda b, i, k, c: (b, c, 0), c=idx))

    return pl.pallas_call(
        functools.partial(_pattn_kernel, tq=tq, nc=nc),
        grid=(bsz, t // tq, N_KV_HEADS),
        in_specs=[pl.BlockSpec(memory_space=pltpu.SMEM),
                  pl.BlockSpec((1, tq, proj.shape[2]), lambda b, i, k: (b, i, 0)),
                  rows(0, nc), rows(1, nc), cols(2), cols(3), cols(0), cols(1)],
        out_specs=pl.BlockSpec((1, tq, HEADS_LANES), lambda b, i, k: (b, i, k)),
        out_shape=jax.ShapeDtypeStruct((bsz, t, D_MODEL), F32),
        scratch_shapes=[pltpu.VMEM((4 * N_HEADS, tq, tq), F32)]
        + [pltpu.VMEM(shape, dtype) for shape, dtype in (
            ((tq, HEADS_LANES), BF16), ((tq, tq), BF16), ((tq, tq // 2), F32), ((tq, LANES), F32),
            ((t // tq, tq, tq), F32), ((tq, tq), F32), ((tq, LANES), F32)) for _ in range(GROUP)],
        compiler_params=_cparams(("arbitrary", "arbitrary", "arbitrary")),
        name="prompt_attention",
    )(rel_bias, proj, cmp_kv, cmp_kv, paged_t, paged_t, win_t, win_t)


SROWS = 8
ALL_ROWS = N_HEADS * SROWS


def _all_head_queries(q_ref):
    lane = lax.broadcasted_iota(jnp.int32, (1, HEADS_LANES), 1)
    parts = []
    for k in range(N_KV_HEADS):
        for g in range(GROUP):
            qg = q_ref[0, :, g * HEADS_LANES:(g + 1) * HEADS_LANES] * (HEAD_DIM ** -0.5)
            parts.append(jnp.where((lane // HEAD_DIM) == k, qg, 0.0))
    return jnp.concatenate(parts, axis=0).astype(BF16)


def _all_head_bias(dist, rb_ref):
    return jnp.concatenate(_rel_bias_heads(dist, rb_ref, list(range(N_HEADS))), axis=0)


def _sattn_a_kernel(rb_ref, q_ref, cmp_ref, win_ref, part_ref, sel_ref, *, pos0, ts, wp, nc):
    lane = lax.broadcasted_iota(jnp.int32, (1, HEADS_LANES), 1)
    sig = _sigmoid(q_ref[0, :, D_MODEL:D_MODEL + LANES])
    q = _all_head_queries(q_ref)
    qpos = pos0 + lax.broadcasted_iota(jnp.int32, (SROWS, 1), 0)
    tile_rows = lambda a: jnp.concatenate([a] * N_HEADS, axis=0)

    blk = lax.broadcasted_iota(jnp.int32, (1, nc), 1)
    cmp_end = blk * BLOCK + (BLOCK - 1)
    s = _dot_nt(q, cmp_ref[0, :, 0:HEADS_LANES].astype(BF16)) + _all_head_bias(qpos - cmp_end, rb_ref)
    p, l = _softmax_parts(s, tile_rows(cmp_end <= qpos))
    p = p / l
    o_c = _dot(p.astype(BF16), cmp_ref[0, :, HEADS_LANES:2 * HEADS_LANES].astype(BF16))

    imp = jnp.concatenate(
        [sum(p[(k * GROUP + g) * SROWS:(k * GROUP + g + 1) * SROWS] for g in range(GROUP))
         for k in range(N_KV_HEADS)], axis=0)
    cur = jnp.concatenate([qpos // BLOCK] * N_KV_HEADS, axis=0)
    forced = (blk == cur) | (blk == cur - 1) | (blk == 0)
    score = jnp.where(blk > cur, NEG, jnp.where(forced, FORCE_SCORE, imp))
    sel = jnp.zeros(score.shape, F32)
    for _ in range(TOP_N - 1):
        m = jnp.max(score, axis=-1, keepdims=True)
        idx = jnp.min(jnp.where(score == m, blk, nc), axis=-1, keepdims=True)
        hit = blk == idx
        sel = jnp.where(hit, 1.0, sel)
        score = jnp.where(hit, -jnp.inf, score)
    for k in range(N_KV_HEADS):
        sel_ref[0, :, k * nc:(k + 1) * nc] = sel[k * SROWS:(k + 1) * SROWS]

    wk = win_ref.shape[2]
    j = lax.broadcasted_iota(jnp.int32, (1, wk), 1)
    kw_pos = pos0 - wp + j
    dist = qpos - kw_pos
    mask = (dist >= 0) & (dist <= WINDOW) & (kw_pos >= 0) & (j < wp + ts)
    s = _dot(q, win_ref[0, 0:HEADS_LANES, :].astype(BF16)) + _all_head_bias(dist, rb_ref)
    p, l = _softmax_parts(s, tile_rows(mask))
    o_w = _dot_nt((p / l).astype(BF16), win_ref[0, HEADS_LANES:2 * HEADS_LANES, :].astype(BF16))

    for g in range(GROUP):
        chunk = jnp.zeros((SROWS, HEADS_LANES), F32)
        for k in range(N_KV_HEADS):
            r = slice((k * GROUP + g) * SROWS, (k * GROUP + g + 1) * SROWS)
            c_cmp, c_win = (g * 3 + 0) * N_KV_HEADS + k, (g * 3 + 2) * N_KV_HEADS + k
            comb = sig[:, c_cmp:c_cmp + 1] * o_c[r] + sig[:, c_win:c_win + 1] * o_w[r]
            chunk = chunk + jnp.where((lane // HEAD_DIM) == k, comb, 0.0)
        part_ref[0, :, g * HEADS_LANES:(g + 1) * HEADS_LANES] = chunk


def _sample_attention_a(rel_bias, q, cmp_kv, win_t, *, pos0, ts, wp):
    bsz = q.shape[0]
    nc = cmp_kv.shape[1]
    assert pos0 % BLOCK == 0 and ts < BLOCK and pos0 // BLOCK == nc and ts <= SROWS and TOP_N >= 3
    whole = lambda arr: pl.BlockSpec((1,) + arr.shape[1:], lambda b: (b, 0, 0))
    return pl.pallas_call(
        functools.partial(_sattn_a_kernel, pos0=pos0, ts=ts, wp=wp, nc=nc),
        grid=(bsz,),
        in_specs=[pl.BlockSpec(memory_space=pltpu.SMEM), whole(q), whole(cmp_kv), whole(win_t)],
        out_specs=[pl.BlockSpec((1, SROWS, D_MODEL), lambda b: (b, 0, 0)),
                   pl.BlockSpec((1, SROWS, N_KV_HEADS * nc), lambda b: (b, 0, 0))],
        out_shape=[jax.ShapeDtypeStruct((bsz, SROWS, D_MODEL), F32),
                   jax.ShapeDtypeStruct((bsz, SROWS, N_KV_HEADS * nc), F32)],
        compiler_params=_cparams(("parallel",)),
        name="sample_attention_a",
    )(rel_bias, q, cmp_kv, win_t)


def _sattn_b_kernel(pt_ref, rb_ref, q_ref, sel_ref, part_ref, knew_ref, *refs, pos0, ts, nc, n_pg, n_seq):
    k_refs = [refs[s * n_pg:(s + 1) * n_pg] for s in range(n_seq)]
    v_refs = [refs[(n_seq + s) * n_pg:(n_seq + s + 1) * n_pg] for s in range(n_seq)]
    o_ref, expand_ref = refs[2 * n_seq * n_pg:2 * n_seq * n_pg + 2]
    state = refs[2 * n_seq * n_pg + 2:]
    p_idx = pl.program_id(1)
    keys = n_pg * PAGE
    blocks = keys // BLOCK
    n_pat = LANES // blocks
    qpos = pos0 + lax.broadcasted_iota(jnp.int32, (SROWS, 1), 0)
    col = lax.broadcasted_iota(jnp.int32, (1, keys), 1)

    @pl.when((pl.program_id(0) == 0) & (p_idx == 0))
    def _():
        b_row = lax.broadcasted_iota(jnp.int32, (LANES, 1), 0)
        for o in range(n_pat):
            expand_ref[o] = jnp.where(b_row == o * blocks + col // BLOCK, -MASK_BIG, 0.0).astype(BF16)

    def near_bias(dist):
        return jnp.concatenate(
            [b - rb_ref[N_BUCKETS - 1, h] for h, b in enumerate(_rel_bias_heads(dist, rb_ref, list(range(N_HEADS))))],
            axis=0)

    page0 = p_idx * n_pg
    last = p_idx == pl.num_programs(1) - 1
    window = pl.multiple_of((2 * page0) // LANES * LANES, LANES)
    expand = expand_ref[((2 * page0) % LANES) // blocks]
    seq_state = [state[5 * s:5 * s + 5] for s in range(n_seq)]

    def update(s, scores, vt):
        _, _, m_ref, l_ref, acc_ref = seq_state[s]
        m, l, acc = _online_update((m_ref[...], l_ref[...], acc_ref[...]), scores, vt)
        m_ref[...], l_ref[...], acc_ref[...] = m, l, acc

    def page_scores(s):
        qall_ref, unsel_ref = seq_state[s][:2]
        kt = jnp.concatenate([r[0] for r in k_refs[s]], axis=1).astype(BF16)
        vt = jnp.concatenate([r[0] for r in v_refs[s]], axis=1).astype(BF16)
        return _dot(qall_ref[...], kt) + _dot(unsel_ref[:, pl.ds(window, LANES)].astype(BF16), expand), vt

    @pl.when(p_idx == 0)
    def _():
        for s in range(n_seq):
            qall_ref, unsel_ref, m_ref, l_ref, acc_ref = seq_state[s]
            qall_ref[...] = _all_head_queries(q_ref.at[pl.ds(s, 1)])
            for k in range(N_KV_HEADS):
                for g in range(GROUP):
                    r = slice((k * GROUP + g) * SROWS, (k * GROUP + g + 1) * SROWS)
                    unsel_ref[r, :] = 1.0 - sel_ref[s, :, k * nc:(k + 1) * nc]
            m_ref[...] = jnp.full(m_ref.shape, NEG, F32)
            l_ref[...] = jnp.zeros(l_ref.shape, F32)
            acc_ref[...] = jnp.zeros(acc_ref.shape, F32)

    for s in range(n_seq):
        base, vt = page_scores(s)
        update(s, base + jnp.where(last, -MASK_BIG, 0.0), vt)

    @pl.when(last)
    def _():
        for s in range(n_seq):
            qall_ref, _, _, l_ref, acc_ref = seq_state[s]
            base, vt = page_scores(s)
            update(s, base + near_bias(qpos - (page0 * PAGE + col)), vt)
            ncol = lax.broadcasted_iota(jnp.int32, (1, PAGE), 1)
            dist = qpos - (pos0 + ncol)
            add = near_bias(dist) + jnp.concatenate(
                [jnp.where((dist >= 0) & (ncol < ts), 0.0, -MASK_BIG)] * N_HEADS, axis=0)
            update(s, _dot(qall_ref[...], knew_ref[s, 0:HEADS_LANES, :].astype(BF16)) + add,
                   knew_ref[s, HEADS_LANES:2 * HEADS_LANES, :].astype(BF16))
            o_sel = acc_ref[...] / jnp.maximum(l_ref[...], TINY)
            sig = _sigmoid(q_ref[s, :, D_MODEL:D_MODEL + LANES])
            lane = lax.broadcasted_iota(jnp.int32, (1, HEADS_LANES), 1)
            for g in range(GROUP):
                sl = slice(g * HEADS_LANES, (g + 1) * HEADS_LANES)
                chunk = part_ref[s, :, sl]
                for k in range(N_KV_HEADS):
                    r = slice((k * GROUP + g) * SROWS, (k * GROUP + g + 1) * SROWS)
                    c = (g * 3 + 1) * N_KV_HEADS + k
                    chunk = chunk + jnp.where((lane // HEAD_DIM) == k, sig[:, c:c + 1] * o_sel[r], 0.0)
                o_ref[s, :, sl] = chunk


def _sample_attention_b(rel_bias, page_ids, q, sel, part, knew_t, cache_t, *, pos0, ts, n_pg):
    bsz = q.shape[0]
    n_pages = page_ids.shape[0] // bsz
    nc = sel.shape[2] // N_KV_HEADS
    n_seq = 2 if bsz % 2 == 0 else 1
    keys = n_pg * PAGE
    assert n_pages % n_pg == 0 and n_pages * 2 == nc and pos0 == n_pages * PAGE and TOP_N >= 3
    assert nc % LANES == 0 and LANES % (keys // BLOCK) == 0 and keys >= FAR_DIST

    def page_spec(slot, s, j):
        return pl.BlockSpec(
            (1, HEADS_LANES, PAGE),
            functools.partial(lambda b, p, pt, s, j, c: (pt[(b * n_seq + s) * n_pages + p * n_pg + j], c, 0),
                              s=s, j=j, c=slot))

    whole = lambda arr: pl.BlockSpec((n_seq,) + arr.shape[1:], lambda b, p, pt: (b, 0, 0))
    pages = [page_spec(slot, s, j) for slot in (2, 3) for s in range(n_seq) for j in range(n_pg)]
    per_seq = [pltpu.VMEM((ALL_ROWS, HEADS_LANES), BF16), pltpu.VMEM((ALL_ROWS, nc), F32),
               pltpu.VMEM((ALL_ROWS, 1), F32), pltpu.VMEM((ALL_ROWS, 1), F32),
               pltpu.VMEM((ALL_ROWS, HEADS_LANES), F32)]
    grid_spec = pltpu.PrefetchScalarGridSpec(
        num_scalar_prefetch=1,
        grid=(bsz // n_seq, n_pages // n_pg),
        in_specs=[pl.BlockSpec(memory_space=pltpu.SMEM), whole(q), whole(sel), whole(part), whole(knew_t)] + pages,
        out_specs=pl.BlockSpec((n_seq, SROWS, D_MODEL), lambda b, p, pt: (b, 0, 0)),
        scratch_shapes=[pltpu.VMEM((LANES // (keys // BLOCK), LANES, keys), BF16)] + per_seq * n_seq,
    )
    return pl.pallas_call(
        functools.partial(_sattn_b_kernel, pos0=pos0, ts=ts, nc=nc, n_pg=n_pg, n_seq=n_seq),
        grid_spec=grid_spec,
        out_shape=jax.ShapeDtypeStruct((bsz, SROWS, D_MODEL), F32),
        compiler_params=_cparams(("arbitrary", "arbitrary")),
        name="sample_attention_b",
    )(page_ids, rel_bias, q, sel, part, knew_t, *([cache_t] * (2 * n_seq * n_pg)))


def _router_kernel(h_ref, g_ref, wr_ref, xn_ref, comb_ref):
    xn = _rmsnorm(h_ref[...], g_ref[...]).astype(BF16)
    xn_ref[...] = xn
    logits = _dot(xn, wr_ref[...].astype(BF16))
    lane = lax.broadcasted_iota(jnp.int32, logits.shape, 1)
    lg = jnp.where(lane < N_EXPERTS, logits, -jnp.inf)
    m1 = jnp.max(lg, axis=-1, keepdims=True)
    i1 = jnp.min(jnp.where(lg == m1, lane, LANES), axis=-1, keepdims=True)
    lg2 = jnp.where(lane == i1, -jnp.inf, lg)
    m2 = jnp.max(lg2, axis=-1, keepdims=True)
    i2 = jnp.min(jnp.where(lg2 == m2, lane, LANES), axis=-1, keepdims=True)
    e = jnp.exp(m2 - m1)
    comb_ref[...] = jnp.where(lane == i1, 1.0 / (1.0 + e), 0.0) + jnp.where(lane == i2, e / (1.0 + e), 0.0)


def _router(h, g, w_router_pad, tm, name):
    m = h.shape[0]
    return pl.pallas_call(
        _router_kernel,
        grid=(m // tm,),
        in_specs=[pl.BlockSpec((tm, D_MODEL), lambda i: (i, 0)),
                  pl.BlockSpec((1, D_MODEL), lambda i: (0, 0)),
                  pl.BlockSpec((D_MODEL, LANES), lambda i: (0, 0))],
        out_specs=[pl.BlockSpec((tm, D_MODEL), lambda i: (i, 0)), pl.BlockSpec((tm, LANES), lambda i: (i, 0))],
        out_shape=[jax.ShapeDtypeStruct((m, D_MODEL), BF16), jax.ShapeDtypeStruct((m, LANES), F32)],
        compiler_params=_cparams(("parallel",)),
        name=name,
    )(h, g.reshape(1, -1), w_router_pad)


def _moe_kernel(xn_ref, comb_ref, h_ref, wg_ref, wu_ref, wd_ref, gf_ref, y_ref, acc_ref):
    e, f = pl.program_id(1), pl.program_id(2)
    last_e, last_f = pl.num_programs(1) - 1, pl.num_programs(2) - 1

    @pl.when((e == 0) & (f == 0))
    def _():
        y_ref[...] = h_ref[...]

    @pl.when(f == 0)
    def _():
        acc_ref[...] = jnp.zeros(acc_ref.shape, F32)

    x = xn_ref[...]
    hid = _silu(_dot(x, wg_ref[0].astype(BF16))) * _dot(x, wu_ref[0].astype(BF16))
    acc_ref[...] += _dot(hid.astype(BF16), wd_ref[0].astype(BF16))

    @pl.when(f == last_f)
    def _():
        lane = lax.broadcasted_iota(jnp.int32, comb_ref.shape, 1)
        ce = jnp.sum(jnp.where(lane == e, comb_ref[...], 0.0), axis=-1, keepdims=True)
        y_ref[...] += ce * acc_ref[...]

    @pl.when((e == last_e) & (f == last_f))
    def _():
        y_ref[...] = _rmsnorm(y_ref[...], gf_ref[...])


def _moe(xn, comb, h, wg, wu, wd, g_final, tm, tf, name):
    m = h.shape[0]
    n_e, _, d_ff = wg.shape
    assert m % tm == 0 and d_ff % tf == 0
    row = lambda i, e, f: (i, 0)
    return pl.pallas_call(
        _moe_kernel,
        grid=(m // tm, n_e, d_ff // tf),
        in_specs=[pl.BlockSpec((tm, D_MODEL), row), pl.BlockSpec((tm, LANES), row), pl.BlockSpec((tm, D_MODEL), row),
                  pl.BlockSpec((1, D_MODEL, tf), lambda i, e, f: (e, 0, f)),
                  pl.BlockSpec((1, D_MODEL, tf), lambda i, e, f: (e, 0, f)),
                  pl.BlockSpec((1, tf, D_MODEL), lambda i, e, f: (e, f, 0)),
                  pl.BlockSpec((1, D_MODEL), lambda i, e, f: (0, 0))],
        out_specs=pl.BlockSpec((tm, D_MODEL), row),
        out_shape=jax.ShapeDtypeStruct((m, D_MODEL), F32),
        scratch_shapes=[pltpu.VMEM((tm, D_MODEL), F32)],
        compiler_params=_cparams(("parallel", "arbitrary", "arbitrary")),
        name=name,
    )(xn, comb, h, wg, wu, wd, g_final.reshape(1, -1))


def _route_kernel(h_ref, g_ref, wr_ref, xn_ref, idx_ref, w_ref):
    xn = _rmsnorm(h_ref[...], g_ref[...])
    xn_ref[...] = xn
    logits = _dot(xn.astype(BF16), wr_ref[...].astype(BF16))
    lane = lax.broadcasted_iota(jnp.int32, logits.shape, 1)
    lg = jnp.where(lane < N_EXPERTS, logits, -jnp.inf)
    m1 = jnp.max(lg, axis=-1, keepdims=True)
    i1 = jnp.min(jnp.where(lg == m1, lane, LANES), axis=-1, keepdims=True)
    lg2 = jnp.where(lane == i1, -jnp.inf, lg)
    m2 = jnp.max(lg2, axis=-1, keepdims=True)
    i2 = jnp.min(jnp.where(lg2 == m2, lane, LANES), axis=-1, keepdims=True)
    e = jnp.exp(m2 - m1)
    idx_ref[...] = jnp.where(lane == 0, i1, jnp.where(lane == 1, i2, 0))
    w_ref[...] = jnp.where(lane == 0, 1.0 / (1.0 + e), jnp.where(lane == 1, e / (1.0 + e), 0.0))


def _route(h, g, w_router_pad, tm, name):
    m = h.shape[0]
    row = lambda n: pl.BlockSpec((tm, n), lambda i: (i, 0))
    return pl.pallas_call(
        _route_kernel,
        grid=(m // tm,),
        in_specs=[row(D_MODEL), pl.BlockSpec((1, D_MODEL), lambda i: (0, 0)),
                  pl.BlockSpec((D_MODEL, LANES), lambda i: (0, 0))],
        out_specs=[row(D_MODEL), row(LANES), row(LANES)],
        out_shape=[jax.ShapeDtypeStruct((m, D_MODEL), F32), jax.ShapeDtypeStruct((m, LANES), jnp.int32),
                   jax.ShapeDtypeStruct((m, LANES), F32)],
        compiler_params=_cparams(("parallel",)),
        name=name,
    )(h, g.reshape(1, -1), w_router_pad)


ROW_GATHER_STEP = 2048


def _gather_rows_kernel(idx_ref, src_ref, o_ref, sem, *, rows, parts):
    base = pl.program_id(0) * rows
    width = src_ref.shape[1]

    def copies(t, src_rows):
        return [pltpu.make_async_copy(src_ref.at[pl.ds(src_rows[h], 1)],
                                      o_ref.at[pl.ds(t, 1), pl.ds(h * width, width)], sem) for h in range(parts)]

    def issue(t, carry):
        for c in copies(t, [idx_ref[base + parts * t + h] for h in range(parts)]):
            c.start()
        return carry

    def drain(t, carry):
        for c in copies(t, [0] * parts):
            c.wait()
        return carry

    lax.fori_loop(0, rows // parts, issue, 0, unroll=8)
    lax.fori_loop(0, rows // parts, drain, 0, unroll=8)


def _gather_rows(src, idx, parts, name):
    n = idx.shape[0]
    rows = _row_tile(n, ROW_GATHER_STEP)
    width = src.shape[1]
    grid_spec = pltpu.PrefetchScalarGridSpec(
        num_scalar_prefetch=1,
        grid=(n // rows,),
        in_specs=[pl.BlockSpec(memory_space=pl.ANY)],
        out_specs=pl.BlockSpec((rows // parts, parts * width), lambda i, idx_ref: (i, 0)),
        scratch_shapes=[pltpu.SemaphoreType.DMA(())],
    )
    return pl.pallas_call(
        functools.partial(_gather_rows_kernel, rows=rows, parts=parts),
        grid_spec=grid_spec,
        out_shape=jax.ShapeDtypeStruct((n // parts, parts * width), src.dtype),
        compiler_params=_cparams(("arbitrary",)),
        name=name,
    )(idx, src)


def _experts_kernel(te_ref, tv_ref, x_ref, wg_ref, wu_ref, wd_ref, o_ref, *, tf):
    i = pl.program_id(0)

    @pl.when(tv_ref[i] == 0)
    def _():
        o_ref[...] = jnp.zeros(o_ref.shape, F32)

    @pl.when(tv_ref[i] != 0)
    def _():
        x = x_ref[...].astype(BF16)
        acc = jnp.zeros(o_ref.shape, F32)
        for f in range(wg_ref.shape[2] // tf):
            sl = slice(f * tf, (f + 1) * tf)
            hid = _silu(_dot(x, wg_ref[0, :, sl].astype(BF16))) * _dot(x, wu_ref[0, :, sl].astype(BF16))
            acc = acc + _dot(hid.astype(BF16), wd_ref[0, sl, :].astype(BF16))
        o_ref[...] = acc


def _experts(xg, tile_expert, tile_valid, wg, wu, wd, rows, tf, name):
    n = xg.shape[0]
    n_e, _, d_ff = wg.shape
    assert n % rows == 0 and d_ff % tf == 0
    once = pl.Buffered(1)
    grid_spec = pltpu.PrefetchScalarGridSpec(
        num_scalar_prefetch=2,
        grid=(n // rows,),
        in_specs=[pl.BlockSpec((rows, D_MODEL), lambda i, te, tv: (i, 0)),
                  pl.BlockSpec((1, D_MODEL, d_ff), lambda i, te, tv: (te[i], 0, 0), pipeline_mode=once),
                  pl.BlockSpec((1, D_MODEL, d_ff), lambda i, te, tv: (te[i], 0, 0), pipeline_mode=once),
                  pl.BlockSpec((1, d_ff, D_MODEL), lambda i, te, tv: (te[i], 0, 0), pipeline_mode=once)],
        out_specs=pl.BlockSpec((rows, D_MODEL), lambda i, te, tv: (i, 0)),
    )
    return pl.pallas_call(
        functools.partial(_experts_kernel, tf=tf),
        grid_spec=grid_spec,
        out_shape=jax.ShapeDtypeStruct((n, D_MODEL), F32),
        compiler_params=_cparams(("arbitrary",)),
        name=name,
    )(tile_expert, tile_valid, xg, wg, wu, wd)


def _moe_combine_kernel(h_ref, y2_ref, w_ref, gf_ref, o_ref):
    w = w_ref[...]
    y = h_ref[...] + w[:, 0:1] * y2_ref[:, 0:D_MODEL] + w[:, 1:2] * y2_ref[:, D_MODEL:2 * D_MODEL]
    o_ref[...] = _rmsnorm(y, gf_ref[...])


def _moe_combine(h, y2, w, g_final, tm, name):
    m = h.shape[0]
    row = lambda n: pl.BlockSpec((tm, n), lambda i: (i, 0))
    return pl.pallas_call(
        _moe_combine_kernel,
        grid=(m // tm,),
        in_specs=[row(D_MODEL), row(2 * D_MODEL), row(LANES), pl.BlockSpec((1, D_MODEL), lambda i: (0, 0))],
        out_specs=row(D_MODEL),
        out_shape=jax.ShapeDtypeStruct((m, D_MODEL), F32),
        compiler_params=_cparams(("parallel",)),
        name=name,
    )(h, y2, w, g_final.reshape(1, -1))


def _moe_grouped(h, p, tag, rows=512):
    m = h.shape[0]
    tm = _row_tile(m, 1024)
    xn, idx, w = _route(h, p["norm_ffn"][1], p["w_router_pad"], tm, "route_" + tag)
    n_e = p["moe_w_gate"].shape[1]
    expert = idx[:, :2].reshape(-1)
    onehot = (expert[:, None] == jnp.arange(n_e, dtype=jnp.int32)[None, :]).astype(jnp.int32)
    pos = jnp.sum((jnp.cumsum(onehot, axis=0) - 1) * onehot, axis=1)
    counts = jnp.sum(onehot, axis=0)
    padded = (counts + rows - 1) // rows * rows
    ends = jnp.cumsum(padded)
    row_of = (ends - padded)[expert] + pos
    n_tiles = (2 * m + n_e * (rows - 1)) // rows + 1
    src = (jnp.arange(n_tiles * rows, dtype=jnp.int32) % m).at[row_of].set(jnp.arange(2 * m, dtype=jnp.int32) // 2)
    tile_start = jnp.arange(n_tiles, dtype=jnp.int32) * rows
    tile_expert = jnp.minimum(jnp.sum((tile_start[:, None] >= ends[None, :]).astype(jnp.int32), axis=1), n_e - 1)
    tile_valid = (tile_start < ends[-1]).astype(jnp.int32)
    xg = _gather_rows(xn, src, 1, "moe_gather_" + tag)
    yg = _experts(xg, tile_expert, tile_valid, p["moe_w_gate"][0], p["moe_w_up"][0], p["moe_w_down"][0], rows, 256,
                  "moe_experts_" + tag)
    y2 = _gather_rows(yg, row_of, 2, "moe_ungather_" + tag)
    return _moe_combine(h, y2, w, p["norm_final"], tm, "moe_combine_" + tag)


def _compress_params(cmp_pos, cmp_w1, cmp_w2):
    pe = jnp.transpose(cmp_pos, (0, 2, 1))[:, None, :, :]
    pe = jnp.broadcast_to(pe, (2, N_KV_HEADS, HEAD_DIM, BLOCK))
    pe_t = jnp.concatenate([pe, pe], axis=-1).reshape(2 * HEADS_LANES, PAGE)
    hidden = cmp_w1.shape[2]
    w1_dp = cmp_w1.reshape(2, BLOCK, HEAD_DIM, hidden).transpose(0, 2, 1, 3)
    zeros = jnp.zeros_like(w1_dp)
    w1t = jnp.stack([jnp.concatenate([w1_dp, zeros], axis=-1), jnp.concatenate([zeros, w1_dp], axis=-1)], axis=2)
    w1t = w1t.reshape(2, HEAD_DIM * PAGE, 2 * hidden).astype(BF16)
    w2pad = jnp.zeros((2, N_KV_HEADS, hidden, HEADS_LANES), F32)
    for h in range(N_KV_HEADS):
        w2pad = w2pad.at[:, h, :, h * HEAD_DIM:(h + 1) * HEAD_DIM].set(cmp_w2)
    return pe_t, w1t, w2pad


def _attn_weights(w_in, w_out):
    hd = N_HEADS * HEAD_DIM
    wq = w_in[:, :hd].reshape(D_MODEL, N_KV_HEADS, GROUP, HEAD_DIM).transpose(0, 2, 1, 3).reshape(D_MODEL, hd)
    wg = w_in[:, hd:].reshape(D_MODEL, N_KV_HEADS, GROUP, 3).transpose(0, 2, 3, 1).reshape(D_MODEL, 3 * N_HEADS)
    w_in_pad = jnp.concatenate([wq, wg, jnp.zeros((D_MODEL, LANES - 3 * N_HEADS), F32)], axis=1)
    w_out_perm = w_out.reshape(N_KV_HEADS, GROUP, HEAD_DIM, D_MODEL).transpose(1, 0, 2, 3).reshape(hd, D_MODEL)
    return w_in_pad, w_out_perm


def _row_tile(m, cap):
    t = cap
    while m % t:
        t //= 2
    return t


def _from_rows_t(a_t, n_slots):
    bsz, _, t = a_t.shape
    return a_t.reshape(bsz, n_slots, N_KV_HEADS, HEAD_DIM, t).transpose(0, 4, 1, 2, 3)


def _to_rows_t(a):
    bsz, t, n_slots = a.shape[:3]
    return a.transpose(0, 2, 3, 4, 1).reshape(bsz, n_slots * HEADS_LANES, t)


def _layer0_front(x, p, tag):
    m = x.shape[0]
    w1, b1 = p["conv_w_pw1"][0], p["conv_b_pw1"][0]
    return _mm(x, [(w1, 0), (w1, D_MODEL)], D_MODEL, tm=_row_tile(m, 1024), tn=512, norm_g=p["norm_mix"][0],
               biases=[(b1, 0), (b1, D_MODEL)], act="glu", name="pw1_glu_" + tag)


def _layer0_back(x, c_act, p, seq, tag):
    m = x.shape[0]
    tm = _row_tile(m, 1024)
    h1 = _mm(c_act, [(p["conv_w_pw2"][0], 0)], D_MODEL, tm=tm, tn=512, biases=[(p["conv_b_pw2"][0], 0)], residual=x,
             name="pw2_" + tag)
    d_ff = p["ffn_w_gate"].shape[2]
    a = _mm(h1, [(p["ffn_w_gate"][0], 0), (p["ffn_w_up"][0], 0)], d_ff, tm=tm, tn=256, norm_g=p["norm_ffn"][0],
            act="swiglu", out_dtype=BF16, name="ffn_up_" + tag)
    h2 = _mm(a, [(p["ffn_w_down"][0], 0)], D_MODEL, tm=_row_tile(m, 512), tn=512, residual=h1, name="ffn_down_" + tag)
    tmt = _row_tile(seq, 1024)
    paged_t = _mm_t(h2, p["w_kv_t"][:PAGED_ROWS], p["norm_kv"], seq=seq, tm=tmt, tn=512, name="kv_paged_" + tag)
    win_t = _mm_t(h2, p["w_kv_t"][PAGED_ROWS:], p["norm_kv"], seq=seq, tm=tmt, tn=512, name="kv_win_" + tag)
    proj = _mm(h2, [(p["w_in_pad"], 0)], p["w_in_pad"].shape[1], tm=tm, tn=384, norm_g=p["norm_mix"][1],
               name="q_proj_" + tag)
    return h2, paged_t, win_t, proj


def _layer1_back(o, w_out, h2, p, tag):
    m = h2.shape[0]
    tm = _row_tile(m, 1024)
    h3 = _mm(o, [(w_out, 0)], D_MODEL, tm=tm, tn=512, residual=h2, name="attn_out_" + tag)
    if m >= GROUPED_MOE_MIN_TOKENS:
        return _moe_grouped(h3, p, tag)
    xn, comb = _router(h3, p["norm_ffn"][1], p["w_router_pad"], tm, "router_" + tag)
    return _moe(xn, comb, h3, p["moe_w_gate"][0], p["moe_w_up"][0], p["moe_w_down"][0], p["norm_final"], tm, 256,
                "moe_" + tag)


def kernel(x_prompt, x_sample, cache_kv, state_win_kv, state_conv, page_table, norm_mix, norm_ffn, norm_kv, norm_final, conv_w_pw1, conv_b_pw1, conv_w_dw, conv_b_dw, conv_ln_g, conv_ln_b, conv_w_pw2, conv_b_pw2, w_kv, cmp_pos, cmp_w1, cmp_w2, nsa_w_in, nsa_w_out, rel_bias, ffn_w_gate, ffn_w_up, ffn_w_down, moe_w_router, moe_w_gate, moe_w_up, moe_w_down):
    p = dict(norm_mix=norm_mix, norm_ffn=norm_ffn, norm_kv=norm_kv, norm_final=norm_final,
             conv_w_pw1=conv_w_pw1, conv_b_pw1=conv_b_pw1, conv_w_pw2=conv_w_pw2, conv_b_pw2=conv_b_pw2,
             ffn_w_gate=ffn_w_gate, ffn_w_up=ffn_w_up, ffn_w_down=ffn_w_down,
             moe_w_gate=moe_w_gate, moe_w_up=moe_w_up, moe_w_down=moe_w_down)
    p["w_kv_t"] = w_kv.T
    p["w_in_pad"], p["w_out_perm"] = _attn_weights(nsa_w_in[0], nsa_w_out[0])
    p["w_router_pad"] = jnp.pad(moe_w_router[0], ((0, 0), (0, LANES - N_EXPERTS)))
    pe_t, w1t, w2pad = _compress_params(cmp_pos, cmp_w1, cmp_w2)
    conv_args = (conv_w_dw[0], conv_b_dw[0], conv_ln_g[0], conv_ln_b[0])
    n_state = CONV_WIDTH - 1

    bp, tp, _ = x_prompt.shape
    mp = bp * tp
    xp = x_prompt.reshape(mp, D_MODEL)
    glu = _layer0_front(xp, p, "p").reshape(bp, tp, D_MODEL)
    c_act = _conv_ln_swish(glu, jnp.zeros((1, CONV_HALO, D_MODEL), F32), 256, 32, *conv_args, "conv_p")
    h2, paged_t, win_t, proj = _layer0_back(xp, c_act.reshape(mp, D_MODEL), p, tp, "p")
    ppb = tp // PAGE
    cmp_p = _compress(paged_t, jnp.arange(bp * ppb, dtype=jnp.int32), lambda pid: (pid // ppb, 0, pid % ppb),
                      pe_t, w1t, w2pad, _row_tile(bp * ppb, 32), "compress_p").reshape(bp, tp // BLOCK, -1)
    o = _prompt_attention(rel_bias, proj.reshape(bp, tp, -1), cmp_p, paged_t, win_t, tq=256)
    y_p = _layer1_back(o.reshape(mp, D_MODEL), nsa_w_out[0], h2, p, "p").reshape(bp, tp, D_MODEL)
    kv_p = _from_rows_t(paged_t, 4)
    keep_p = min(WINDOW, tp)
    win_p = _from_rows_t(win_t[:, :, tp - keep_p:], 2)
    conv_p = glu[:, tp - n_state:][None]

    bs, ts, _ = x_sample.shape
    ms = bs * ts
    n_pages = page_table.shape[1]
    assert cache_kv.shape[1] == PAGE
    pos0 = n_pages * PAGE
    xs = x_sample.reshape(ms, D_MODEL)
    glu_s = _layer0_front(xs, p, "s").reshape(bs, ts, D_MODEL)
    full_s = jnp.concatenate([state_conv[0], glu_s], axis=1)
    conv_s = full_s[:, full_s.shape[1] - n_state:][None]
    c_act_s = _conv_ln_swish(jnp.pad(glu_s, ((0, 0), (0, SROWS - ts), (0, 0))),
                             jnp.pad(state_conv[0], ((0, 0), (CONV_HALO - n_state, 0), (0, 0))),
                             SROWS, SROWS, *conv_args, "conv_s")
    h2_s, paged_ts, win_ts, proj_s = _layer0_back(xs, c_act_s[:, :ts].reshape(ms, D_MODEL), p, ms, "s")
    new_paged_t = paged_ts[0].reshape(PAGED_ROWS, bs, ts).transpose(1, 0, 2)
    new_win_t = win_ts[0].reshape(2 * HEADS_LANES, bs, ts).transpose(1, 0, 2)
    cache_t = _to_rows_t(cache_kv)
    page_ids = page_table.reshape(-1)
    cmp_s = _compress(cache_t, page_ids, lambda pid: (pid, 0, 0), pe_t, w1t, w2pad,
                      _row_tile(page_ids.shape[0], 32), "compress_s").reshape(bs, pos0 // BLOCK, -1)
    wp = state_win_kv.shape[1]
    win_all_t = jnp.concatenate([_to_rows_t(state_win_kv), new_win_t], axis=2)
    keep_s = min(WINDOW, pos0 + ts)
    win_s = _from_rows_t(win_all_t[:, :, wp + ts - keep_s:], 2)
    pad_last = lambda a, n: jnp.pad(a, ((0, 0), (0, 0), (0, n - a.shape[2])))
    q_s = jnp.pad(proj_s.reshape(bs, ts, -1), ((0, 0), (0, SROWS - ts), (0, 0)))
    part, sel = _sample_attention_a(rel_bias, q_s, cmp_s, pad_last(win_all_t, -(-(wp + ts) // LANES) * LANES),
                                    pos0=pos0, ts=ts, wp=wp)
    knew_t = pad_last(new_paged_t[:, 2 * HEADS_LANES:], PAGE)
    o_s = _sample_attention_b(rel_bias, page_ids, q_s, sel, part, knew_t, cache_t, pos0=pos0, ts=ts, n_pg=16)
    y_s = _layer1_back(o_s[:, :ts].reshape(ms, D_MODEL), p["w_out_perm"], h2_s, p, "s").reshape(bs, ts, D_MODEL)
    kv_s_out = _from_rows_t(new_paged_t, 4)
    return (y_p, y_s, kv_p, win_p, conv_p, kv_s_out, win_s, conv_s)
```

```python
import functools
import math

import numpy as np
import jax
import jax.numpy as jnp
from jax import lax
from jax.experimental import pallas as pl
from jax.experimental.pallas import tpu as pltpu

F32 = jnp.float32
BF16 = jnp.bfloat16

D_MODEL = 1024
N_HEADS = 16
HEAD_DIM = 64
N_KV_HEADS = 4
GROUP = N_HEADS // N_KV_HEADS
HEADS_LANES = N_KV_HEADS * HEAD_DIM
PAGED_ROWS = 4 * HEADS_LANES
BLOCK = 64
TOP_N = 16
WINDOW = 512
CONV_WIDTH = 31
CONV_HALO = 32
CMP_PITCH = BLOCK + 8
N_BUCKETS = 32
MAX_EXACT = 16
MAX_DISTANCE = 128
N_EXPERTS = 8
GROUPED_MOE_MIN_TOKENS = 1024
FORCE_SCORE = 1e4
EPS = 1e-6
NEG = -1e30
MASK_BIG = 2.0 ** 100
TINY = 1e-30
LANES = 128
PAGE = 128
VMEM_LIMIT_BYTES = 56 * 1024 * 1024


def _bucket_thresholds():
    d = np.arange(0, 4 * MAX_DISTANCE)
    nf = np.maximum(d, 1).astype(np.float32)
    large = MAX_EXACT + (np.log(nf / np.float32(MAX_EXACT)) / np.float32(math.log(MAX_DISTANCE / MAX_EXACT))
                         * np.float32(N_BUCKETS - MAX_EXACT)).astype(np.int32)
    b = np.where(d < MAX_EXACT, d, np.minimum(large, N_BUCKETS - 1))
    assert np.all(np.diff(b) >= 0)
    return [int(np.min(d[b >= k])) for k in range(N_BUCKETS)]


BUCKET_THR = _bucket_thresholds()
FAR_DIST = BUCKET_THR[-1]


def _cparams(sem):
    return pltpu.CompilerParams(dimension_semantics=sem, vmem_limit_bytes=VMEM_LIMIT_BYTES)


def _sigmoid(x):
    return 1.0 / (1.0 + jnp.exp(-x))


def _silu(x):
    return x * _sigmoid(x)


def _rmsnorm(x, g):
    return x * lax.rsqrt(jnp.mean(x * x, axis=-1, keepdims=True) + EPS) * g


def _dot(a, b):
    return jnp.dot(a, b, preferred_element_type=F32)


def _dot_nt(a, b):
    return lax.dot_general(a, b, (((1,), (1,)), ((), ())), preferred_element_type=F32)


def _rel_bias(dist, rb_ref, head):
    val = jnp.full(dist.shape, rb_ref[0, head], F32)
    for b in range(1, N_BUCKETS):
        val = jnp.where(dist >= BUCKET_THR[b], rb_ref[b, head], val)
    return val


def _rel_bias_heads(dist, rb_ref, heads):
    ge = [dist >= BUCKET_THR[b] for b in range(1, N_BUCKETS)]
    out = []
    for hd in heads:
        val = jnp.full(dist.shape, rb_ref[0, hd], F32)
        for b in range(1, N_BUCKETS):
            val = jnp.where(ge[b - 1], rb_ref[b, hd], val)
        out.append(val)
    return out


def _softmax_parts(s, mask):
    s = jnp.where(mask, s, NEG)
    m = jnp.max(s, axis=-1, keepdims=True)
    p = jnp.where(mask, jnp.exp(s - m), 0.0)
    return p, jnp.maximum(jnp.sum(p, axis=-1, keepdims=True), TINY)


def _online_update(carry, s, vt):
    m, l, acc = carry
    m_new = jnp.maximum(m, jnp.max(s, axis=-1, keepdims=True))
    alpha = jnp.exp(m - m_new)
    p = jnp.exp(s - m_new)
    return m_new, alpha * l + jnp.sum(p, axis=-1, keepdims=True), alpha * acc + _dot_nt(p.astype(BF16), vt)


def _mm_kernel(*refs, n_w, has_norm, has_bias, act, has_res, use_scratch):
    it = iter(refs)
    x_ref = next(it)
    g_ref = next(it) if has_norm else None
    w_refs = [next(it) for _ in range(n_w)]
    b_refs = [next(it) for _ in range(n_w)] if has_bias else []
    r_ref = next(it) if has_res else None
    o_ref = next(it)
    if use_scratch:
        xb_ref = next(it)

        @pl.when(pl.program_id(1) == 0)
        def _():
            x = x_ref[...].astype(F32)
            if has_norm:
                x = _rmsnorm(x, g_ref[...])
            xb_ref[...] = x.astype(BF16)

        xb = xb_ref[...]
    else:
        xb = x_ref[...]
    ys = [_dot(xb, w[...].astype(BF16)) for w in w_refs]
    if has_bias:
        ys = [y + b[...] for y, b in zip(ys, b_refs)]
    if act == "glu":
        y = ys[0] * _sigmoid(ys[1])
    elif act == "swiglu":
        y = _silu(ys[0]) * ys[1]
    else:
        y = ys[0]
    if has_res:
        y = y + r_ref[...]
    o_ref[...] = y.astype(o_ref.dtype)


def _mm(x, ws, n_out, *, tm, tn, name, norm_g=None, biases=None, act=None, residual=None, out_dtype=F32):
    m, k = x.shape
    assert m % tm == 0 and n_out % tn == 0
    use_scratch = norm_g is not None or x.dtype != BF16
    args, specs = [x], [pl.BlockSpec((tm, k), lambda i, j: (i, 0))]
    if norm_g is not None:
        args.append(norm_g.reshape(1, k))
        specs.append(pl.BlockSpec((1, k), lambda i, j: (0, 0)))
    for w, off in ws:
        assert off % tn == 0
        args.append(w)
        specs.append(pl.BlockSpec((k, tn), functools.partial(lambda i, j, o: (0, j + o), o=off // tn)))
    if biases is not None:
        for (bias, off) in biases:
            args.append(bias.reshape(1, -1))
            specs.append(pl.BlockSpec((1, tn), functools.partial(lambda i, j, o: (0, j + o), o=off // tn)))
    if residual is not None:
        args.append(residual)
        specs.append(pl.BlockSpec((tm, tn), lambda i, j: (i, j)))
    kern = functools.partial(_mm_kernel, n_w=len(ws), has_norm=norm_g is not None, has_bias=biases is not None,
                             act=act, has_res=residual is not None, use_scratch=use_scratch)
    return pl.pallas_call(
        kern,
        grid=(m // tm, n_out // tn),
        in_specs=specs,
        out_specs=pl.BlockSpec((tm, tn), lambda i, j: (i, j)),
        out_shape=jax.ShapeDtypeStruct((m, n_out), out_dtype),
        scratch_shapes=[pltpu.VMEM((tm, k), BF16)] if use_scratch else [],
        compiler_params=_cparams(("parallel", "arbitrary")),
        name=name,
    )(*args)


def _mm_t_kernel(x_ref, g_ref, wt_ref, o_ref, xb_ref):
    @pl.when(pl.program_id(1) == 0)
    def _():
        xb_ref[...] = _rmsnorm(x_ref[...], g_ref[...]).astype(BF16)

    o_ref[0] = _dot_nt(wt_ref[...].astype(BF16), xb_ref[...])


def _mm_t(x, wt, norm_g, *, seq, tm, tn, name):
    m, k = x.shape
    n = wt.shape[0]
    assert m % seq == 0 and seq % tm == 0 and n % tn == 0
    per = seq // tm
    return pl.pallas_call(
        _mm_t_kernel,
        grid=(m // tm, n // tn),
        in_specs=[pl.BlockSpec((tm, k), lambda i, j: (i, 0)),
                  pl.BlockSpec((1, k), lambda i, j: (0, 0)),
                  pl.BlockSpec((tn, k), lambda i, j: (j, 0))],
        out_specs=pl.BlockSpec((1, tn, tm), lambda i, j: (i // per, j, i % per)),
        out_shape=jax.ShapeDtypeStruct((m // seq, n, seq), F32),
        scratch_shapes=[pltpu.VMEM((tm, k), BF16)],
        compiler_params=_cparams(("parallel", "arbitrary")),
        name=name,
    )(x, norm_g.reshape(1, k), wt)


def _conv_kernel(main_ref, prev_ref, prefix_ref, wdw_ref, bdw_ref, lng_ref, lnb_ref, o_ref, win_ref, c_ref, *, tt, rc):
    first = pl.program_id(1) == 0
    for c in range(D_MODEL // LANES):
        sl = slice(c * LANES, (c + 1) * LANES)
        win_ref[c, 0:CONV_HALO, :] = jnp.where(first, prefix_ref[0, :, sl], prev_ref[0, :, sl])
        win_ref[c, CONV_HALO:CONV_HALO + tt, :] = main_ref[0, :, sl]

    def row_chunk(r, carry):
        r0 = pl.multiple_of(r * rc, rc)
        for c in range(D_MODEL // LANES):
            sl = slice(c * LANES, (c + 1) * LANES)
            acc = jnp.zeros((rc, LANES), F32)
            for k in range(CONV_WIDTH):
                acc = acc + win_ref[c, pl.ds(r0 + (CONV_HALO - CONV_WIDTH + 1) + k, rc), :] * wdw_ref[k:k + 1, sl]
            c_ref[pl.ds(r0, rc), sl] = acc + bdw_ref[:, sl]
        return carry

    lax.fori_loop(0, tt // rc, row_chunk, 0)
    c = c_ref[...]
    mu = jnp.mean(c, axis=-1, keepdims=True)
    xc = c - mu
    y = xc * lax.rsqrt(jnp.mean(xc * xc, axis=-1, keepdims=True) + EPS)
    y = y * lng_ref[...] + lnb_ref[...]
    o_ref[0] = _silu(y).astype(o_ref.dtype)


def _conv_ln_swish(seq, prefix, tt, rc, wdw, bdw, lng, lnb, name):
    bsz, t, _ = seq.shape
    assert t % tt == 0 and (tt % CONV_HALO == 0 or t == tt)
    vec = pl.BlockSpec((1, D_MODEL), lambda b, i: (0, 0))
    prev = seq if t > tt else prefix
    prev_map = lambda b, i: ((b if prev.shape[0] > 1 else 0), jnp.maximum(i * (tt // CONV_HALO) - 1, 0), 0)
    prefix_map = lambda b, i: ((b if prefix.shape[0] > 1 else 0), 0, 0)
    return pl.pallas_call(
        functools.partial(_conv_kernel, tt=tt, rc=rc),
        grid=(bsz, t // tt),
        in_specs=[pl.BlockSpec((1, tt, D_MODEL), lambda b, i: (b, i, 0)),
                  pl.BlockSpec((1, CONV_HALO, D_MODEL), prev_map),
                  pl.BlockSpec((1, CONV_HALO, D_MODEL), prefix_map),
                  pl.BlockSpec((CONV_WIDTH, D_MODEL), lambda b, i: (0, 0)), vec, vec, vec],
        out_specs=pl.BlockSpec((1, tt, D_MODEL), lambda b, i: (b, i, 0)),
        out_shape=jax.ShapeDtypeStruct((bsz, t, D_MODEL), BF16),
        scratch_shapes=[pltpu.VMEM((D_MODEL // LANES, tt + CONV_HALO, LANES), F32), pltpu.VMEM((tt, D_MODEL), F32)],
        compiler_params=_cparams(("parallel", "parallel")),
        name=name,
    )(seq, prev, prefix, wdw, bdw.reshape(1, -1), lng.reshape(1, -1), lnb.reshape(1, -1))


def _compress_kernel(pt_ref, *refs, n_pages):
    x_refs = refs[:n_pages]
    pe_ref, w1_ref, w2_ref, o_ref, xs_ref = refs[n_pages:]
    hidden = w1_ref.shape[2] // 2
    rows = N_KV_HEADS * n_pages
    for j in range(n_pages):
        x = x_refs[j][0] + pe_ref[...]
        for s in range(2):
            for h in range(N_KV_HEADS):
                r0 = (s * N_KV_HEADS + h) * HEAD_DIM
                xs_ref[s, pl.ds((h * n_pages + j) * CMP_PITCH, HEAD_DIM), :] = x[r0:r0 + HEAD_DIM, :]
    for s in range(2):
        lhs = jnp.concatenate([xs_ref[s, pl.ds(d, rows, stride=CMP_PITCH), :].astype(BF16) for d in range(HEAD_DIM)],
                              axis=1)
        hid = _silu(_dot(lhs, w1_ref[s])).astype(BF16)
        for blk in range(2):
            out = jnp.zeros((n_pages, HEADS_LANES), F32)
            for h in range(N_KV_HEADS):
                out = out + _dot(hid[h * n_pages:(h + 1) * n_pages, blk * hidden:(blk + 1) * hidden],
                                 w2_ref[s, h].astype(BF16))
            c0 = blk * 2 * HEADS_LANES + s * HEADS_LANES
            o_ref[:, c0:c0 + HEADS_LANES] = out


def _compress(src, page_ids, page_index, pe_t, w1t, w2pad, n_pages, name):
    n = page_ids.shape[0]
    assert n % n_pages == 0 and n_pages % 8 == 0
    x_specs = [pl.BlockSpec((1, 2 * HEADS_LANES, PAGE),
                            functools.partial(lambda i, pt, j: page_index(pt[i * n_pages + j]), j=j))
               for j in range(n_pages)]
    grid_spec = pltpu.PrefetchScalarGridSpec(
        num_scalar_prefetch=1,
        grid=(n // n_pages,),
        in_specs=x_specs + [
            pl.BlockSpec((2 * HEADS_LANES, PAGE), lambda i, pt: (0, 0)),
            pl.BlockSpec(w1t.shape, lambda i, pt: (0, 0, 0), pipeline_mode=pl.Buffered(1)),
            pl.BlockSpec(w2pad.shape, lambda i, pt: (0, 0, 0, 0)),
        ],
        out_specs=pl.BlockSpec((n_pages, 4 * HEADS_LANES), lambda i, pt: (i, 0)),
        scratch_shapes=[pltpu.VMEM((2, N_KV_HEADS * n_pages * CMP_PITCH, LANES), F32)],
    )
    out = pl.pallas_call(
        functools.partial(_compress_kernel, n_pages=n_pages),
        grid_spec=grid_spec,
        out_shape=jax.ShapeDtypeStruct((n, 4 * HEADS_LANES), F32),
        compiler_params=_cparams(("parallel",)),
        name=name,
    )(page_ids, *([src] * n_pages), pe_t, w1t, w2pad)
    return out.reshape(2 * n, 2 * HEADS_LANES)


def _head_gate(sig, lane, g, branch, k):
    col = (g * 3 + branch) * N_KV_HEADS + k
    return jnp.sum(jnp.where(lane == col, sig, 0.0), axis=-1, keepdims=True)


def _pattn_kernel(rb_ref, q_ref, kc_ref, vc_ref, ks_ref, vs_ref, kw_ref, vw_ref, o_ref, strips_ref, *scratch,
                  tq, nc):
    qx_ref, p_ref, mv_ref, acc_ref, s_ref, mb_ref, comb_ref = (scratch[n * GROUP:(n + 1) * GROUP] for n in range(7))
    b, i, k = pl.program_id(0), pl.program_id(1), pl.program_id(2)
    rc = lax.broadcasted_iota(jnp.int32, (tq, tq), 0) - lax.broadcasted_iota(jnp.int32, (tq, tq), 1)
    n_back = WINDOW // tq

    @pl.when((b == 0) & (i == 0) & (k == 0))
    def _():
        def head_body(hh, carry):
            head = (hh & 3) * GROUP + (hh >> 2)
            far = rb_ref[N_BUCKETS - 1, head]
            strips_ref[hh * 4] = jnp.where(rc >= 0, _rel_bias(rc, rb_ref, head), -MASK_BIG)
            strips_ref[hh * 4 + 1] = _rel_bias(rc + tq, rb_ref, head)
            strips_ref[hh * 4 + 2] = jnp.full((tq, tq), far, F32)
            strips_ref[hh * 4 + 3] = jnp.where(rc + n_back * tq <= WINDOW, far, -MASK_BIG)
            return carry

        lax.fori_loop(0, N_HEADS, head_body, 0)

    lane = lax.broadcasted_iota(jnp.int32, (1, HEADS_LANES), 1)
    head_mask = (lane // HEAD_DIM) == k
    glane = lax.broadcasted_iota(jnp.int32, (1, LANES), 1)
    sig = _sigmoid(q_ref[0, :, D_MODEL:D_MODEL + LANES])
    kc = kc_ref[0].astype(BF16)
    pick = (lax.broadcasted_iota(jnp.int32, (HEADS_LANES, LANES), 0)
            == k * HEAD_DIM + lax.broadcasted_iota(jnp.int32, (HEADS_LANES, LANES), 1))
    pick = (pick & (lax.broadcasted_iota(jnp.int32, (HEADS_LANES, LANES), 1) < HEAD_DIM)).astype(BF16)
    vc = _dot(vc_ref[0].astype(BF16), pick).astype(BF16)
    eye = (rc == 0).astype(BF16)
    qs = [jnp.where(head_mask, q_ref[0, :, g * HEADS_LANES:(g + 1) * HEADS_LANES] * (HEAD_DIM ** -0.5), 0.0)
          .astype(BF16) for g in range(GROUP)]

    qpos_t = i * tq + lax.broadcasted_iota(jnp.int32, (1, tq), 1)
    blk_t = lax.broadcasted_iota(jnp.int32, (nc, 1), 0)
    cmp_end_t = blk_t * BLOCK + (BLOCK - 1)
    mask_c = cmp_end_t <= qpos_t
    bias_c = _rel_bias_heads(qpos_t - cmp_end_t, rb_ref, [k * GROUP + g for g in range(GROUP)])
    imp = jnp.zeros((nc, tq), F32)
    for g in range(GROUP):
        s = jnp.where(mask_c, _dot_nt(kc, qs[g]) + bias_c[g], NEG)
        m = jnp.max(s, axis=0, keepdims=True)
        p = jnp.where(mask_c, jnp.exp(s - m), 0.0)
        p = p / jnp.maximum(jnp.sum(p, axis=0, keepdims=True), TINY)
        imp = imp + p
        p_rows = _dot_nt(eye, p.astype(BF16)).astype(BF16)
        comb_ref[g][...] = _head_gate(sig, glane, g, 0, k) * _dot(p_rows, vc)

    cur_t = qpos_t // BLOCK
    forced = (blk_t == cur_t) | (blk_t == cur_t - 1) | (blk_t == 0)
    score = jnp.where(blk_t > cur_t, NEG, jnp.where(forced, FORCE_SCORE, imp))
    sel_t = jnp.zeros((nc, tq), F32)
    for j in range(nc):
        row = score[j:j + 1, :]
        beats = (score > row) | ((score == row) & (blk_t < j))
        rank = jnp.sum(beats.astype(F32), axis=0, keepdims=True)
        sel_t = jnp.where(blk_t == j, (rank < TOP_N).astype(F32), sel_t)
    sel = _dot_nt(eye, sel_t.astype(BF16)).astype(BF16)

    kx = (k + 1) % N_KV_HEADS
    ones_row = (lax.broadcasted_iota(jnp.int32, (LANES - HEAD_DIM, tq), 0) == 0).astype(F32)
    place = (lax.broadcasted_iota(jnp.int32, (nc, HEADS_LANES), 1)
             == kx * HEAD_DIM + lax.broadcasted_iota(jnp.int32, (nc, HEADS_LANES), 0)).astype(BF16)
    unsel_lanes = _dot(1.0 - sel, place).astype(BF16)
    row = lax.broadcasted_iota(jnp.int32, (HEADS_LANES, 1), 0)
    slot_blk = row - kx * HEAD_DIM
    in_slot = (slot_blk >= 0) & (slot_blk < nc)
    key_blk = lax.broadcasted_iota(jnp.int32, (1, tq), 1) // BLOCK

    def tile(ref, j):
        return ref[0, :, pl.ds(pl.multiple_of(j * tq, tq), tq)]

    def k_sel(j):
        mask_rows = jnp.where(slot_blk == j * (tq // BLOCK) + key_blk, -MASK_BIG, 0.0)
        return jnp.where(in_slot, mask_rows, tile(ks_ref, j)).astype(BF16)

    def k_win(j):
        return tile(kw_ref, j).astype(BF16)

    def scores(g, kt, sidx):
        return _dot(qx_ref[g][...], kt) + strips_ref[(g * N_KV_HEADS + k) * 4 + sidx]

    def run_branch(queries, n_visits, k_tile, v_ref, strip_index, branch):
        for g in range(GROUP):
            qx_ref[g][...] = queries[g]
            mv_ref[g][...] = jnp.full((tq, tq // 2), NEG, F32)
            acc_ref[g][...] = jnp.zeros((tq, LANES), F32)

        def max_visit(jj, carry):
            kt, sidx = k_tile(i - jj), strip_index(jj)
            for g in range(GROUP):
                s = scores(g, kt, sidx)
                s_ref[g][jj] = s
                mv_ref[g][...] = jnp.maximum(mv_ref[g][...], jnp.maximum(s[:, :tq // 2], s[:, tq // 2:]))
            return carry

        lax.fori_loop(0, n_visits, max_visit, 0)
        for g in range(GROUP):
            mb_ref[g][...] = jnp.broadcast_to(jnp.max(mv_ref[g][...], axis=-1, keepdims=True), (tq, tq))

        def acc_visit(jj, carry):
            v_head = v_ref[0, pl.ds(pl.multiple_of(k * HEAD_DIM, HEAD_DIM), HEAD_DIM),
                           pl.ds(pl.multiple_of((i - jj) * tq, tq), tq)]
            vt = jnp.concatenate([v_head, ones_row], axis=0).astype(BF16)
            for g in range(GROUP):
                p_ref[g][...] = jnp.exp(s_ref[g][jj] - mb_ref[g][...]).astype(BF16)
            for g in range(GROUP):
                acc_ref[g][...] += _dot_nt(p_ref[g][...], vt)
            return carry

        lax.fori_loop(0, n_visits, acc_visit, 0)
        for g in range(GROUP):
            acc = acc_ref[g][...]
            scale = _head_gate(sig, glane, g, branch, k) / jnp.maximum(acc[:, HEAD_DIM:HEAD_DIM + 1], TINY)
            comb_ref[g][...] += scale * acc

    run_branch([qs[g] + unsel_lanes for g in range(GROUP)], i + 1, k_sel, vs_ref,
               lambda jj: jnp.minimum(jj, 2), 1)
    run_branch(qs, jnp.minimum(i, n_back) + 1, k_win, vw_ref,
               lambda jj: jnp.where(jj == n_back, 3, jnp.minimum(jj, 2)), 2)
    o_ref[0] = jnp.concatenate([comb_ref[g][:, 0:HEAD_DIM] for g in range(GROUP)], axis=1)


def _prompt_attention(rel_bias, proj, cmp_kv, paged_t, win_t, *, tq):
    bsz, t, _ = proj.shape
    nc = cmp_kv.shape[1]
    assert t % tq == 0 and tq % BLOCK == 0 and WINDOW == 2 * tq and nc * BLOCK == t
    assert tq + 1 >= FAR_DIST

    def rows(idx, n):
        return pl.BlockSpec((1, n, HEADS_LANES), functools.partial(lambda b, i, k, c: (b, 0, c), c=idx))

    def cols(idx):
        return pl.BlockSpec((1, HEADS_LANES, t), functools.partial(lambda b, i, k, c: (b, c, 0), c=idx))

    return pl.pallas_call(
        functools.partial(_pattn_kernel, tq=tq, nc=nc),
        grid=(bsz, t // tq, N_KV_HEADS),
        in_specs=[pl.BlockSpec(memory_space=pltpu.SMEM),
                  pl.BlockSpec((1, tq, proj.shape[2]), lambda b, i, k: (b, i, 0)),
                  rows(0, nc), rows(1, nc), cols(2), cols(3), cols(0), cols(1)],
        out_specs=pl.BlockSpec((1, tq, HEADS_LANES), lambda b, i, k: (b, i, k)),
        out_shape=jax.ShapeDtypeStruct((bsz, t, D_MODEL), F32),
        scratch_shapes=[pltpu.VMEM((4 * N_HEADS, tq, tq), F32)]
        + [pltpu.VMEM(shape, dtype) for shape, dtype in (
            ((tq, HEADS_LANES), BF16), ((tq, tq), BF16), ((tq, tq // 2), F32), ((tq, LANES), F32),
            ((t // tq, tq, tq), F32), ((tq, tq), F32), ((tq, LANES), F32)) for _ in range(GROUP)],
        compiler_params=_cparams(("arbitrary", "arbitrary", "arbitrary")),
        name="prompt_attention",
    )(rel_bias, proj, cmp_kv, cmp_kv, paged_t, paged_t, win_t, win_t)


SROWS = 8
ALL_ROWS = N_HEADS * SROWS


def _all_head_queries(q_ref):
    lane = lax.broadcasted_iota(jnp.int32, (1, HEADS_LANES), 1)
    parts = []
    for k in range(N_KV_HEADS):
        for g in range(GROUP):
            qg = q_ref[0, :, g * HEADS_LANES:(g + 1) * HEADS_LANES] * (HEAD_DIM ** -0.5)
            parts.append(jnp.where((lane // HEAD_DIM) == k, qg, 0.0))
    return jnp.concatenate(parts, axis=0).astype(BF16)


def _all_head_bias(dist, rb_ref):
    return jnp.concatenate(_rel_bias_heads(dist, rb_ref, list(range(N_HEADS))), axis=0)


def _sattn_a_kernel(rb_ref, q_ref, cmp_ref, win_ref, part_ref, sel_ref, *, pos0, ts, wp, nc):
    lane = lax.broadcasted_iota(jnp.int32, (1, HEADS_LANES), 1)
    sig = _sigmoid(q_ref[0, :, D_MODEL:D_MODEL + LANES])
    q = _all_head_queries(q_ref)
    qpos = pos0 + lax.broadcasted_iota(jnp.int32, (SROWS, 1), 0)
    tile_rows = lambda a: jnp.concatenate([a] * N_HEADS, axis=0)

    blk = lax.broadcasted_iota(jnp.int32, (1, nc), 1)
    cmp_end = blk * BLOCK + (BLOCK - 1)
    s = _dot_nt(q, cmp_ref[0, :, 0:HEADS_LANES].astype(BF16)) + _all_head_bias(qpos - cmp_end, rb_ref)
    p, l = _softmax_parts(s, tile_rows(cmp_end <= qpos))
    p = p / l
    o_c = _dot(p.astype(BF16), cmp_ref[0, :, HEADS_LANES:2 * HEADS_LANES].astype(BF16))

    imp = jnp.concatenate(
        [sum(p[(k * GROUP + g) * SROWS:(k * GROUP + g + 1) * SROWS] for g in range(GROUP))
         for k in range(N_KV_HEADS)], axis=0)
    cur = jnp.concatenate([qpos // BLOCK] * N_KV_HEADS, axis=0)
    forced = (blk == cur) | (blk == cur - 1) | (blk == 0)
    score = jnp.where(blk > cur, NEG, jnp.where(forced, FORCE_SCORE, imp))
    sel = jnp.zeros(score.shape, F32)
    for _ in range(TOP_N - 1):
        m = jnp.max(score, axis=-1, keepdims=True)
        idx = jnp.min(jnp.where(score == m, blk, nc), axis=-1, keepdims=True)
        hit = blk == idx
        sel = jnp.where(hit, 1.0, sel)
        score = jnp.where(hit, -jnp.inf, score)
    for k in range(N_KV_HEADS):
        sel_ref[0, :, k * nc:(k + 1) * nc] = sel[k * SROWS:(k + 1) * SROWS]

    wk = win_ref.shape[2]
    j = lax.broadcasted_iota(jnp.int32, (1, wk), 1)
    kw_pos = pos0 - wp + j
    dist = qpos - kw_pos
    mask = (dist >= 0) & (dist <= WINDOW) & (kw_pos >= 0) & (j < wp + ts)
    s = _dot(q, win_ref[0, 0:HEADS_LANES, :].astype(BF16)) + _all_head_bias(dist, rb_ref)
    p, l = _softmax_parts(s, tile_rows(mask))
    o_w = _dot_nt((p / l).astype(BF16), win_ref[0, HEADS_LANES:2 * HEADS_LANES, :].astype(BF16))

    for g in range(GROUP):
        chunk = jnp.zeros((SROWS, HEADS_LANES), F32)
        for k in range(N_KV_HEADS):
            r = slice((k * GROUP + g) * SROWS, (k * GROUP + g + 1) * SROWS)
            c_cmp, c_win = (g * 3 + 0) * N_KV_HEADS + k, (g * 3 + 2) * N_KV_HEADS + k
            comb = sig[:, c_cmp:c_cmp + 1] * o_c[r] + sig[:, c_win:c_win + 1] * o_w[r]
            chunk = chunk + jnp.where((lane // HEAD_DIM) == k, comb, 0.0)
        part_ref[0, :, g * HEADS_LANES:(g + 1) * HEADS_LANES] = chunk


def _sample_attention_a(rel_bias, q, cmp_kv, win_t, *, pos0, ts, wp):
    bsz = q.shape[0]
    nc = cmp_kv.shape[1]
    assert pos0 % BLOCK == 0 and ts < BLOCK and pos0 // BLOCK == nc and ts <= SROWS and TOP_N >= 3
    whole = lambda arr: pl.BlockSpec((1,) + arr.shape[1:], lambda b: (b, 0, 0))
    return pl.pallas_call(
        functools.partial(_sattn_a_kernel, pos0=pos0, ts=ts, wp=wp, nc=nc),
        grid=(bsz,),
        in_specs=[pl.BlockSpec(memory_space=pltpu.SMEM), whole(q), whole(cmp_kv), whole(win_t)],
        out_specs=[pl.BlockSpec((1, SROWS, D_MODEL), lambda b: (b, 0, 0)),
                   pl.BlockSpec((1, SROWS, N_KV_HEADS * nc), lambda b: (b, 0, 0))],
        out_shape=[jax.ShapeDtypeStruct((bsz, SROWS, D_MODEL), F32),
                   jax.ShapeDtypeStruct((bsz, SROWS, N_KV_HEADS * nc), F32)],
        compiler_params=_cparams(("parallel",)),
        name="sample_attention_a",
    )(rel_bias, q, cmp_kv, win_t)


def _sattn_b_kernel(pt_ref, rb_ref, q_ref, sel_ref, part_ref, knew_ref, *refs, pos0, ts, nc, n_pg, n_seq):
    k_refs = [refs[s * n_pg:(s + 1) * n_pg] for s in range(n_seq)]
    v_refs = [refs[(n_seq + s) * n_pg:(n_seq + s + 1) * n_pg] for s in range(n_seq)]
    o_ref, expand_ref = refs[2 * n_seq * n_pg:2 * n_seq * n_pg + 2]
    state = refs[2 * n_seq * n_pg + 2:]
    p_idx = pl.program_id(1)
    keys = n_pg * PAGE
    blocks = keys // BLOCK
    n_pat = LANES // blocks
    qpos = pos0 + lax.broadcasted_iota(jnp.int32, (SROWS, 1), 0)
    col = lax.broadcasted_iota(jnp.int32, (1, keys), 1)

    @pl.when((pl.program_id(0) == 0) & (p_idx == 0))
    def _():
        b_row = lax.broadcasted_iota(jnp.int32, (LANES, 1), 0)
        for o in range(n_pat):
            expand_ref[o] = jnp.where(b_row == o * blocks + col // BLOCK, -MASK_BIG, 0.0).astype(BF16)

    def near_bias(dist):
        return jnp.concatenate(
            [b - rb_ref[N_BUCKETS - 1, h] for h, b in enumerate(_rel_bias_heads(dist, rb_ref, list(range(N_HEADS))))],
            axis=0)

    page0 = p_idx * n_pg
    last = p_idx == pl.num_programs(1) - 1
    window = pl.multiple_of((2 * page0) // LANES * LANES, LANES)
    expand = expand_ref[((2 * page0) % LANES) // blocks]
    seq_state = [state[5 * s:5 * s + 5] for s in range(n_seq)]

    def update(s, scores, vt):
        _, _, m_ref, l_ref, acc_ref = seq_state[s]
        m, l, acc = _online_update((m_ref[...], l_ref[...], acc_ref[...]), scores, vt)
        m_ref[...], l_ref[...], acc_ref[...] = m, l, acc

    def page_scores(s):
        qall_ref, unsel_ref = seq_state[s][:2]
        kt = jnp.concatenate([r[0] for r in k_refs[s]], axis=1).astype(BF16)
        vt = jnp.concatenate([r[0] for r in v_refs[s]], axis=1).astype(BF16)
        return _dot(qall_ref[...], kt) + _dot(unsel_ref[:, pl.ds(window, LANES)].astype(BF16), expand), vt

    @pl.when(p_idx == 0)
    def _():
        for s in range(n_seq):
            qall_ref, unsel_ref, m_ref, l_ref, acc_ref = seq_state[s]
            qall_ref[...] = _all_head_queries(q_ref.at[pl.ds(s, 1)])
            for k in range(N_KV_HEADS):
                for g in range(GROUP):
                    r = slice((k * GROUP + g) * SROWS, (k * GROUP + g + 1) * SROWS)
                    unsel_ref[r, :] = 1.0 - sel_ref[s, :, k * nc:(k + 1) * nc]
            m_ref[...] = jnp.full(m_ref.shape, NEG, F32)
            l_ref[...] = jnp.zeros(l_ref.shape, F32)
            acc_ref[...] = jnp.zeros(acc_ref.shape, F32)

    for s in range(n_seq):
        base, vt = page_scores(s)
        update(s, base + jnp.where(last, -MASK_BIG, 0.0), vt)

    @pl.when(last)
    def _():
        for s in range(n_seq):
            qall_ref, _, _, l_ref, acc_ref = seq_state[s]
            base, vt = page_scores(s)
            update(s, base + near_bias(qpos - (page0 * PAGE + col)), vt)
            ncol = lax.broadcasted_iota(jnp.int32, (1, PAGE), 1)
            dist = qpos - (pos0 + ncol)
            add = near_bias(dist) + jnp.concatenate(
                [jnp.where((dist >= 0) & (ncol < ts), 0.0, -MASK_BIG)] * N_HEADS, axis=0)
            update(s, _dot(qall_ref[...], knew_ref[s, 0:HEADS_LANES, :].astype(BF16)) + add,
                   knew_ref[s, HEADS_LANES:2 * HEADS_LANES, :].astype(BF16))
            o_sel = acc_ref[...] / jnp.maximum(l_ref[...], TINY)
            sig = _sigmoid(q_ref[s, :, D_MODEL:D_MODEL + LANES])
            lane = lax.broadcasted_iota(jnp.int32, (1, HEADS_LANES), 1)
            for g in range(GROUP):
                sl = slice(g * HEADS_LANES, (g + 1) * HEADS_LANES)
                chunk = part_ref[s, :, sl]
                for k in range(N_KV_HEADS):
                    r = slice((k * GROUP + g) * SROWS, (k * GROUP + g + 1) * SROWS)
                    c = (g * 3 + 1) * N_KV_HEADS + k
                    chunk = chunk + jnp.where((lane // HEAD_DIM) == k, sig[:, c:c + 1] * o_sel[r], 0.0)
                o_ref[s, :, sl] = chunk


def _sample_attention_b(rel_bias, page_ids, q, sel, part, knew_t, cache_t, *, pos0, ts, n_pg):
    bsz = q.shape[0]
    n_pages = page_ids.shape[0] // bsz
    nc = sel.shape[2] // N_KV_HEADS
    n_seq = 2 if bsz % 2 == 0 else 1
    keys = n_pg * PAGE
    assert n_pages % n_pg == 0 and n_pages * 2 == nc and pos0 == n_pages * PAGE and TOP_N >= 3
    assert nc % LANES == 0 and LANES % (keys // BLOCK) == 0 and keys >= FAR_DIST

    def page_spec(slot, s, j):
        return pl.BlockSpec(
            (1, HEADS_LANES, PAGE),
            functools.partial(lambda b, p, pt, s, j, c: (pt[(b * n_seq + s) * n_pages + p * n_pg + j], c, 0),
                              s=s, j=j, c=slot))

    whole = lambda arr: pl.BlockSpec((n_seq,) + arr.shape[1:], lambda b, p, pt: (b, 0, 0))
    pages = [page_spec(slot, s, j) for slot in (2, 3) for s in range(n_seq) for j in range(n_pg)]
    per_seq = [pltpu.VMEM((ALL_ROWS, HEADS_LANES), BF16), pltpu.VMEM((ALL_ROWS, nc), F32),
               pltpu.VMEM((ALL_ROWS, 1), F32), pltpu.VMEM((ALL_ROWS, 1), F32),
               pltpu.VMEM((ALL_ROWS, HEADS_LANES), F32)]
    grid_spec = pltpu.PrefetchScalarGridSpec(
        num_scalar_prefetch=1,
        grid=(bsz // n_seq, n_pages // n_pg),
        in_specs=[pl.BlockSpec(memory_space=pltpu.SMEM), whole(q), whole(sel), whole(part), whole(knew_t)] + pages,
        out_specs=pl.BlockSpec((n_seq, SROWS, D_MODEL), lambda b, p, pt: (b, 0, 0)),
        scratch_shapes=[pltpu.VMEM((LANES // (keys // BLOCK), LANES, keys), BF16)] + per_seq * n_seq,
    )
    return pl.pallas_call(
        functools.partial(_sattn_b_kernel, pos0=pos0, ts=ts, nc=nc, n_pg=n_pg, n_seq=n_seq),
        grid_spec=grid_spec,
        out_shape=jax.ShapeDtypeStruct((bsz, SROWS, D_MODEL), F32),
        compiler_params=_cparams(("arbitrary", "arbitrary")),
        name="sample_attention_b",
    )(page_ids, rel_bias, q, sel, part, knew_t, *([cache_t] * (2 * n_seq * n_pg)))


def _router_kernel(h_ref, g_ref, wr_ref, xn_ref, comb_ref):
    xn = _rmsnorm(h_ref[...], g_ref[...]).astype(BF16)
    xn_ref[...] = xn
    logits = _dot(xn, wr_ref[...].astype(BF16))
    lane = lax.broadcasted_iota(jnp.int32, logits.shape, 1)
    lg = jnp.where(lane < N_EXPERTS, logits, -jnp.inf)
    m1 = jnp.max(lg, axis=-1, keepdims=True)
    i1 = jnp.min(jnp.where(lg == m1, lane, LANES), axis=-1, keepdims=True)
    lg2 = jnp.where(lane == i1, -jnp.inf, lg)
    m2 = jnp.max(lg2, axis=-1, keepdims=True)
    i2 = jnp.min(jnp.where(lg2 == m2, lane, LANES), axis=-1, keepdims=True)
    e = jnp.exp(m2 - m1)
    comb_ref[...] = jnp.where(lane == i1, 1.0 / (1.0 + e), 0.0) + jnp.where(lane == i2, e / (1.0 + e), 0.0)


def _router(h, g, w_router_pad, tm, name):
    m = h.shape[0]
    return pl.pallas_call(
        _router_kernel,
        grid=(m // tm,),
        in_specs=[pl.BlockSpec((tm, D_MODEL), lambda i: (i, 0)),
                  pl.BlockSpec((1, D_MODEL), lambda i: (0, 0)),
                  pl.BlockSpec((D_MODEL, LANES), lambda i: (0, 0))],
        out_specs=[pl.BlockSpec((tm, D_MODEL), lambda i: (i, 0)), pl.BlockSpec((tm, LANES), lambda i: (i, 0))],
        out_shape=[jax.ShapeDtypeStruct((m, D_MODEL), BF16), jax.ShapeDtypeStruct((m, LANES), F32)],
        compiler_params=_cparams(("parallel",)),
        name=name,
    )(h, g.reshape(1, -1), w_router_pad)


def _moe_kernel(xn_ref, comb_ref, h_ref, wg_ref, wu_ref, wd_ref, gf_ref, y_ref, acc_ref):
    e, f = pl.program_id(1), pl.program_id(2)
    last_e, last_f = pl.num_programs(1) - 1, pl.num_programs(2) - 1

    @pl.when((e == 0) & (f == 0))
    def _():
        y_ref[...] = h_ref[...]

    @pl.when(f == 0)
    def _():
        acc_ref[...] = jnp.zeros(acc_ref.shape, F32)

    x = xn_ref[...]
    hid = _silu(_dot(x, wg_ref[0].astype(BF16))) * _dot(x, wu_ref[0].astype(BF16))
    acc_ref[...] += _dot(hid.astype(BF16), wd_ref[0].astype(BF16))

    @pl.when(f == last_f)
    def _():
        lane = lax.broadcasted_iota(jnp.int32, comb_ref.shape, 1)
        ce = jnp.sum(jnp.where(lane == e, comb_ref[...], 0.0), axis=-1, keepdims=True)
        y_ref[...] += ce * acc_ref[...]

    @pl.when((e == last_e) & (f == last_f))
    def _():
        y_ref[...] = _rmsnorm(y_ref[...], gf_ref[...])


def _moe(xn, comb, h, wg, wu, wd, g_final, tm, tf, name):
    m = h.shape[0]
    n_e, _, d_ff = wg.shape
    assert m % tm == 0 and d_ff % tf == 0
    row = lambda i, e, f: (i, 0)
    return pl.pallas_call(
        _moe_kernel,
        grid=(m // tm, n_e, d_ff // tf),
        in_specs=[pl.BlockSpec((tm, D_MODEL), row), pl.BlockSpec((tm, LANES), row), pl.BlockSpec((tm, D_MODEL), row),
                  pl.BlockSpec((1, D_MODEL, tf), lambda i, e, f: (e, 0, f)),
                  pl.BlockSpec((1, D_MODEL, tf), lambda i, e, f: (e, 0, f)),
                  pl.BlockSpec((1, tf, D_MODEL), lambda i, e, f: (e, f, 0)),
                  pl.BlockSpec((1, D_MODEL), lambda i, e, f: (0, 0))],
        out_specs=pl.BlockSpec((tm, D_MODEL), row),
        out_shape=jax.ShapeDtypeStruct((m, D_MODEL), F32),
        scratch_shapes=[pltpu.VMEM((tm, D_MODEL), F32)],
        compiler_params=_cparams(("parallel", "arbitrary", "arbitrary")),
        name=name,
    )(xn, comb, h, wg, wu, wd, g_final.reshape(1, -1))


def _route_kernel(h_ref, g_ref, wr_ref, xn_ref, idx_ref, w_ref):
    xn = _rmsnorm(h_ref[...], g_ref[...])
    xn_ref[...] = xn
    logits = _dot(xn.astype(BF16), wr_ref[...].astype(BF16))
    lane = lax.broadcasted_iota(jnp.int32, logits.shape, 1)
    lg = jnp.where(lane < N_EXPERTS, logits, -jnp.inf)
    m1 = jnp.max(lg, axis=-1, keepdims=True)
    i1 = jnp.min(jnp.where(lg == m1, lane, LANES), axis=-1, keepdims=True)
    lg2 = jnp.where(lane == i1, -jnp.inf, lg)
    m2 = jnp.max(lg2, axis=-1, keepdims=True)
    i2 = jnp.min(jnp.where(lg2 == m2, lane, LANES), axis=-1, keepdims=True)
    e = jnp.exp(m2 - m1)
    idx_ref[...] = jnp.where(lane == 0, i1, jnp.where(lane == 1, i2, 0))
    w_ref[...] = jnp.where(lane == 0, 1.0 / (1.0 + e), jnp.where(lane == 1, e / (1.0 + e), 0.0))


def _route(h, g, w_router_pad, tm, name):
    m = h.shape[0]
    row = lambda n: pl.BlockSpec((tm, n), lambda i: (i, 0))
    return pl.pallas_call(
        _route_kernel,
        grid=(m // tm,),
        in_specs=[row(D_MODEL), pl.BlockSpec((1, D_MODEL), lambda i: (0, 0)),
                  pl.BlockSpec((D_MODEL, LANES), lambda i: (0, 0))],
        out_specs=[row(D_MODEL), row(LANES), row(LANES)],
        out_shape=[jax.ShapeDtypeStruct((m, D_MODEL), F32), jax.ShapeDtypeStruct((m, LANES), jnp.int32),
                   jax.ShapeDtypeStruct((m, LANES), F32)],
        compiler_params=_cparams(("parallel",)),
        name=name,
    )(h, g.reshape(1, -1), w_router_pad)


ROW_GATHER_STEP = 4096


def _gather_rows_kernel(idx_ref, src_ref, o_ref, sem, *, rows, parts):
    base = pl.program_id(0) * rows
    width = src_ref.shape[1]

    def copies(t, src_rows):
        return [pltpu.make_async_copy(src_ref.at[pl.ds(src_rows[h], 1)],
                                      o_ref.at[pl.ds(t, 1), pl.ds(h * width, width)], sem) for h in range(parts)]

    def issue(t, carry):
        for c in copies(t, [idx_ref[base + parts * t + h] for h in range(parts)]):
            c.start()
        return carry

    def drain(t, carry):
        for c in copies(t, [0] * parts):
            c.wait()
        return carry

    lax.fori_loop(0, rows // parts, issue, 0, unroll=8)
    lax.fori_loop(0, rows // parts, drain, 0, unroll=8)


def _gather_rows(src, idx, parts, name):
    n = idx.shape[0]
    rows = _row_tile(n, ROW_GATHER_STEP)
    width = src.shape[1]
    grid_spec = pltpu.PrefetchScalarGridSpec(
        num_scalar_prefetch=1,
        grid=(n // rows,),
        in_specs=[pl.BlockSpec(memory_space=pl.ANY)],
        out_specs=pl.BlockSpec((rows // parts, parts * width), lambda i, idx_ref: (i, 0)),
        scratch_shapes=[pltpu.SemaphoreType.DMA(())],
    )
    return pl.pallas_call(
        functools.partial(_gather_rows_kernel, rows=rows, parts=parts),
        grid_spec=grid_spec,
        out_shape=jax.ShapeDtypeStruct((n // parts, parts * width), src.dtype),
        compiler_params=_cparams(("arbitrary",)),
        name=name,
    )(idx, src)


def _experts_kernel(te_ref, tv_ref, x_ref, wg_ref, wu_ref, wd_ref, o_ref, *, tf):
    i = pl.program_id(0)

    @pl.when(tv_ref[i] == 0)
    def _():
        o_ref[...] = jnp.zeros(o_ref.shape, F32)

    @pl.when(tv_ref[i] != 0)
    def _():
        x = x_ref[...].astype(BF16)
        acc = jnp.zeros(o_ref.shape, F32)
        for f in range(wg_ref.shape[2] // tf):
            sl = slice(f * tf, (f + 1) * tf)
            hid = _silu(_dot(x, wg_ref[0, :, sl].astype(BF16))) * _dot(x, wu_ref[0, :, sl].astype(BF16))
            acc = acc + _dot(hid.astype(BF16), wd_ref[0, sl, :].astype(BF16))
        o_ref[...] = acc


def _experts(xg, tile_expert, tile_valid, wg, wu, wd, rows, tf, name):
    n = xg.shape[0]
    n_e, _, d_ff = wg.shape
    assert n % rows == 0 and d_ff % tf == 0
    once = pl.Buffered(1)
    grid_spec = pltpu.PrefetchScalarGridSpec(
        num_scalar_prefetch=2,
        grid=(n // rows,),
        in_specs=[pl.BlockSpec((rows, D_MODEL), lambda i, te, tv: (i, 0)),
                  pl.BlockSpec((1, D_MODEL, d_ff), lambda i, te, tv: (te[i], 0, 0), pipeline_mode=once),
                  pl.BlockSpec((1, D_MODEL, d_ff), lambda i, te, tv: (te[i], 0, 0), pipeline_mode=once),
                  pl.BlockSpec((1, d_ff, D_MODEL), lambda i, te, tv: (te[i], 0, 0), pipeline_mode=once)],
        out_specs=pl.BlockSpec((rows, D_MODEL), lambda i, te, tv: (i, 0)),
    )
    return pl.pallas_call(
        functools.partial(_experts_kernel, tf=tf),
        grid_spec=grid_spec,
        out_shape=jax.ShapeDtypeStruct((n, D_MODEL), F32),
        compiler_params=_cparams(("arbitrary",)),
        name=name,
    )(tile_expert, tile_valid, xg, wg, wu, wd)


def _moe_combine_kernel(h_ref, y2_ref, w_ref, gf_ref, o_ref):
    w = w_ref[...]
    y = h_ref[...] + w[:, 0:1] * y2_ref[:, 0:D_MODEL] + w[:, 1:2] * y2_ref[:, D_MODEL:2 * D_MODEL]
    o_ref[...] = _rmsnorm(y, gf_ref[...])


def _moe_combine(h, y2, w, g_final, tm, name):
    m = h.shape[0]
    row = lambda n: pl.BlockSpec((tm, n), lambda i: (i, 0))
    return pl.pallas_call(
        _moe_combine_kernel,
        grid=(m // tm,),
        in_specs=[row(D_MODEL), row(2 * D_MODEL), row(LANES), pl.BlockSpec((1, D_MODEL), lambda i: (0, 0))],
        out_specs=row(D_MODEL),
        out_shape=jax.ShapeDtypeStruct((m, D_MODEL), F32),
        compiler_params=_cparams(("parallel",)),
        name=name,
    )(h, y2, w, g_final.reshape(1, -1))


def _moe_grouped(h, p, tag, rows=512):
    m = h.shape[0]
    tm = _row_tile(m, 1024)
    xn, idx, w = _route(h, p["norm_ffn"][1], p["w_router_pad"], tm, "route_" + tag)
    n_e = p["moe_w_gate"].shape[1]
    expert = idx[:, :2].reshape(-1)
    onehot = (expert[:, None] == jnp.arange(n_e, dtype=jnp.int32)[None, :]).astype(jnp.int32)
    pos = jnp.sum((jnp.cumsum(onehot, axis=0) - 1) * onehot, axis=1)
    counts = jnp.sum(onehot, axis=0)
    padded = (counts + rows - 1) // rows * rows
    ends = jnp.cumsum(padded)
    row_of = (ends - padded)[expert] + pos
    n_tiles = (2 * m + n_e * (rows - 1)) // rows + 1
    src = (jnp.arange(n_tiles * rows, dtype=jnp.int32) % m).at[row_of].set(jnp.arange(2 * m, dtype=jnp.int32) // 2)
    tile_start = jnp.arange(n_tiles, dtype=jnp.int32) * rows
    tile_expert = jnp.minimum(jnp.sum((tile_start[:, None] >= ends[None, :]).astype(jnp.int32), axis=1), n_e - 1)
    tile_valid = (tile_start < ends[-1]).astype(jnp.int32)
    xg = _gather_rows(xn, src, 1, "moe_gather_" + tag)
    yg = _experts(xg, tile_expert, tile_valid, p["moe_w_gate"][0], p["moe_w_up"][0], p["moe_w_down"][0], rows, 256,
                  "moe_experts_" + tag)
    y2 = _gather_rows(yg, row_of, 2, "moe_ungather_" + tag)
    return _moe_combine(h, y2, w, p["norm_final"], tm, "moe_combine_" + tag)


def _compress_params(cmp_pos, cmp_w1, cmp_w2):
    pe = jnp.transpose(cmp_pos, (0, 2, 1))[:, None, :, :]
    pe = jnp.broadcast_to(pe, (2, N_KV_HEADS, HEAD_DIM, BLOCK))
    pe_t = jnp.concatenate([pe, pe], axis=-1).reshape(2 * HEADS_LANES, PAGE)
    hidden = cmp_w1.shape[2]
    w1_dp = cmp_w1.reshape(2, BLOCK, HEAD_DIM, hidden).transpose(0, 2, 1, 3)
    zeros = jnp.zeros_like(w1_dp)
    w1t = jnp.stack([jnp.concatenate([w1_dp, zeros], axis=-1), jnp.concatenate([zeros, w1_dp], axis=-1)], axis=2)
    w1t = w1t.reshape(2, HEAD_DIM * PAGE, 2 * hidden).astype(BF16)
    w2pad = jnp.zeros((2, N_KV_HEADS, hidden, HEADS_LANES), F32)
    for h in range(N_KV_HEADS):
        w2pad = w2pad.at[:, h, :, h * HEAD_DIM:(h + 1) * HEAD_DIM].set(cmp_w2)
    return pe_t, w1t, w2pad


def _attn_weights(w_in, w_out):
    hd = N_HEADS * HEAD_DIM
    wq = w_in[:, :hd].reshape(D_MODEL, N_KV_HEADS, GROUP, HEAD_DIM).transpose(0, 2, 1, 3).reshape(D_MODEL, hd)
    wg = w_in[:, hd:].reshape(D_MODEL, N_KV_HEADS, GROUP, 3).transpose(0, 2, 3, 1).reshape(D_MODEL, 3 * N_HEADS)
    w_in_pad = jnp.concatenate([wq, wg, jnp.zeros((D_MODEL, LANES - 3 * N_HEADS), F32)], axis=1)
    w_out_perm = w_out.reshape(N_KV_HEADS, GROUP, HEAD_DIM, D_MODEL).transpose(1, 0, 2, 3).reshape(hd, D_MODEL)
    return w_in_pad, w_out_perm


def _row_tile(m, cap):
    t = cap
    while m % t:
        t //= 2
    return t


def _from_rows_t(a_t, n_slots):
    bsz, _, t = a_t.shape
    return a_t.reshape(bsz, n_slots, N_KV_HEADS, HEAD_DIM, t).transpose(0, 4, 1, 2, 3)


def _to_rows_t(a):
    bsz, t, n_slots = a.shape[:3]
    return a.transpose(0, 2, 3, 4, 1).reshape(bsz, n_slots * HEADS_LANES, t)


def _layer0_front(x, p, tag):
    m = x.shape[0]
    w1, b1 = p["conv_w_pw1"][0], p["conv_b_pw1"][0]
    return _mm(x, [(w1, 0), (w1, D_MODEL)], D_MODEL, tm=_row_tile(m, 1024), tn=512, norm_g=p["norm_mix"][0],
               biases=[(b1, 0), (b1, D_MODEL)], act="glu", name="pw1_glu_" + tag)


def _layer0_back(x, c_act, p, seq, tag):
    m = x.shape[0]
    tm = _row_tile(m, 1024)
    h1 = _mm(c_act, [(p["conv_w_pw2"][0], 0)], D_MODEL, tm=tm, tn=512, biases=[(p["conv_b_pw2"][0], 0)], residual=x,
             name="pw2_" + tag)
    d_ff = p["ffn_w_gate"].shape[2]
    a = _mm(h1, [(p["ffn_w_gate"][0], 0), (p["ffn_w_up"][0], 0)], d_ff, tm=tm, tn=256, norm_g=p["norm_ffn"][0],
            act="swiglu", out_dtype=BF16, name="ffn_up_" + tag)
    h2 = _mm(a, [(p["ffn_w_down"][0], 0)], D_MODEL, tm=_row_tile(m, 512), tn=512, residual=h1, name="ffn_down_" + tag)
    tmt = _row_tile(seq, 1024)
    paged_t = _mm_t(h2, p["w_kv_t"][:PAGED_ROWS], p["norm_kv"], seq=seq, tm=tmt, tn=512, name="kv_paged_" + tag)
    win_t = _mm_t(h2, p["w_kv_t"][PAGED_ROWS:], p["norm_kv"], seq=seq, tm=tmt, tn=512, name="kv_win_" + tag)
    proj = _mm(h2, [(p["w_in_pad"], 0)], p["w_in_pad"].shape[1], tm=tm, tn=384, norm_g=p["norm_mix"][1],
               name="q_proj_" + tag)
    return h2, paged_t, win_t, proj


def _layer1_back(o, w_out, h2, p, tag):
    m = h2.shape[0]
    tm = _row_tile(m, 1024)
    h3 = _mm(o, [(w_out, 0)], D_MODEL, tm=tm, tn=512, residual=h2, name="attn_out_" + tag)
    if m >= GROUPED_MOE_MIN_TOKENS:
        return _moe_grouped(h3, p, tag)
    xn, comb = _router(h3, p["norm_ffn"][1], p["w_router_pad"], tm, "router_" + tag)
    return _moe(xn, comb, h3, p["moe_w_gate"][0], p["moe_w_up"][0], p["moe_w_down"][0], p["norm_final"], tm, 256,
                "moe_" + tag)


def kernel(x_prompt, x_sample, cache_kv, state_win_kv, state_conv, page_table, norm_mix, norm_ffn, norm_kv, norm_final, conv_w_pw1, conv_b_pw1, conv_w_dw, conv_b_dw, conv_ln_g, conv_ln_b, conv_w_pw2, conv_b_pw2, w_kv, cmp_pos, cmp_w1, cmp_w2, nsa_w_in, nsa_w_out, rel_bias, ffn_w_gate, ffn_w_up, ffn_w_down, moe_w_router, moe_w_gate, moe_w_up, moe_w_down):
    p = dict(norm_mix=norm_mix, norm_ffn=norm_ffn, norm_kv=norm_kv, norm_final=norm_final,
             conv_w_pw1=conv_w_pw1, conv_b_pw1=conv_b_pw1, conv_w_pw2=conv_w_pw2, conv_b_pw2=conv_b_pw2,
             ffn_w_gate=ffn_w_gate, ffn_w_up=ffn_w_up, ffn_w_down=ffn_w_down,
             moe_w_gate=moe_w_gate, moe_w_up=moe_w_up, moe_w_down=moe_w_down)
    p["w_kv_t"] = w_kv.T
    p["w_in_pad"], p["w_out_perm"] = _attn_weights(nsa_w_in[0], nsa_w_out[0])
    p["w_router_pad"] = jnp.pad(moe_w_router[0], ((0, 0), (0, LANES - N_EXPERTS)))
    pe_t, w1t, w2pad = _compress_params(cmp_pos, cmp_w1, cmp_w2)
    conv_args = (conv_w_dw[0], conv_b_dw[0], conv_ln_g[0], conv_ln_b[0])
    n_state = CONV_WIDTH - 1

    bp, tp, _ = x_prompt.shape
    mp = bp * tp
    xp = x_prompt.reshape(mp, D_MODEL)
    glu = _layer0_front(xp, p, "p").reshape(bp, tp, D_MODEL)
    c_act = _conv_ln_swish(glu, jnp.zeros((1, CONV_HALO, D_MODEL), F32), 256, 32, *conv_args, "conv_p")
    h2, paged_t, win_t, proj = _layer0_back(xp, c_act.reshape(mp, D_MODEL), p, tp, "p")
    ppb = tp // PAGE
    cmp_p = _compress(paged_t, jnp.arange(bp * ppb, dtype=jnp.int32), lambda pid: (pid // ppb, 0, pid % ppb),
                      pe_t, w1t, w2pad, _row_tile(bp * ppb, 32), "compress_p").reshape(bp, tp // BLOCK, -1)
    o = _prompt_attention(rel_bias, proj.reshape(bp, tp, -1), cmp_p, paged_t, win_t, tq=256)
    y_p = _layer1_back(o.reshape(mp, D_MODEL), nsa_w_out[0], h2, p, "p").reshape(bp, tp, D_MODEL)
    kv_p = _from_rows_t(paged_t, 4)
    keep_p = min(WINDOW, tp)
    win_p = _from_rows_t(win_t[:, :, tp - keep_p:], 2)
    conv_p = glu[:, tp - n_state:][None]

    bs, ts, _ = x_sample.shape
    ms = bs * ts
    n_pages = page_table.shape[1]
    assert cache_kv.shape[1] == PAGE
    pos0 = n_pages * PAGE
    xs = x_sample.reshape(ms, D_MODEL)
    glu_s = _layer0_front(xs, p, "s").reshape(bs, ts, D_MODEL)
    full_s = jnp.concatenate([state_conv[0], glu_s], axis=1)
    conv_s = full_s[:, full_s.shape[1] - n_state:][None]
    c_act_s = _conv_ln_swish(jnp.pad(glu_s, ((0, 0), (0, SROWS - ts), (0, 0))),
                             jnp.pad(state_conv[0], ((0, 0), (CONV_HALO - n_state, 0), (0, 0))),
                             SROWS, SROWS, *conv_args, "conv_s")
    h2_s, paged_ts, win_ts, proj_s = _layer0_back(xs, c_act_s[:, :ts].reshape(ms, D_MODEL), p, ms, "s")
    new_paged_t = paged_ts[0].reshape(PAGED_ROWS, bs, ts).transpose(1, 0, 2)
    new_win_t = win_ts[0].reshape(2 * HEADS_LANES, bs, ts).transpose(1, 0, 2)
    cache_t = _to_rows_t(cache_kv)
    page_ids = page_table.reshape(-1)
    cmp_s = _compress(cache_t, page_ids, lambda pid: (pid, 0, 0), pe_t, w1t, w2pad,
                      _row_tile(page_ids.shape[0], 32), "compress_s").reshape(bs, pos0 // BLOCK, -1)
    wp = state_win_kv.shape[1]
    win_all_t = jnp.concatenate([_to_rows_t(state_win_kv), new_win_t], axis=2)
    keep_s = min(WINDOW, pos0 + ts)
    win_s = _from_rows_t(win_all_t[:, :, wp + ts - keep_s:], 2)
    pad_last = lambda a, n: jnp.pad(a, ((0, 0), (0, 0), (0, n - a.shape[2])))
    q_s = jnp.pad(proj_s.reshape(bs, ts, -1), ((0, 0), (0, SROWS - ts), (0, 0)))
    part, sel = _sample_attention_a(rel_bias, q_s, cmp_s, pad_last(win_all_t, -(-(wp + ts) // LANES) * LANES),
                                    pos0=pos0, ts=ts, wp=wp)
    knew_t = pad_last(new_paged_t[:, 2 * HEADS_LANES:], PAGE)
    o_s = _sample_attention_b(rel_bias, page_ids, q_s, sel, part, knew_t, cache_t, pos0=pos0, ts=ts, n_pg=16)
    y_s = _layer1_back(o_s[:, :ts].reshape(ms, D_MODEL), p["w_out_perm"], h2_s, p, "s").reshape(bs, ts, D_MODEL)
    kv_s_out = _from_rows_t(new_paged_t, 4)
    return (y_p, y_s, kv_p, win_p, conv_p, kv_s_out, win_s, conv_s)
```
